```python
import math
import jax, jax.numpy as jnp
from jax import lax
import numpy as np

D_MODEL = 1024
BATCH = 2
SEQ = 8192
DEPTH = 1

N_META = 16
NORM_EPS = 1e-6
ATT_HEADS = 8
ATT_QK_DIM = 64
ATT_V_DIM = 2 * ATT_QK_DIM
ATT_QK_WIDTH = ATT_HEADS * 2 * ATT_QK_DIM
ATT_WIDTH = ATT_HEADS * ATT_V_DIM
Q_BLOCK = 128
DN_HEADS = 8
DN_K_DIM = 128
DN_V_DIM = 128
DN_KEY_WIDTH = DN_HEADS * DN_K_DIM
DN_WIDTH = DN_HEADS * DN_V_DIM
DN_CONV = 4
DN_CONV_CH = 2 * DN_KEY_WIDTH + DN_WIDTH
DN_CHUNK = 64
IN_SIZES = (ATT_QK_WIDTH, ATT_QK_WIDTH, ATT_WIDTH, ATT_WIDTH,
            DN_KEY_WIDTH, DN_KEY_WIDTH, DN_WIDTH, DN_WIDTH, DN_HEADS, DN_HEADS,
            D_MODEL, D_MODEL)
IN_DIM = sum(IN_SIZES)

kernel_name = 'hybrid_diffattn_gdn_block'


def rms_norm(x, w):
    xf = x.astype(jnp.float32)
    y = xf * lax.rsqrt(jnp.mean(xf * xf, axis=-1, keepdims=True) + NORM_EPS)
    return (y * w.astype(jnp.float32)).astype(x.dtype)


def l2_normalize(x):
    xf = x.astype(jnp.float32)
    return (xf * lax.rsqrt(jnp.sum(xf * xf, axis=-1, keepdims=True) + 1e-6)).astype(x.dtype)


def causal_depthwise_conv(x, w):
    K, C = w.shape
    return lax.conv_general_dilated(x, w[:, None, :].astype(x.dtype), window_strides=(1,),
                                    padding=[(K - 1, 0)], dimension_numbers=('NWC', 'WIO', 'NWC'),
                                    feature_group_count=C)


def diff_attention(q, k, v, lam):
    B, L, H, _, dh = q.shape
    nb = -(-L // Q_BLOCK)
    lq = nb * Q_BLOCK
    qp = jnp.pad(q, ((0, 0), (0, lq - L), (0, 0), (0, 0), (0, 0)))
    qb = qp.reshape(B, nb, Q_BLOCK, H, 2, dh).transpose(1, 0, 2, 3, 4, 5)
    k_pos = jnp.arange(L)
    scale = dh ** -0.5

    def one_block(args):
        q_blk, start = args
        s = jnp.einsum('bqhmd,bkhmd->bhmqk', q_blk, k).astype(jnp.float32) * scale
        q_pos = start + jnp.arange(Q_BLOCK)
        causal = k_pos[None, :] <= q_pos[:, None]
        p = jax.nn.softmax(jnp.where(causal, s, -jnp.inf), axis=-1)
        pd = p[:, :, 0] - lam * p[:, :, 1]
        return jnp.einsum('bhqk,bkhd->bqhd', pd.astype(v.dtype), v)

    out = lax.map(one_block, (qb, jnp.arange(nb) * Q_BLOCK))
    return out.transpose(1, 0, 2, 3, 4).reshape(B, lq, H, -1)[:, :L]


def gated_delta_chunked(q, k, v, beta, g):
    B, Lp, H, dk = q.shape
    dv = v.shape[-1]
    n = Lp // DN_CHUNK

    def chunks(t):
        return t.reshape(B, n, DN_CHUNK, H, -1).transpose(0, 3, 1, 2, 4).astype(jnp.float32)

    q, k, v = chunks(q), chunks(k), chunks(v)
    beta = chunks(beta[..., None])[..., 0]
    gc = jnp.cumsum(chunks(g[..., None])[..., 0], axis=-1)
    incl = jnp.tril(jnp.ones((DN_CHUNK, DN_CHUNK), dtype=bool))
    strict = jnp.tril(jnp.ones((DN_CHUNK, DN_CHUNK), dtype=bool), -1)
    diff = gc[..., :, None] - gc[..., None, :]
    decay = jnp.where(incl, jnp.exp(jnp.where(incl, diff, 0.0)), 0.0)
    kb = k * beta[..., None]
    m = jnp.where(strict, jnp.einsum('bhnid,bhnjd->bhnij', kb, k) * decay, 0.0)
    eye = jnp.eye(DN_CHUNK, dtype=jnp.float32)
    t_inv = lax.linalg.triangular_solve(m + eye, jnp.broadcast_to(eye, m.shape), left_side=True,
                                        lower=True, unit_diagonal=True)
    u = jnp.einsum('bhnij,bhnjd->bhnid', t_inv, v * beta[..., None])
    w = jnp.einsum('bhnij,bhnjd->bhnid', t_inv, kb * jnp.exp(gc)[..., None])
    a_intra = jnp.einsum('bhnid,bhnjd->bhnij', q, k) * decay
    q_dec = q * jnp.exp(gc)[..., None]
    k_dec = k * jnp.exp(gc[..., -1:] - gc)[..., None]
    chunk_decay = jnp.exp(gc[..., -1])

    def step(state, inp):
        u_c, w_c, a_c, qd_c, kd_c, cd_c = inp
        v_new = u_c - jnp.einsum('bhcd,bhde->bhce', w_c, state)
        o_c = jnp.einsum('bhcd,bhde->bhce', qd_c, state) + jnp.einsum('bhij,bhje->bhie', a_c, v_new)
        state = state * cd_c[..., None, None] + jnp.einsum('bhcd,bhce->bhde', kd_c, v_new)
        return state, o_c

    xs = tuple(jnp.moveaxis(t, 2, 0) for t in (u, w, a_intra, q_dec, k_dec, chunk_decay))
    s0 = jnp.zeros((B, H, dk, dv), jnp.float32)
    _, o = lax.scan(step, s0, xs)
    return o.transpose(1, 0, 3, 2, 4).reshape(B, Lp, H, dv)


def hybrid_layer(h, layer_idx, norm_w, w_in, lambda_q1, lambda_k1, lambda_q2, lambda_k2,
                 attn_norm_w, conv_w, a_log, dt_bias, dn_norm_w, w_branch_attn, w_branch_delta, w_out):
    B, L, _ = h.shape
    hn = rms_norm(h, norm_w)
    proj = hn @ w_in.astype(h.dtype)
    offs = np.cumsum(IN_SIZES)[:-1].tolist()
    aq, ak, av, az, dq, dk, dv, dz, db, da, ga, gd = jnp.split(proj, offs, axis=-1)

    lam_init = 0.8 - 0.6 * math.exp(-0.3 * layer_idx)
    f32 = jnp.float32
    lam = (jnp.exp(jnp.sum(lambda_q1.astype(f32) * lambda_k1.astype(f32)))
           - jnp.exp(jnp.sum(lambda_q2.astype(f32) * lambda_k2.astype(f32))) + lam_init)
    qa = aq.reshape(B, L, ATT_HEADS, 2, ATT_QK_DIM)
    ka = ak.reshape(B, L, ATT_HEADS, 2, ATT_QK_DIM)
    va = av.reshape(B, L, ATT_HEADS, ATT_V_DIM)
    o_a = diff_attention(qa, ka, va, lam)
    o_a = (rms_norm(o_a, attn_norm_w) * (1.0 - lam_init)).reshape(B, L, ATT_WIDTH)
    y_a = (o_a * jax.nn.silu(az)) @ w_branch_attn.astype(h.dtype)

    qkv = jnp.concatenate([dq, dk, dv], axis=-1)
    qkv = jax.nn.silu(causal_depthwise_conv(qkv, conv_w))
    cq, ck, cv = jnp.split(qkv, [DN_KEY_WIDTH, 2 * DN_KEY_WIDTH], axis=-1)
    q = l2_normalize(cq.reshape(B, L, DN_HEADS, DN_K_DIM)) * (DN_K_DIM ** -0.5)
    k = l2_normalize(ck.reshape(B, L, DN_HEADS, DN_K_DIM))
    v = cv.reshape(B, L, DN_HEADS, DN_V_DIM)
    beta = jax.nn.sigmoid(db.astype(f32))
    g = -jnp.exp(a_log.astype(f32)) * jax.nn.softplus(da.astype(f32) + dt_bias.astype(f32))
    pad = (-N_META) % DN_CHUNK

    def pad_front(t):
        return jnp.pad(t, [(0, 0), (pad, 0)] + [(0, 0)] * (t.ndim - 2))

    o_d = gated_delta_chunked(pad_front(q), pad_front(k), pad_front(v), pad_front(beta), pad_front(g))
    o_d = rms_norm(o_d[:, pad:].astype(h.dtype), dn_norm_w).reshape(B, L, DN_WIDTH)
    y_d = (o_d * jax.nn.silu(dz)) @ w_branch_delta.astype(h.dtype)

    merged = jax.nn.sigmoid(ga) * y_a + jax.nn.sigmoid(gd) * y_d
    return merged @ w_out.astype(h.dtype)


def setup_inputs(seed: int = 0) -> dict:
    key = jax.random.key(seed)
    ks = jax.random.split(key, 20)
    nrm = jax.random.normal
    dt = jnp.exp(jax.random.uniform(ks[8], (DEPTH, DN_HEADS)) * (math.log(0.1) - math.log(0.001))
                 + math.log(0.001))
    return {
        'x': nrm(ks[0], (BATCH, SEQ, D_MODEL), jnp.float32),
        'meta_tokens': nrm(ks[1], (N_META, D_MODEL), jnp.float32),
        'norm_w': 1.0 + 0.02 * nrm(ks[2], (DEPTH, D_MODEL), jnp.float32),
        'w_in': nrm(ks[3], (DEPTH, D_MODEL, IN_DIM), jnp.float32) * D_MODEL ** -0.5,
        'lambda_q1': 0.1 * nrm(ks[4], (DEPTH, ATT_QK_DIM), jnp.float32),
        'lambda_k1': 0.1 * nrm(ks[5], (DEPTH, ATT_QK_DIM), jnp.float32),
        'lambda_q2': 0.1 * nrm(ks[6], (DEPTH, ATT_QK_DIM), jnp.float32),
        'lambda_k2': 0.1 * nrm(ks[7], (DEPTH, ATT_QK_DIM), jnp.float32),
        'attn_norm_w': 1.0 + 0.02 * nrm(ks[9], (DEPTH, ATT_V_DIM), jnp.float32),
        'conv_w': nrm(ks[10], (DEPTH, DN_CONV, DN_CONV_CH), jnp.float32) * DN_CONV ** -0.5,
        'a_log': jnp.log(jax.random.uniform(ks[11], (DEPTH, DN_HEADS), jnp.float32, 1.0, 16.0)),
        'dt_bias': dt + jnp.log(-jnp.expm1(-dt)),
        'dn_norm_w': 1.0 + 0.02 * nrm(ks[12], (DEPTH, DN_V_DIM), jnp.float32),
        'w_branch_attn': nrm(ks[13], (DEPTH, ATT_WIDTH, D_MODEL), jnp.float32) * ATT_WIDTH ** -0.5,
        'w_branch_delta': nrm(ks[14], (DEPTH, DN_WIDTH, D_MODEL), jnp.float32) * DN_WIDTH ** -0.5,
        'w_out': nrm(ks[15], (DEPTH, D_MODEL, D_MODEL), jnp.float32) * D_MODEL ** -0.5,
        'final_norm_w': 1.0 + 0.02 * nrm(ks[16], (D_MODEL,), jnp.float32),
    }


def reference(x, meta_tokens, norm_w, w_in, lambda_q1, lambda_k1, lambda_q2, lambda_k2,
              attn_norm_w, conv_w, a_log, dt_bias, dn_norm_w, w_branch_attn, w_branch_delta,
              w_out, final_norm_w):
    B = x.shape[0]
    meta = jnp.broadcast_to(meta_tokens[None].astype(x.dtype), (B, N_META, D_MODEL))
    h = jnp.concatenate([meta, x], axis=1)
    for l in range(DEPTH):
        h = h + hybrid_layer(h, l, norm_w[l], w_in[l], lambda_q1[l], lambda_k1[l], lambda_q2[l],
                             lambda_k2[l], attn_norm_w[l], conv_w[l], a_log[l], dt_bias[l],
                             dn_norm_w[l], w_branch_attn[l], w_branch_delta[l], w_out[l])
    return rms_norm(h, final_norm_w)[:, N_META:]
```

```python
import functools
import math

import jax
import jax.numpy as jnp
from jax import lax
from jax.experimental import pallas as pl
from jax.experimental.pallas import tpu as pltpu

D_MODEL = 1024
SEQ = 8192
N_META = 16
NORM_EPS = 1e-6
HEADS = 8
HEAD_DIM = 128
QK_DIM = 64
CONV_K = 4
CHUNK = 64
LANES = 128
LP = 8448
FRONT = LP - SEQ - N_META
N_MAIN = 10 * D_MODEL
NEG = -1e30

C_AQ, C_AK, C_AV, C_AZ, C_DQ, C_DK, C_DV, C_DZ, C_GA, C_GD = range(10)

F32 = jnp.float32
BF16 = jnp.bfloat16


def _rms(x, w):
    return x * lax.rsqrt(jnp.mean(x * x, axis=-1, keepdims=True) + NORM_EPS) * w


def _dot(a, b):
    return jnp.dot(a, b, preferred_element_type=F32)


def _dot_nt(a, b):
    return lax.dot_general(a, b, (((1,), (1,)), ((), ())), preferred_element_type=F32)


def _dot_tn(a, b):
    return lax.dot_general(a, b, (((0,), (0,)), ((), ())), preferred_element_type=F32)


IN_TM = 768
IN_TN = 1024


def _inproj_kernel(h_ref, nw_ref, w_ref, wg_ref, wgt_ref, p_ref, gcol_ref, grow_ref, hn_sc):
    @pl.when(pl.program_id(1) == 0)
    def _():
        hn = _rms(h_ref[...], nw_ref[...]).astype(BF16)
        hn_sc[...] = hn
        gcol_ref[...] = _dot(hn, wg_ref[...])
        grow_ref[...] = _dot_nt(wgt_ref[...], hn)

    p_ref[...] = _dot(hn_sc[...], w_ref[...]).astype(BF16)


def _inproj(hp2, norm_w, w_main, w_gate, w_gate_t):
    rows = hp2.shape[0]
    grid = (rows // IN_TM, N_MAIN // IN_TN)
    return pl.pallas_call(
        _inproj_kernel,
        grid=grid,
        in_specs=[
            pl.BlockSpec((IN_TM, D_MODEL), lambda i, j: (i, 0)),
            pl.BlockSpec((1, D_MODEL), lambda i, j: (0, 0)),
            pl.BlockSpec((D_MODEL, IN_TN), lambda i, j: (0, j)),
            pl.BlockSpec((D_MODEL, LANES), lambda i, j: (0, 0)),
            pl.BlockSpec((2 * HEADS, D_MODEL), lambda i, j: (0, 0)),
        ],
        out_specs=[
            pl.BlockSpec((IN_TM, IN_TN), lambda i, j: (i, j)),
            pl.BlockSpec((IN_TM, LANES), lambda i, j: (i, 0)),
            pl.BlockSpec((2 * HEADS, IN_TM), lambda i, j: (0, i)),
        ],
        out_shape=[
            jax.ShapeDtypeStruct((rows, N_MAIN), BF16),
            jax.ShapeDtypeStruct((rows, LANES), F32),
            jax.ShapeDtypeStruct((2 * HEADS, rows), F32),
        ],
        scratch_shapes=[pltpu.VMEM((IN_TM, D_MODEL), BF16)],
        compiler_params=pltpu.CompilerParams(
            dimension_semantics=("parallel", "arbitrary"),
            vmem_limit_bytes=40 * 1024 * 1024),
        name="inproj",
    )(hp2, norm_w, w_main, w_gate, w_gate_t)


AT_T = 256


def _attn_kernel(lam_ref, q_ref, k_ref, v_ref, z_ref, nw_ref, o_ref, m_sc, l_sc, acc_sc, *,
                 lam_init):
    i = pl.program_id(2)
    t = AT_T
    q = q_ref[...] * jnp.asarray(QK_DIM ** -0.5, BF16)
    lane = lax.broadcasted_iota(jnp.int32, (t, 2 * QK_DIM), 1)
    zero = jnp.zeros_like(q)
    q2 = jnp.concatenate([jnp.where(lane < QK_DIM, q, zero), jnp.where(lane >= QK_DIM, q, zero)], axis=0)

    m_sc[...] = jnp.full(m_sc.shape, NEG, F32)
    l_sc[...] = jnp.zeros(l_sc.shape, F32)
    acc_sc[...] = jnp.zeros(acc_sc.shape, F32)

    def block(j, masked):
        start = pl.multiple_of(j * t, t)
        kj = k_ref[pl.ds(start, t), :]
        vj = v_ref[pl.ds(start, t), :]
        s = _dot_nt(q2, kj)
        if masked:
            r2 = lax.broadcasted_iota(jnp.int32, (2 * t, t), 0)
            row = i * t + jnp.where(r2 >= t, r2 - t, r2)
            col = j * t + lax.broadcasted_iota(jnp.int32, (2 * t, t), 1)
            s = jnp.where((col <= row) & (col >= FRONT), s, NEG)
        m_prev = m_sc[...]
        m_new = jnp.maximum(m_prev, jnp.max(s, axis=-1, keepdims=True))
        alpha = jnp.exp(m_prev - m_new)
        p = jnp.exp(s - m_new)
        l_sc[...] = alpha * l_sc[...] + jnp.sum(p, axis=-1, keepdims=True)
        acc_sc[...] = alpha * acc_sc[...] + _dot(p.astype(BF16), vj)
        m_sc[...] = m_new

    block(0, True)

    def mid(j, c):
        block(j, False)
        return c

    lax.fori_loop(1, i, mid, 0)

    @pl.when(i > 0)
    def _():
        block(i, True)

    lp = lam_ref[...]
    lam = (jnp.exp(jnp.sum(lp[0:1] * lp[1:2], axis=-1, keepdims=True))
           - jnp.exp(jnp.sum(lp[2:3] * lp[3:4], axis=-1, keepdims=True)) + lam_init)
    o_all = acc_sc[...] / l_sc[...]
    o = o_all[:t] - lam * o_all[t:]
    o = _rms(o, nw_ref[...]) * (1.0 - lam_init)
    z = z_ref[...].astype(F32)
    o_ref[...] = (o * (z * jax.nn.sigmoid(z))).astype(BF16)


def _attention(proj3, lam_params, attn_norm_w, lam_init):
    b = proj3.shape[0]
    t = AT_T
    hb = D_MODEL // HEAD_DIM
    kern = functools.partial(_attn_kernel, lam_init=lam_init)
    return pl.pallas_call(
        kern,
        grid=(b, HEADS, LP // t),
        in_specs=[
            pl.BlockSpec((4, QK_DIM), lambda b_, h, i: (0, 0)),
            pl.BlockSpec((None, t, HEAD_DIM), lambda b_, h, i: (b_, i, C_AQ * hb + h)),
            pl.BlockSpec((None, LP, HEAD_DIM), lambda b_, h, i: (b_, 0, C_AK * hb + h)),
            pl.BlockSpec((None, LP, HEAD_DIM), lambda b_, h, i: (b_, 0, C_AV * hb + h)),
            pl.BlockSpec((None, t, HEAD_DIM), lambda b_, h, i: (b_, i, C_AZ * hb + h)),
            pl.BlockSpec((1, HEAD_DIM), lambda b_, h, i: (0, 0)),
        ],
        out_specs=pl.BlockSpec((None, t, HEAD_DIM), lambda b_, h, i: (b_, i, h)),
        out_shape=jax.ShapeDtypeStruct((b, LP, D_MODEL), BF16),
        scratch_shapes=[
            pltpu.VMEM((2 * t, 1), F32),
            pltpu.VMEM((2 * t, 1), F32),
            pltpu.VMEM((2 * t, HEAD_DIM), F32),
        ],
        compiler_params=pltpu.CompilerParams(
            dimension_semantics=("parallel", "parallel", "arbitrary"),
            vmem_limit_bytes=40 * 1024 * 1024),
        name="diff_attn",
    )(lam_params, proj3, proj3, proj3, proj3, attn_norm_w)


DN_T = 2 * CHUNK
HALO = 8


def _split3(x):
    hi = x.astype(BF16)
    r1 = x - hi.astype(F32)
    mid = r1.astype(BF16)
    lo = (r1 - mid.astype(F32)).astype(BF16)
    return hi, mid, lo


def _dn_prep_kernel(dq_ref, dk_ref, dv_ref, hq_ref, hk_ref, hv_ref, cw_ref, gcol_ref, grow_ref,
                    gpr_ref, gpc_ref, u_ref, w_ref, qd_ref, kd_ref, a_ref, cd_ref, xs_sc):
    i = pl.program_id(1)
    t = DN_T
    width = D_MODEL
    for idx, (t_ref, h_ref) in enumerate(((dq_ref, hq_ref), (dk_ref, hk_ref), (dv_ref, hv_ref))):
        halo = h_ref[...].astype(F32)
        xs_sc[0:HALO, idx * width:(idx + 1) * width] = jnp.where(i == 0, jnp.zeros_like(halo), halo)
        xs_sc[HALO:HALO + t, idx * width:(idx + 1) * width] = t_ref[...].astype(F32)
    cw = cw_ref[...]
    y = cw[0:1] * xs_sc[pl.ds(HALO - CONV_K + 1, t), :]
    for j in range(1, CONV_K):
        y = y + cw[j:j + 1] * xs_sc[pl.ds(HALO - CONV_K + 1 + j, t), :]
    y = y * jax.nn.sigmoid(y)

    r = lax.broadcasted_iota(jnp.int32, (t, t), 0)
    c = lax.broadcasted_iota(jnp.int32, (t, t), 1)
    same = (r >= CHUNK) == (c >= CHUNK)
    incl = same & (c <= r)
    strict = same & (c < r)
    tri = jnp.where(incl, 1.0, 0.0).astype(BF16)
    tri_t = jnp.where(same & (r <= c), 1.0, 0.0).astype(BF16)
    blk = jnp.where(same, 1.0, 0.0).astype(BF16)
    eye = jnp.where(r == c, 1.0, 0.0).astype(F32)

    lane = lax.broadcasted_iota(jnp.int32, (t, LANES), 1)
    rowi = lax.broadcasted_iota(jnp.int32, (t, LANES), 0) + i * t
    gcol = gcol_ref[...]
    gpr = gpr_ref[...]
    is_g = (lane >= HEADS) & (lane < 2 * HEADS)
    live_c = rowi >= FRONT
    beta_c = jnp.where(live_c, jax.nn.sigmoid(gcol), 0.0)
    xg = gcol + gpr[1:2]
    sp = jnp.maximum(xg, 0.0) + jnp.log1p(jnp.exp(-jnp.abs(xg)))
    g_c = jnp.where(is_g & live_c, -jnp.exp(gpr[0:1]) * sp, 0.0)
    g3 = _split3(g_c)
    gc_c = _dot(tri, g3[0]) + _dot(tri, g3[1]) + _dot(tri, g3[2])
    gl_c = _dot(blk, g3[0]) + _dot(blk, g3[1]) + _dot(blk, g3[2])
    sub = lax.broadcasted_iota(jnp.int32, (2 * HEADS, t), 0)
    coli = lax.broadcasted_iota(jnp.int32, (2 * HEADS, t), 1) + i * t
    xr = grow_ref[...] + gpc_ref[1]
    spr = jnp.maximum(xr, 0.0) + jnp.log1p(jnp.exp(-jnp.abs(xr)))
    g_r = jnp.where((sub >= HEADS) & (coli >= FRONT), -jnp.exp(gpc_ref[0]) * spr, 0.0)
    gr3 = _split3(g_r)
    gc_r = _dot(gr3[0], tri_t) + _dot(gr3[1], tri_t) + _dot(gr3[2], tri_t)

    cd_ref[...] = jnp.exp(gl_c)

    for h in range(HEADS):
        sl = slice(h * HEAD_DIM, (h + 1) * HEAD_DIM)
        qh = y[:, sl]
        kh = y[:, width + h * HEAD_DIM: width + (h + 1) * HEAD_DIM]
        vh = y[:, 2 * width + h * HEAD_DIM: 2 * width + (h + 1) * HEAD_DIM]
        qh = qh * lax.rsqrt(jnp.sum(qh * qh, axis=-1, keepdims=True) + 1e-6) * (HEAD_DIM ** -0.5)
        kh = kh * lax.rsqrt(jnp.sum(kh * kh, axis=-1, keepdims=True) + 1e-6)
        beta = beta_c[:, h:h + 1]
        gcc = gc_c[:, HEADS + h:HEADS + h + 1]
        glc = gl_c[:, HEADS + h:HEADS + h + 1]
        gcr = gc_r[HEADS + h:HEADS + h + 1, :]
        diff = gcc - gcr
        dec = jnp.where(incl, jnp.exp(jnp.where(incl, diff, 0.0)), 0.0)
        kb = kh * beta
        kbf = kh.astype(BF16)
        m = jnp.where(strict, _dot_nt(kb.astype(BF16), kbf) * dec, 0.0)
        tinv = eye - m
        pw = m
        for _ in range(5):
            pwb = pw.astype(BF16)
            pw = _dot(pwb, pwb)
            tinv = tinv + _dot(tinv.astype(BF16), pw.astype(BF16))
        egc = jnp.exp(gcc)
        rhs = jnp.concatenate([vh * beta, kb * egc], axis=1).astype(BF16)
        uw = _dot(tinv.astype(BF16), rhs)
        u_ref[:, sl] = uw[:, :HEAD_DIM]
        w_ref[:, sl] = uw[:, HEAD_DIM:].astype(BF16)
        a_full = _dot_nt(qh.astype(BF16), kbf) * dec
        a_cmp = a_full + pltpu.roll(a_full, CHUNK, axis=1)
        a_ref[:, h * CHUNK:(h + 1) * CHUNK] = a_cmp[:, :CHUNK].astype(BF16)
        qd_ref[:, sl] = (qh * egc).astype(BF16)
        kd_ref[:, sl] = (kh * jnp.exp(glc - gcc)).astype(BF16)


def _dn_prep(proj3, conv_w2, gcol3, grow, gp_row, gp_col):
    b = proj3.shape[0]
    t = DN_T
    nt = LP // t
    hpb = t // HALO

    def tile(cb):
        return pl.BlockSpec((None, t, D_MODEL), lambda b_, i: (b_, i, cb))

    def halo(cb):
        return pl.BlockSpec((None, HALO, D_MODEL), lambda b_, i: (b_, jnp.maximum(i * hpb - 1, 0), cb))

    full = pl.BlockSpec((None, t, D_MODEL), lambda b_, i: (b_, i, 0))
    return pl.pallas_call(
        _dn_prep_kernel,
        grid=(b, nt),
        in_specs=[
            tile(C_DQ), tile(C_DK), tile(C_DV), halo(C_DQ), halo(C_DK), halo(C_DV),
            pl.BlockSpec((CONV_K, 3 * D_MODEL), lambda b_, i: (0, 0)),
            pl.BlockSpec((None, t, LANES), lambda b_, i: (b_, i, 0)),
            pl.BlockSpec((2 * HEADS, t), lambda b_, i: (0, b_ * nt + i)),
            pl.BlockSpec((2, LANES), lambda b_, i: (0, 0)),
            pl.BlockSpec((2, 2 * HEADS, t), lambda b_, i: (0, 0, 0)),
        ],
        out_specs=[
            full, full, full, full,
            pl.BlockSpec((None, t, HEADS * CHUNK), lambda b_, i: (b_, i, 0)),
            pl.BlockSpec((None, t, LANES), lambda b_, i: (b_, i, 0)),
        ],
        out_shape=[
            jax.ShapeDtypeStruct((b, LP, D_MODEL), F32),
            jax.ShapeDtypeStruct((b, LP, D_MODEL), BF16),
            jax.ShapeDtypeStruct((b, LP, D_MODEL), BF16),
            jax.ShapeDtypeStruct((b, LP, D_MODEL), BF16),
            jax.ShapeDtypeStruct((b, LP, HEADS * CHUNK), BF16),
            jax.ShapeDtypeStruct((b, LP, LANES), F32),
        ],
        scratch_shapes=[pltpu.VMEM((HALO + t, 3 * D_MODEL), F32)],
        compiler_params=pltpu.CompilerParams(
            dimension_semantics=("parallel", "parallel"),
            vmem_limit_bytes=40 * 1024 * 1024),
        name="dn_prep",
    )(proj3, proj3, proj3, proj3, proj3, proj3, conv_w2, gcol3, grow, gp_row, gp_col)


SC_T = 2 * CHUNK


def _dn_scan_kernel(u_ref, w_ref, qd_ref, kd_ref, a_ref, cd_ref, z_ref, nw_ref, o_ref, s_sc):
    @pl.when(pl.program_id(1) == 0)
    def _():
        s_sc[...] = jnp.zeros(s_sc.shape, F32)

    nw = nw_ref[...]
    for h in range(HEADS):
        sl = slice(h * HEAD_DIM, (h + 1) * HEAD_DIM)
        s = s_sc[h]
        for cix in range(SC_T // CHUNK):
            rs = slice(cix * CHUNK, (cix + 1) * CHUNK)
            sb = s.astype(BF16)
            wq = jnp.concatenate([w_ref[rs, sl], qd_ref[rs, sl]], axis=0)
            ws_qs = _dot(wq, sb)
            v_new = u_ref[rs, sl] - ws_qs[:CHUNK]
            vb = v_new.astype(BF16)
            o = ws_qs[CHUNK:] + _dot(a_ref[rs, h * CHUNK:(h + 1) * CHUNK], vb)
            cd = cd_ref[cix * CHUNK:cix * CHUNK + 1, HEADS + h:HEADS + h + 1]
            s = s * cd + _dot_tn(kd_ref[rs, sl], vb)
            z = z_ref[rs, sl].astype(F32)
            o_ref[rs, sl] = (_rms(o, nw) * (z * jax.nn.sigmoid(z))).astype(BF16)
        s_sc[h] = s


def _dn_scan(u, w, qd, kd, a, cd, proj3, dn_norm_w):
    b = u.shape[0]
    t = SC_T
    full = pl.BlockSpec((None, t, D_MODEL), lambda b_, i: (b_, i, 0))
    return pl.pallas_call(
        _dn_scan_kernel,
        grid=(b, LP // t),
        in_specs=[
            full, full, full, full,
            pl.BlockSpec((None, t, HEADS * CHUNK), lambda b_, i: (b_, i, 0)),
            pl.BlockSpec((None, t, LANES), lambda b_, i: (b_, i, 0)),
            pl.BlockSpec((None, t, D_MODEL), lambda b_, i: (b_, i, C_DZ)),
            pl.BlockSpec((1, HEAD_DIM), lambda b_, i: (0, 0)),
        ],
        out_specs=full,
        out_shape=jax.ShapeDtypeStruct((b, LP, D_MODEL), BF16),
        scratch_shapes=[pltpu.VMEM((HEADS, HEAD_DIM, HEAD_DIM), F32)],
        compiler_params=pltpu.CompilerParams(
            dimension_semantics=("parallel", "arbitrary"),
            vmem_limit_bytes=40 * 1024 * 1024),
        name="dn_scan",
    )(u, w, qd, kd, a, cd, proj3, dn_norm_w)


OUT_TM = 768


def _merge_kernel(xa_ref, xd_ref, ga_ref, gd_ref, h_ref, wa_ref, wd_ref, wo_ref, nw_ref, o_ref):
    ya = _dot(xa_ref[...], wa_ref[...])
    yd = _dot(xd_ref[...], wd_ref[...])
    merged = (jax.nn.sigmoid(ga_ref[...].astype(F32)) * ya
              + jax.nn.sigmoid(gd_ref[...].astype(F32)) * yd)
    out = h_ref[...] + _dot(merged.astype(BF16), wo_ref[...])
    o_ref[...] = _rms(out, nw_ref[...])


def _merge(xa, xd, proj2, hp2, wa, wd, wo, final_norm_w):
    rows = hp2.shape[0]
    tm = OUT_TM
    row = pl.BlockSpec((tm, D_MODEL), lambda i: (i, 0))
    wspec = pl.BlockSpec((D_MODEL, D_MODEL), lambda i: (0, 0))
    return pl.pallas_call(
        _merge_kernel,
        grid=(rows // tm,),
        in_specs=[
            row, row,
            pl.BlockSpec((tm, D_MODEL), lambda i: (i, C_GA)),
            pl.BlockSpec((tm, D_MODEL), lambda i: (i, C_GD)),
            row, wspec, wspec, wspec,
            pl.BlockSpec((1, D_MODEL), lambda i: (0, 0)),
        ],
        out_specs=row,
        out_shape=jax.ShapeDtypeStruct((rows, D_MODEL), F32),
        compiler_params=pltpu.CompilerParams(
            dimension_semantics=("parallel",),
            vmem_limit_bytes=48 * 1024 * 1024),
        name="merge_out",
    )(xa, xd, proj2, proj2, hp2, wa, wd, wo, final_norm_w)


def kernel(x, meta_tokens, norm_w, w_in, lambda_q1, lambda_k1, lambda_q2, lambda_k2, attn_norm_w,
           conv_w, a_log, dt_bias, dn_norm_w, w_branch_attn, w_branch_delta, w_out, final_norm_w):
    b = x.shape[0]
    assert x.shape == (b, SEQ, D_MODEL) and norm_w.shape[0] == 1
    layer = 0
    lam_init = 0.8 - 0.6 * math.exp(-0.3 * layer)

    head = jnp.concatenate([jnp.zeros((FRONT, D_MODEL), x.dtype), meta_tokens.astype(x.dtype)], axis=0)
    hp = jnp.concatenate([jnp.broadcast_to(head[None], (b, FRONT + N_META, D_MODEL)), x], axis=1)
    hp2 = hp.reshape(b * LP, D_MODEL)

    wi = w_in[layer]
    gate0 = 8 * D_MODEL
    w_main = jnp.concatenate([wi[:, :gate0], wi[:, gate0 + 2 * HEADS:]], axis=1).astype(BF16)
    w_gate = wi[:, gate0:gate0 + 2 * HEADS]
    w_gate_p = jnp.pad(w_gate, ((0, 0), (0, LANES - 2 * HEADS))).astype(BF16)
    w_gate_t = w_gate.T.astype(BF16)
    zeros8 = jnp.zeros((HEADS,), F32)
    gp_row = jnp.stack([jnp.pad(jnp.concatenate([zeros8, a_log[layer]]), (0, LANES - 2 * HEADS)),
                        jnp.pad(jnp.concatenate([zeros8, dt_bias[layer]]), (0, LANES - 2 * HEADS))])
    gp_col = jnp.stack([jnp.concatenate([zeros8, a_log[layer]]),
                        jnp.concatenate([zeros8, dt_bias[layer]])])
    gp_col = jnp.broadcast_to(gp_col[:, :, None], (2, 2 * HEADS, DN_T)).astype(F32)
    lam_params = jnp.stack([lambda_q1[layer], lambda_k1[layer], lambda_q2[layer], lambda_k2[layer]])

    proj2, gcol, grow = _inproj(hp2, norm_w[layer][None], w_main, w_gate_p, w_gate_t)
    proj3 = proj2.reshape(b, LP, N_MAIN)

    xa = _attention(proj3, lam_params, attn_norm_w[layer][None], lam_init)

    u, w, qd, kd, a, cd = _dn_prep(proj3, conv_w[layer], gcol.reshape(b, LP, LANES), grow, gp_row, gp_col)
    xd = _dn_scan(u, w, qd, kd, a, cd, proj3, dn_norm_w[layer][None])

    out = _merge(xa.reshape(b * LP, D_MODEL), xd.reshape(b * LP, D_MODEL), proj2, hp2,
                 w_branch_attn[layer].astype(BF16), w_branch_delta[layer].astype(BF16),
                 w_out[layer].astype(BF16), final_norm_w[None])
    return out.reshape(b, LP, D_MODEL)[:, FRONT + N_META:]
```

```python
import functools
import math

import jax
import jax.numpy as jnp
from jax import lax
from jax.experimental import pallas as pl
from jax.experimental.pallas import tpu as pltpu

D_MODEL = 1024
SEQ = 8192
N_META = 16
NORM_EPS = 1e-6
HEADS = 8
HEAD_DIM = 128
QK_DIM = 64
CONV_K = 4
CHUNK = 64
LANES = 128
LP = 8448
FRONT = LP - SEQ - N_META
N_MAIN = 10 * D_MODEL
NEG = -1e30

C_AQ, C_AK, C_AV, C_AZ, C_DQ, C_DK, C_DV, C_DZ, C_GA, C_GD = range(10)

F32 = jnp.float32
BF16 = jnp.bfloat16


def _rms(x, w):
    return x * lax.rsqrt(jnp.mean(x * x, axis=-1, keepdims=True) + NORM_EPS) * w


def _dot(a, b):
    return jnp.dot(a, b, preferred_element_type=F32)


def _dot_nt(a, b):
    return lax.dot_general(a, b, (((1,), (1,)), ((), ())), preferred_element_type=F32)


def _dot_tn(a, b):
    return lax.dot_general(a, b, (((0,), (0,)), ((), ())), preferred_element_type=F32)


IN_TM = 768
IN_TN = 1024


def _inproj_kernel(h_ref, nw_ref, w_ref, wg_ref, wgt_ref, p_ref, gcol_ref, grow_ref, hn_sc):
    @pl.when(pl.program_id(1) == 0)
    def _():
        hn = _rms(h_ref[...], nw_ref[...]).astype(BF16)
        hn_sc[...] = hn
        gcol_ref[...] = _dot(hn, wg_ref[...])
        grow_ref[...] = _dot_nt(wgt_ref[...], hn)

    p_ref[...] = _dot(hn_sc[...], w_ref[...]).astype(BF16)


def _inproj(hp2, norm_w, w_main, w_gate, w_gate_t):
    rows = hp2.shape[0]
    grid = (rows // IN_TM, N_MAIN // IN_TN)
    return pl.pallas_call(
        _inproj_kernel,
        grid=grid,
        in_specs=[
            pl.BlockSpec((IN_TM, D_MODEL), lambda i, j: (i, 0)),
            pl.BlockSpec((1, D_MODEL), lambda i, j: (0, 0)),
            pl.BlockSpec((D_MODEL, IN_TN), lambda i, j: (0, j)),
            pl.BlockSpec((D_MODEL, LANES), lambda i, j: (0, 0)),
            pl.BlockSpec((2 * HEADS, D_MODEL), lambda i, j: (0, 0)),
        ],
        out_specs=[
            pl.BlockSpec((IN_TM, IN_TN), lambda i, j: (i, j)),
            pl.BlockSpec((IN_TM, LANES), lambda i, j: (i, 0)),
            pl.BlockSpec((2 * HEADS, IN_TM), lambda i, j: (0, i)),
        ],
        out_shape=[
            jax.ShapeDtypeStruct((rows, N_MAIN), BF16),
            jax.ShapeDtypeStruct((rows, LANES), F32),
            jax.ShapeDtypeStruct((2 * HEADS, rows), F32),
        ],
        scratch_shapes=[pltpu.VMEM((IN_TM, D_MODEL), BF16)],
        compiler_params=pltpu.CompilerParams(
            dimension_semantics=("parallel", "arbitrary"),
            vmem_limit_bytes=40 * 1024 * 1024),
        name="inproj",
    )(hp2, norm_w, w_main, w_gate, w_gate_t)


AT_TQ = 256
AT_TK = 1024
AT_LK = -(-LP // AT_TK) * AT_TK
AT_SHIFT_MAX = 30.0


def _attn_kernel(lam_ref, q_ref, k_ref, v_ref, z_ref, nw_ref, o_ref, kx_sc, vx_sc, kmax_sc, m_sc,
                 acc_sc, *, lam_init):
    i = pl.program_id(2)
    tq, tk = AT_TQ, AT_TK
    nc = tk // LANES

    @pl.when(i == 0)
    def _():
        k = k_ref[...]
        kx_sc[0:LP, 0:HEAD_DIM] = k
        rowi = lax.broadcasted_iota(jnp.int32, (LP, LANES), 0)
        ln = lax.broadcasted_iota(jnp.int32, (LP, LANES), 1)
        ext = jnp.where(ln == 0, 1.0, jnp.where((ln == 1) & (rowi < FRONT), NEG, 0.0))
        kx_sc[0:LP, HEAD_DIM:] = ext.astype(BF16)
        vx_sc[0:LP, 0:HEAD_DIM] = v_ref[...]
        vx_sc[0:LP, HEAD_DIM:] = jnp.ones((LP, HEAD_DIM), BF16)
        if AT_LK > LP:
            kx_sc[LP:, :] = jnp.zeros((AT_LK - LP, 2 * HEAD_DIM), BF16)
            vx_sc[LP:, :] = jnp.zeros((AT_LK - LP, 2 * HEAD_DIM), BF16)
        kf = k.astype(F32)
        ksq = kf * kf
        first = ln < QK_DIM
        n1 = jnp.max(jnp.sum(jnp.where(first, ksq, 0.0), axis=-1, keepdims=True), axis=0, keepdims=True)
        n2 = jnp.max(jnp.sum(jnp.where(first, 0.0, ksq), axis=-1, keepdims=True), axis=0, keepdims=True)
        kmax_sc[0:1, :] = jnp.broadcast_to(jnp.sqrt(n1), (1, LANES))
        kmax_sc[1:2, :] = jnp.broadcast_to(jnp.sqrt(n2), (1, LANES))

    q = q_ref[...]
    lane = lax.broadcasted_iota(jnp.int32, (tq, LANES), 1)
    first = lane < QK_DIM
    zero = jnp.zeros_like(q)
    qf = q.astype(F32)
    qsq = qf * qf
    shift1 = jnp.sqrt(jnp.sum(jnp.where(first, qsq, 0.0), axis=-1, keepdims=True)) * kmax_sc[0:1, :]
    shift2 = jnp.sqrt(jnp.sum(jnp.where(first, 0.0, qsq), axis=-1, keepdims=True)) * kmax_sc[1:2, :]
    shift = jnp.concatenate([shift1, shift2], axis=0)
    use_shift = jnp.max(shift) <= AT_SHIFT_MAX
    q2 = jnp.concatenate([jnp.where(first, q, zero), jnp.where(first, zero, q)], axis=0)
    lane2 = lax.broadcasted_iota(jnp.int32, (2 * tq, LANES), 1)
    pad_lane = jnp.where(lane2 == 1, 1.0, 0.0)

    acc_sc[...] = jnp.zeros(acc_sc.shape, F32)
    nb = (i + tk // tq) // (tk // tq)

    def causal(s, j):
        r2 = lax.broadcasted_iota(jnp.int32, (2 * tq, tk), 0)
        row = i * tq + jnp.where(r2 >= tq, r2 - tq, r2)
        col = j * tk + lax.broadcasted_iota(jnp.int32, (2 * tq, tk), 1)
        return jnp.where(col <= row, s, NEG)

    @pl.when(use_shift)
    def _():
        qx = jnp.concatenate([q2, (pad_lane - jnp.where(lane2 == 0, shift, 0.0)).astype(BF16)], axis=1)

        def block(j, diag):
            start = pl.multiple_of(j * tk, tk)
            s = _dot_nt(qx, kx_sc[pl.ds(start, tk), :])
            if diag:
                s = causal(s, j)
            p = jnp.exp2(s).astype(BF16)
            acc_sc[...] += _dot(p, vx_sc[pl.ds(start, tk), :])

        def mid(j, c):
            block(j, False)
            return c

        lax.fori_loop(0, nb - 1, mid, 0)
        block(nb - 1, True)

    @pl.when(jnp.logical_not(use_shift))
    def _():
        qx = jnp.concatenate([q2, pad_lane.astype(BF16)], axis=1)
        m_sc[...] = jnp.full(m_sc.shape, NEG, F32)

        def block(j, diag):
            start = pl.multiple_of(j * tk, tk)
            s = _dot_nt(qx, kx_sc[pl.ds(start, tk), :])
            if diag:
                s = causal(s, j)
            chunks = [s[:, c * LANES:(c + 1) * LANES] for c in range(nc)]
            mc = chunks[0]
            for c in range(1, nc):
                mc = jnp.maximum(mc, chunks[c])
            m_prev = m_sc[...]
            m_new = jnp.maximum(m_prev, jnp.max(mc, axis=-1, keepdims=True))
            alpha = jnp.exp2(m_prev - m_new)
            p = jnp.concatenate([jnp.exp2(ch - m_new) for ch in chunks], axis=1).astype(BF16)
            pv = _dot(p, vx_sc[pl.ds(start, tk), :])
            acc_sc[...] = jnp.concatenate([alpha, alpha], axis=1) * acc_sc[...] + pv
            m_sc[...] = m_new

        def mid(j, c):
            block(j, False)
            return c

        lax.fori_loop(0, nb - 1, mid, 0)
        block(nb - 1, True)

    lp = lam_ref[...]
    lam = (jnp.exp(jnp.sum(lp[0:1] * lp[1:2], axis=-1, keepdims=True))
           - jnp.exp(jnp.sum(lp[2:3] * lp[3:4], axis=-1, keepdims=True)) + lam_init)
    acc = acc_sc[...]
    o_all = acc[:, :HEAD_DIM] / jnp.maximum(acc[:, HEAD_DIM:], 1e-37)
    o = o_all[:tq] - lam * o_all[tq:]
    o = _rms(o, nw_ref[...]) * (1.0 - lam_init)
    z = z_ref[...].astype(F32)
    o_ref[...] = (o * (z * jax.nn.sigmoid(z))).astype(BF16)


def _attention(proj3, lam_params, attn_norm_w, lam_init):
    b = proj3.shape[0]
    t = AT_TQ
    hb = D_MODEL // HEAD_DIM
    kern = functools.partial(_attn_kernel, lam_init=lam_init)
    return pl.pallas_call(
        kern,
        grid=(b, HEADS, LP // t),
        in_specs=[
            pl.BlockSpec((4, QK_DIM), lambda b_, h, i: (0, 0)),
            pl.BlockSpec((None, t, HEAD_DIM), lambda b_, h, i: (b_, i, C_AQ * hb + h)),
            pl.BlockSpec((None, LP, HEAD_DIM), lambda b_, h, i: (b_, 0, C_AK * hb + h)),
            pl.BlockSpec((None, LP, HEAD_DIM), lambda b_, h, i: (b_, 0, C_AV * hb + h)),
            pl.BlockSpec((None, t, HEAD_DIM), lambda b_, h, i: (b_, i, C_AZ * hb + h)),
            pl.BlockSpec((1, HEAD_DIM), lambda b_, h, i: (0, 0)),
        ],
        out_specs=pl.BlockSpec((None, t, HEAD_DIM), lambda b_, h, i: (b_, i, h)),
        out_shape=jax.ShapeDtypeStruct((b, LP, D_MODEL), BF16),
        scratch_shapes=[
            pltpu.VMEM((AT_LK, 2 * HEAD_DIM), BF16),
            pltpu.VMEM((AT_LK, 2 * HEAD_DIM), BF16),
            pltpu.VMEM((8, LANES), F32),
            pltpu.VMEM((2 * t, LANES), F32),
            pltpu.VMEM((2 * t, 2 * HEAD_DIM), F32),
        ],
        compiler_params=pltpu.CompilerParams(
            dimension_semantics=("parallel", "parallel", "arbitrary"),
            vmem_limit_bytes=48 * 1024 * 1024),
        name="diff_attn",
    )(lam_params, proj3, proj3, proj3, proj3, attn_norm_w)


DN_T = 2 * CHUNK
HALO = 8


def _split3(x):
    hi = x.astype(BF16)
    r1 = x - hi.astype(F32)
    mid = r1.astype(BF16)
    lo = (r1 - mid.astype(F32)).astype(BF16)
    return hi, mid, lo


def _dn_prep_kernel(dq_ref, dk_ref, dv_ref, hq_ref, hk_ref, hv_ref, cw_ref, gcol_ref, grow_ref,
                    gpr_ref, gpc_ref, u_ref, w_ref, qd_ref, kd_ref, a_ref, cd_ref, xs_sc):
    i = pl.program_id(1)
    t = DN_T
    width = D_MODEL
    for idx, (t_ref, h_ref) in enumerate(((dq_ref, hq_ref), (dk_ref, hk_ref), (dv_ref, hv_ref))):
        halo = h_ref[...].astype(F32)
        xs_sc[0:HALO, idx * width:(idx + 1) * width] = jnp.where(i == 0, jnp.zeros_like(halo), halo)
        xs_sc[HALO:HALO + t, idx * width:(idx + 1) * width] = t_ref[...].astype(F32)
    cw = cw_ref[...]
    y = cw[0:1] * xs_sc[pl.ds(HALO - CONV_K + 1, t), :]
    for j in range(1, CONV_K):
        y = y + cw[j:j + 1] * xs_sc[pl.ds(HALO - CONV_K + 1 + j, t), :]
    y = y * jax.nn.sigmoid(y)

    r = lax.broadcasted_iota(jnp.int32, (t, t), 0)
    c = lax.broadcasted_iota(jnp.int32, (t, t), 1)
    same = (r >= CHUNK) == (c >= CHUNK)
    incl = same & (c <= r)
    strict = same & (c < r)
    tri = jnp.where(incl, 1.0, 0.0).astype(BF16)
    tri_t = jnp.where(same & (r <= c), 1.0, 0.0).astype(BF16)
    blk = jnp.where(same, 1.0, 0.0).astype(BF16)
    eye = jnp.where(r == c, 1.0, 0.0).astype(F32)

    lane = lax.broadcasted_iota(jnp.int32, (t, LANES), 1)
    rowi = lax.broadcasted_iota(jnp.int32, (t, LANES), 0) + i * t
    gcol = gcol_ref[...]
    gpr = gpr_ref[...]
    is_g = (lane >= HEADS) & (lane < 2 * HEADS)
    live_c = rowi >= FRONT
    beta_c = jnp.where(live_c, jax.nn.sigmoid(gcol), 0.0)
    xg = gcol + gpr[1:2]
    sp = jnp.maximum(xg, 0.0) + jnp.log1p(jnp.exp(-jnp.abs(xg)))
    g_c = jnp.where(is_g & live_c, -jnp.exp(gpr[0:1]) * sp, 0.0)
    g3 = _split3(g_c)
    gc_c = _dot(tri, g3[0]) + _dot(tri, g3[1]) + _dot(tri, g3[2])
    gl_c = _dot(blk, g3[0]) + _dot(blk, g3[1]) + _dot(blk, g3[2])
    sub = lax.broadcasted_iota(jnp.int32, (2 * HEADS, t), 0)
    coli = lax.broadcasted_iota(jnp.int32, (2 * HEADS, t), 1) + i * t
    xr = grow_ref[...] + gpc_ref[1]
    spr = jnp.maximum(xr, 0.0) + jnp.log1p(jnp.exp(-jnp.abs(xr)))
    g_r = jnp.where((sub >= HEADS) & (coli >= FRONT), -jnp.exp(gpc_ref[0]) * spr, 0.0)
    gr3 = _split3(g_r)
    gc_r = _dot(gr3[0], tri_t) + _dot(gr3[1], tri_t) + _dot(gr3[2], tri_t)

    cd_ref[...] = jnp.exp(gl_c)

    for h in range(HEADS):
        sl = slice(h * HEAD_DIM, (h + 1) * HEAD_DIM)
        qh = y[:, sl]
        kh = y[:, width + h * HEAD_DIM: width + (h + 1) * HEAD_DIM]
        vh = y[:, 2 * width + h * HEAD_DIM: 2 * width + (h + 1) * HEAD_DIM]
        qh = qh * lax.rsqrt(jnp.sum(qh * qh, axis=-1, keepdims=True) + 1e-6) * (HEAD_DIM ** -0.5)
        kh = kh * lax.rsqrt(jnp.sum(kh * kh, axis=-1, keepdims=True) + 1e-6)
        beta = beta_c[:, h:h + 1]
        gcc = gc_c[:, HEADS + h:HEADS + h + 1]
        glc = gl_c[:, HEADS + h:HEADS + h + 1]
        gcr = gc_r[HEADS + h:HEADS + h + 1, :]
        diff = gcc - gcr
        dec = jnp.where(incl, jnp.exp(jnp.where(incl, diff, 0.0)), 0.0)
        kb = kh * beta
        kbf = kh.astype(BF16)
        m = jnp.where(strict, _dot_nt(kb.astype(BF16), kbf) * dec, 0.0)
        tinv = eye - m
        pw = m
        for _ in range(5):
            pwb = pw.astype(BF16)
            pw = _dot(pwb, pwb)
            tinv = tinv + _dot(tinv.astype(BF16), pw.astype(BF16))
        egc = jnp.exp(gcc)
        rhs = jnp.concatenate([vh * beta, kb * egc], axis=1).astype(BF16)
        uw = _dot(tinv.astype(BF16), rhs)
        u_ref[:, sl] = uw[:, :HEAD_DIM]
        w_ref[:, sl] = uw[:, HEAD_DIM:].astype(BF16)
        a_full = _dot_nt(qh.astype(BF16), kbf) * dec
        a_cmp = a_full + pltpu.roll(a_full, CHUNK, axis=1)
        a_ref[:, h * CHUNK:(h + 1) * CHUNK] = a_cmp[:, :CHUNK].astype(BF16)
        qd_ref[:, sl] = (qh * egc).astype(BF16)
        kd_ref[:, sl] = (kh * jnp.exp(glc - gcc)).astype(BF16)


def _dn_prep(proj3, conv_w2, gcol3, grow, gp_row, gp_col):
    b = proj3.shape[0]
    t = DN_T
    nt = LP // t
    hpb = t // HALO

    def tile(cb):
        return pl.BlockSpec((None, t, D_MODEL), lambda b_, i: (b_, i, cb))

    def halo(cb):
        return pl.BlockSpec((None, HALO, D_MODEL), lambda b_, i: (b_, jnp.maximum(i * hpb - 1, 0), cb))

    full = pl.BlockSpec((None, t, D_MODEL), lambda b_, i: (b_, i, 0))
    return pl.pallas_call(
        _dn_prep_kernel,
        grid=(b, nt),
        in_specs=[
            tile(C_DQ), tile(C_DK), tile(C_DV), halo(C_DQ), halo(C_DK), halo(C_DV),
            pl.BlockSpec((CONV_K, 3 * D_MODEL), lambda b_, i: (0, 0)),
            pl.BlockSpec((None, t, LANES), lambda b_, i: (b_, i, 0)),
            pl.BlockSpec((2 * HEADS, t), lambda b_, i: (0, b_ * nt + i)),
            pl.BlockSpec((2, LANES), lambda b_, i: (0, 0)),
            pl.BlockSpec((2, 2 * HEADS, t), lambda b_, i: (0, 0, 0)),
        ],
        out_specs=[
            full, full, full, full,
            pl.BlockSpec((None, t, HEADS * CHUNK), lambda b_, i: (b_, i, 0)),
            pl.BlockSpec((None, t, LANES), lambda b_, i: (b_, i, 0)),
        ],
        out_shape=[
            jax.ShapeDtypeStruct((b, LP, D_MODEL), F32),
            jax.ShapeDtypeStruct((b, LP, D_MODEL), BF16),
            jax.ShapeDtypeStruct((b, LP, D_MODEL), BF16),
            jax.ShapeDtypeStruct((b, LP, D_MODEL), BF16),
            jax.ShapeDtypeStruct((b, LP, HEADS * CHUNK), BF16),
            jax.ShapeDtypeStruct((b, LP, LANES), F32),
        ],
        scratch_shapes=[pltpu.VMEM((HALO + t, 3 * D_MODEL), F32)],
        compiler_params=pltpu.CompilerParams(
            dimension_semantics=("parallel", "parallel"),
            vmem_limit_bytes=40 * 1024 * 1024),
        name="dn_prep",
    )(proj3, proj3, proj3, proj3, proj3, proj3, conv_w2, gcol3, grow, gp_row, gp_col)


SC_T = 2 * CHUNK


def _dn_scan_kernel(u_ref, w_ref, qd_ref, kd_ref, a_ref, cd_ref, z_ref, nw_ref, o_ref, s_sc):
    @pl.when(pl.program_id(1) == 0)
    def _():
        s_sc[...] = jnp.zeros(s_sc.shape, F32)

    nw = nw_ref[...]
    for h in range(HEADS):
        sl = slice(h * HEAD_DIM, (h + 1) * HEAD_DIM)
        s = s_sc[h]
        for cix in range(SC_T // CHUNK):
            rs = slice(cix * CHUNK, (cix + 1) * CHUNK)
            sb = s.astype(BF16)
            wq = jnp.concatenate([w_ref[rs, sl], qd_ref[rs, sl]], axis=0)
            ws_qs = _dot(wq, sb)
            v_new = u_ref[rs, sl] - ws_qs[:CHUNK]
            vb = v_new.astype(BF16)
            o = ws_qs[CHUNK:] + _dot(a_ref[rs, h * CHUNK:(h + 1) * CHUNK], vb)
            cd = cd_ref[cix * CHUNK:cix * CHUNK + 1, HEADS + h:HEADS + h + 1]
            s = s * cd + _dot_tn(kd_ref[rs, sl], vb)
            z = z_ref[rs, sl].astype(F32)
            o_ref[rs, sl] = (_rms(o, nw) * (z * jax.nn.sigmoid(z))).astype(BF16)
        s_sc[h] = s


def _dn_scan(u, w, qd, kd, a, cd, proj3, dn_norm_w):
    b = u.shape[0]
    t = SC_T
    full = pl.BlockSpec((None, t, D_MODEL), lambda b_, i: (b_, i, 0))
    return pl.pallas_call(
        _dn_scan_kernel,
        grid=(b, LP // t),
        in_specs=[
            full, full, full, full,
            pl.BlockSpec((None, t, HEADS * CHUNK), lambda b_, i: (b_, i, 0)),
            pl.BlockSpec((None, t, LANES), lambda b_, i: (b_, i, 0)),
            pl.BlockSpec((None, t, D_MODEL), lambda b_, i: (b_, i, C_DZ)),
            pl.BlockSpec((1, HEAD_DIM), lambda b_, i: (0, 0)),
        ],
        out_specs=full,
        out_shape=jax.ShapeDtypeStruct((b, LP, D_MODEL), BF16),
        scratch_shapes=[pltpu.VMEM((HEADS, HEAD_DIM, HEAD_DIM), F32)],
        compiler_params=pltpu.CompilerParams(
            dimension_semantics=("parallel", "arbitrary"),
            vmem_limit_bytes=40 * 1024 * 1024),
        name="dn_scan",
    )(u, w, qd, kd, a, cd, proj3, dn_norm_w)


OUT_TM = 768


def _merge_kernel(xa_ref, xd_ref, ga_ref, gd_ref, h_ref, wa_ref, wd_ref, wo_ref, nw_ref, o_ref):
    ya = _dot(xa_ref[...], wa_ref[...])
    yd = _dot(xd_ref[...], wd_ref[...])
    merged = (jax.nn.sigmoid(ga_ref[...].astype(F32)) * ya
              + jax.nn.sigmoid(gd_ref[...].astype(F32)) * yd)
    out = h_ref[...] + _dot(merged.astype(BF16), wo_ref[...])
    o_ref[...] = _rms(out, nw_ref[...])


def _merge(xa, xd, proj2, hp2, wa, wd, wo, final_norm_w):
    rows = hp2.shape[0]
    tm = OUT_TM
    row = pl.BlockSpec((tm, D_MODEL), lambda i: (i, 0))
    wspec = pl.BlockSpec((D_MODEL, D_MODEL), lambda i: (0, 0))
    return pl.pallas_call(
        _merge_kernel,
        grid=(rows // tm,),
        in_specs=[
            row, row,
            pl.BlockSpec((tm, D_MODEL), lambda i: (i, C_GA)),
            pl.BlockSpec((tm, D_MODEL), lambda i: (i, C_GD)),
            row, wspec, wspec, wspec,
            pl.BlockSpec((1, D_MODEL), lambda i: (0, 0)),
        ],
        out_specs=row,
        out_shape=jax.ShapeDtypeStruct((rows, D_MODEL), F32),
        compiler_params=pltpu.CompilerParams(
            dimension_semantics=("parallel",),
            vmem_limit_bytes=48 * 1024 * 1024),
        name="merge_out",
    )(xa, xd, proj2, proj2, hp2, wa, wd, wo, final_norm_w)


def kernel(x, meta_tokens, norm_w, w_in, lambda_q1, lambda_k1, lambda_q2, lambda_k2, attn_norm_w,
           conv_w, a_log, dt_bias, dn_norm_w, w_branch_attn, w_branch_delta, w_out, final_norm_w):
    b = x.shape[0]
    assert x.shape == (b, SEQ, D_MODEL) and norm_w.shape[0] == 1
    layer = 0
    lam_init = 0.8 - 0.6 * math.exp(-0.3 * layer)

    head = jnp.concatenate([jnp.zeros((FRONT, D_MODEL), x.dtype), meta_tokens.astype(x.dtype)], axis=0)
    hp = jnp.concatenate([jnp.broadcast_to(head[None], (b, FRONT + N_META, D_MODEL)), x], axis=1)
    hp2 = hp.reshape(b * LP, D_MODEL)

    wi = w_in[layer]
    gate0 = 8 * D_MODEL
    q_scale = QK_DIM ** -0.5 * math.log2(math.e)
    w_main = jnp.concatenate([wi[:, :D_MODEL] * q_scale, wi[:, D_MODEL:gate0], wi[:, gate0 + 2 * HEADS:]],
                             axis=1).astype(BF16)
    w_gate = wi[:, gate0:gate0 + 2 * HEADS]
    w_gate_p = jnp.pad(w_gate, ((0, 0), (0, LANES - 2 * HEADS))).astype(BF16)
    w_gate_t = w_gate.T.astype(BF16)
    zeros8 = jnp.zeros((HEADS,), F32)
    gp_row = jnp.stack([jnp.pad(jnp.concatenate([zeros8, a_log[layer]]), (0, LANES - 2 * HEADS)),
                        jnp.pad(jnp.concatenate([zeros8, dt_bias[layer]]), (0, LANES - 2 * HEADS))])
    gp_col = jnp.stack([jnp.concatenate([zeros8, a_log[layer]]),
                        jnp.concatenate([zeros8, dt_bias[layer]])])
    gp_col = jnp.broadcast_to(gp_col[:, :, None], (2, 2 * HEADS, DN_T)).astype(F32)
    lam_params = jnp.stack([lambda_q1[layer], lambda_k1[layer], lambda_q2[layer], lambda_k2[layer]])

    proj2, gcol, grow = _inproj(hp2, norm_w[layer][None], w_main, w_gate_p, w_gate_t)
    proj3 = proj2.reshape(b, LP, N_MAIN)

    xa = _attention(proj3, lam_params, attn_norm_w[layer][None], lam_init)

    u, w, qd, kd, a, cd = _dn_prep(proj3, conv_w[layer], gcol.reshape(b, LP, LANES), grow, gp_row, gp_col)
    xd = _dn_scan(u, w, qd, kd, a, cd, proj3, dn_norm_w[layer][None])

    out = _merge(xa.reshape(b * LP, D_MODEL), xd.reshape(b * LP, D_MODEL), proj2, hp2,
                 w_branch_attn[layer].astype(BF16), w_branch_delta[layer].astype(BF16),
                 w_out[layer].astype(BF16), final_norm_w[None])
    return out.reshape(b, LP, D_MODEL)[:, FRONT + N_META:]
```

```python
import functools
import math

import jax
import jax.numpy as jnp
from jax import lax
from jax.experimental import pallas as pl
from jax.experimental.pallas import tpu as pltpu

D_MODEL = 1024
SEQ = 8192
N_META = 16
NORM_EPS = 1e-6
HEADS = 8
HEAD_DIM = 128
QK_DIM = 64
CONV_K = 4
CHUNK = 64
LANES = 128
LP = 8448
FRONT = LP - SEQ - N_META
N_MAIN = 10 * D_MODEL
NEG = -1e30

C_AQ, C_AK, C_AV, C_AZ, C_DQ, C_DK, C_DV, C_DZ, C_GA, C_GD = range(10)

F32 = jnp.float32
BF16 = jnp.bfloat16


def _rms(x, w):
    return x * lax.rsqrt(jnp.mean(x * x, axis=-1, keepdims=True) + NORM_EPS) * w


def _dot(a, b):
    return jnp.dot(a, b, preferred_element_type=F32)


def _dot_nt(a, b):
    return lax.dot_general(a, b, (((1,), (1,)), ((), ())), preferred_element_type=F32)


def _dot_tn(a, b):
    return lax.dot_general(a, b, (((0,), (0,)), ((), ())), preferred_element_type=F32)


IN_TM = 768
IN_TN = 1024


def _inproj_kernel(h_ref, nw_ref, w_ref, wg_ref, wgt_ref, p_ref, gcol_ref, grow_ref, hn_sc):
    @pl.when(pl.program_id(1) == 0)
    def _():
        hn = _rms(h_ref[...], nw_ref[...]).astype(BF16)
        hn_sc[...] = hn
        gcol_ref[...] = _dot(hn, wg_ref[...])
        grow_ref[...] = _dot_nt(wgt_ref[...], hn)

    p_ref[...] = _dot(hn_sc[...], w_ref[...]).astype(BF16)


def _inproj(hp2, norm_w, w_main, w_gate, w_gate_t):
    rows = hp2.shape[0]
    grid = (rows // IN_TM, N_MAIN // IN_TN)
    return pl.pallas_call(
        _inproj_kernel,
        grid=grid,
        in_specs=[
            pl.BlockSpec((IN_TM, D_MODEL), lambda i, j: (i, 0)),
            pl.BlockSpec((1, D_MODEL), lambda i, j: (0, 0)),
            pl.BlockSpec((D_MODEL, IN_TN), lambda i, j: (0, j)),
            pl.BlockSpec((D_MODEL, LANES), lambda i, j: (0, 0)),
            pl.BlockSpec((2 * HEADS, D_MODEL), lambda i, j: (0, 0)),
        ],
        out_specs=[
            pl.BlockSpec((IN_TM, IN_TN), lambda i, j: (i, j)),
            pl.BlockSpec((IN_TM, LANES), lambda i, j: (i, 0)),
            pl.BlockSpec((2 * HEADS, IN_TM), lambda i, j: (0, i)),
        ],
        out_shape=[
            jax.ShapeDtypeStruct((rows, N_MAIN), BF16),
            jax.ShapeDtypeStruct((rows, LANES), F32),
            jax.ShapeDtypeStruct((2 * HEADS, rows), F32),
        ],
        scratch_shapes=[pltpu.VMEM((IN_TM, D_MODEL), BF16)],
        compiler_params=pltpu.CompilerParams(
            dimension_semantics=("parallel", "arbitrary"),
            vmem_limit_bytes=40 * 1024 * 1024),
        name="inproj",
    )(hp2, norm_w, w_main, w_gate, w_gate_t)


AT_TQ = 256
AT_BIG = 8
AT_MID = 4
AT_SHIFT_MAX = 30.0


def _attn_kernel(lam_ref, q_ref, k_ref, v_ref, z_ref, nw_ref, o_ref, kx_sc, vx_sc, kmax_sc, m_sc,
                 acc_sc, *, lam_init):
    i = pl.program_id(2)
    tq = AT_TQ

    @pl.when(i == 0)
    def _():
        k = k_ref[...]
        kx_sc[:, 0:HEAD_DIM] = k
        rowi = lax.broadcasted_iota(jnp.int32, (LP, LANES), 0)
        ln = lax.broadcasted_iota(jnp.int32, (LP, LANES), 1)
        ext = jnp.where(ln == 0, 1.0, jnp.where((ln == 1) & (rowi < FRONT), NEG, 0.0))
        kx_sc[:, HEAD_DIM:] = ext.astype(BF16)
        vx_sc[:, 0:HEAD_DIM] = v_ref[...]
        vx_sc[:, HEAD_DIM:] = jnp.ones((LP, HEAD_DIM), BF16)
        kf = k.astype(F32)
        ksq = kf * kf
        first = ln < QK_DIM
        n1 = jnp.max(jnp.sum(jnp.where(first, ksq, 0.0), axis=-1, keepdims=True), axis=0, keepdims=True)
        n2 = jnp.max(jnp.sum(jnp.where(first, 0.0, ksq), axis=-1, keepdims=True), axis=0, keepdims=True)
        kmax_sc[0:1, :] = jnp.broadcast_to(jnp.sqrt(n1), (1, LANES))
        kmax_sc[1:2, :] = jnp.broadcast_to(jnp.sqrt(n2), (1, LANES))

    q = q_ref[...]
    lane = lax.broadcasted_iota(jnp.int32, (tq, LANES), 1)
    first = lane < QK_DIM
    zero = jnp.zeros_like(q)
    qf = q.astype(F32)
    qsq = qf * qf
    shift1 = jnp.sqrt(jnp.sum(jnp.where(first, qsq, 0.0), axis=-1, keepdims=True)) * kmax_sc[0:1, :]
    shift2 = jnp.sqrt(jnp.sum(jnp.where(first, 0.0, qsq), axis=-1, keepdims=True)) * kmax_sc[1:2, :]
    shift = jnp.concatenate([shift1, shift2], axis=0)
    use_shift = jnp.max(shift) <= AT_SHIFT_MAX
    q2 = jnp.concatenate([jnp.where(first, q, zero), jnp.where(first, zero, q)], axis=0)
    lane2 = lax.broadcasted_iota(jnp.int32, (2 * tq, LANES), 1)
    pad_lane = jnp.where(lane2 == 1, 1.0, 0.0)

    acc_sc[...] = jnp.zeros(acc_sc.shape, F32)

    def causal(s):
        r2 = lax.broadcasted_iota(jnp.int32, (2 * tq, tq), 0)
        row = jnp.where(r2 >= tq, r2 - tq, r2)
        col = lax.broadcasted_iota(jnp.int32, (2 * tq, tq), 1)
        return jnp.where(col <= row, s, NEG)

    def scores(qx, start, units):
        return _dot_nt(qx, kx_sc[pl.ds(start, units * tq), :])

    @pl.when(use_shift)
    def _():
        qx = jnp.concatenate([q2, (pad_lane - jnp.where(lane2 == 0, shift, 0.0)).astype(BF16)], axis=1)

        def consume(start, units, s):
            p = jnp.exp2(s).astype(BF16)
            acc_sc[...] += _dot(p, vx_sc[pl.ds(start, units * tq), :])

        n_big = i // AT_BIG

        def big(j, c):
            start = pl.multiple_of(j * (AT_BIG * tq), AT_BIG * tq)
            consume(start, AT_BIG, scores(qx, start, AT_BIG))
            return c

        lax.fori_loop(0, n_big, big, 0)
        rem = i - n_big * AT_BIG
        has_mid = rem >= AT_MID
        mid_start = pl.multiple_of(n_big * (AT_BIG * tq), AT_MID * tq)

        @pl.when(has_mid)
        def _():
            consume(mid_start, AT_MID, scores(qx, mid_start, AT_MID))

        tail_units = rem - jnp.where(has_mid, AT_MID, 0) + 1
        tail_start = pl.multiple_of((i + 1 - tail_units) * tq, tq)
        for units in range(1, AT_MID + 1):
            @pl.when(tail_units == units)
            def _():
                s = scores(qx, tail_start, units)
                s_diag = causal(s[:, (units - 1) * tq:])
                if units > 1:
                    s_diag = jnp.concatenate([s[:, :(units - 1) * tq], s_diag], axis=1)
                consume(tail_start, units, s_diag)

    @pl.when(jnp.logical_not(use_shift))
    def _():
        qx = jnp.concatenate([q2, pad_lane.astype(BF16)], axis=1)
        m_sc[...] = jnp.full(m_sc.shape, NEG, F32)

        def block(j, diag):
            start = pl.multiple_of(j * tq, tq)
            s = scores(qx, start, 1)
            if diag:
                s = causal(s)
            chunks = [s[:, c * LANES:(c + 1) * LANES] for c in range(tq // LANES)]
            mc = chunks[0]
            for ch in chunks[1:]:
                mc = jnp.maximum(mc, ch)
            m_prev = m_sc[...]
            m_new = jnp.maximum(m_prev, jnp.max(mc, axis=-1, keepdims=True))
            alpha = jnp.exp2(m_prev - m_new)
            p = jnp.concatenate([jnp.exp2(ch - m_new) for ch in chunks], axis=1).astype(BF16)
            pv = _dot(p, vx_sc[pl.ds(start, tq), :])
            acc_sc[...] = jnp.concatenate([alpha, alpha], axis=1) * acc_sc[...] + pv
            m_sc[...] = m_new

        def full(j, c):
            block(j, False)
            return c

        lax.fori_loop(0, i, full, 0)
        block(i, True)

    lp = lam_ref[...]
    lam = (jnp.exp(jnp.sum(lp[0:1] * lp[1:2], axis=-1, keepdims=True))
           - jnp.exp(jnp.sum(lp[2:3] * lp[3:4], axis=-1, keepdims=True)) + lam_init)
    acc = acc_sc[...]
    o_all = acc[:, :HEAD_DIM] / jnp.maximum(acc[:, HEAD_DIM:], 1e-37)
    o = o_all[:tq] - lam * o_all[tq:]
    o = _rms(o, nw_ref[...]) * (1.0 - lam_init)
    z = z_ref[...].astype(F32)
    o_ref[...] = (o * (z * jax.nn.sigmoid(z))).astype(BF16)


def _attention(proj3, lam_params, attn_norm_w, lam_init):
    b = proj3.shape[0]
    t = AT_TQ
    hb = D_MODEL // HEAD_DIM
    kern = functools.partial(_attn_kernel, lam_init=lam_init)
    return pl.pallas_call(
        kern,
        grid=(b, HEADS, LP // t),
        in_specs=[
            pl.BlockSpec((4, QK_DIM), lambda b_, h, i: (0, 0)),
            pl.BlockSpec((None, t, HEAD_DIM), lambda b_, h, i: (b_, i, C_AQ * hb + h)),
            pl.BlockSpec((None, LP, HEAD_DIM), lambda b_, h, i: (b_, 0, C_AK * hb + h)),
            pl.BlockSpec((None, LP, HEAD_DIM), lambda b_, h, i: (b_, 0, C_AV * hb + h)),
            pl.BlockSpec((None, t, HEAD_DIM), lambda b_, h, i: (b_, i, C_AZ * hb + h)),
            pl.BlockSpec((1, HEAD_DIM), lambda b_, h, i: (0, 0)),
        ],
        out_specs=pl.BlockSpec((None, t, HEAD_DIM), lambda b_, h, i: (b_, i, h)),
        out_shape=jax.ShapeDtypeStruct((b, LP, D_MODEL), BF16),
        scratch_shapes=[
            pltpu.VMEM((LP, 2 * HEAD_DIM), BF16),
            pltpu.VMEM((LP, 2 * HEAD_DIM), BF16),
            pltpu.VMEM((8, LANES), F32),
            pltpu.VMEM((2 * t, LANES), F32),
            pltpu.VMEM((2 * t, 2 * HEAD_DIM), F32),
        ],
        compiler_params=pltpu.CompilerParams(
            dimension_semantics=("parallel", "parallel", "arbitrary"),
            vmem_limit_bytes=48 * 1024 * 1024),
        name="diff_attn",
    )(lam_params, proj3, proj3, proj3, proj3, attn_norm_w)


DN_T = 2 * CHUNK
HALO = 8


def _split3(x):
    hi = x.astype(BF16)
    r1 = x - hi.astype(F32)
    mid = r1.astype(BF16)
    lo = (r1 - mid.astype(F32)).astype(BF16)
    return hi, mid, lo


def _dn_prep_kernel(dq_ref, dk_ref, dv_ref, hq_ref, hk_ref, hv_ref, cw_ref, gcol_ref, grow_ref,
                    gpr_ref, gpc_ref, u_ref, w_ref, qd_ref, kd_ref, a_ref, cd_ref, xs_sc):
    i = pl.program_id(1)
    t = DN_T
    width = D_MODEL
    for idx, (t_ref, h_ref) in enumerate(((dq_ref, hq_ref), (dk_ref, hk_ref), (dv_ref, hv_ref))):
        halo = h_ref[...].astype(F32)
        xs_sc[0:HALO, idx * width:(idx + 1) * width] = jnp.where(i == 0, jnp.zeros_like(halo), halo)
        xs_sc[HALO:HALO + t, idx * width:(idx + 1) * width] = t_ref[...].astype(F32)
    cw = cw_ref[...]
    y = cw[0:1] * xs_sc[pl.ds(HALO - CONV_K + 1, t), :]
    for j in range(1, CONV_K):
        y = y + cw[j:j + 1] * xs_sc[pl.ds(HALO - CONV_K + 1 + j, t), :]
    y = y * jax.nn.sigmoid(y)

    r = lax.broadcasted_iota(jnp.int32, (t, t), 0)
    c = lax.broadcasted_iota(jnp.int32, (t, t), 1)
    same = (r >= CHUNK) == (c >= CHUNK)
    incl = same & (c <= r)
    strict = same & (c < r)
    tri = jnp.where(incl, 1.0, 0.0).astype(BF16)
    tri_t = jnp.where(same & (r <= c), 1.0, 0.0).astype(BF16)
    blk = jnp.where(same, 1.0, 0.0).astype(BF16)
    eye = jnp.where(r == c, 1.0, 0.0).astype(F32)

    lane = lax.broadcasted_iota(jnp.int32, (t, LANES), 1)
    rowi = lax.broadcasted_iota(jnp.int32, (t, LANES), 0) + i * t
    gcol = gcol_ref[...]
    gpr = gpr_ref[...]
    is_g = (lane >= HEADS) & (lane < 2 * HEADS)
    live_c = rowi >= FRONT
    beta_c = jnp.where(live_c, jax.nn.sigmoid(gcol), 0.0)
    xg = gcol + gpr[1:2]
    sp = jnp.maximum(xg, 0.0) + jnp.log1p(jnp.exp(-jnp.abs(xg)))
    g_c = jnp.where(is_g & live_c, -jnp.exp(gpr[0:1]) * sp, 0.0)
    g3 = _split3(g_c)
    gc_c = _dot(tri, g3[0]) + _dot(tri, g3[1]) + _dot(tri, g3[2])
    gl_c = _dot(blk, g3[0]) + _dot(blk, g3[1]) + _dot(blk, g3[2])
    sub = lax.broadcasted_iota(jnp.int32, (2 * HEADS, t), 0)
    coli = lax.broadcasted_iota(jnp.int32, (2 * HEADS, t), 1) + i * t
    xr = grow_ref[...] + gpc_ref[1]
    spr = jnp.maximum(xr, 0.0) + jnp.log1p(jnp.exp(-jnp.abs(xr)))
    g_r = jnp.where((sub >= HEADS) & (coli >= FRONT), -jnp.exp(gpc_ref[0]) * spr, 0.0)
    gr3 = _split3(g_r)
    gc_r = _dot(gr3[0], tri_t) + _dot(gr3[1], tri_t) + _dot(gr3[2], tri_t)

    cd_ref[...] = jnp.exp(gl_c)

    hs = range(HEADS)
    sls = [slice(h * HEAD_DIM, (h + 1) * HEAD_DIM) for h in hs]
    q = [y[:, h * HEAD_DIM:(h + 1) * HEAD_DIM] for h in hs]
    k = [y[:, width + h * HEAD_DIM:width + (h + 1) * HEAD_DIM] for h in hs]
    v = [y[:, 2 * width + h * HEAD_DIM:2 * width + (h + 1) * HEAD_DIM] for h in hs]
    q = [x * lax.rsqrt(jnp.sum(x * x, axis=-1, keepdims=True) + 1e-6) * (HEAD_DIM ** -0.5) for x in q]
    k = [x * lax.rsqrt(jnp.sum(x * x, axis=-1, keepdims=True) + 1e-6) for x in k]
    beta = [beta_c[:, h:h + 1] for h in hs]
    gcc = [gc_c[:, HEADS + h:HEADS + h + 1] for h in hs]
    glc = [gl_c[:, HEADS + h:HEADS + h + 1] for h in hs]
    gcr = [gc_r[HEADS + h:HEADS + h + 1, :] for h in hs]
    dec = [jnp.where(incl, jnp.exp(jnp.where(incl, gcc[h] - gcr[h], 0.0)), 0.0) for h in hs]
    kb = [k[h] * beta[h] for h in hs]
    kbf = [x.astype(BF16) for x in k]
    m = [jnp.where(strict, _dot_nt(kb[h].astype(BF16), kbf[h]) * dec[h], 0.0) for h in hs]
    tinv = [eye - x for x in m]
    pwb = [x.astype(BF16) for x in m]
    for _ in range(5):
        pw = [_dot(x, x) for x in pwb]
        pwb = [x.astype(BF16) for x in pw]
        tinv = [tinv[h] + _dot(tinv[h].astype(BF16), pwb[h]) for h in hs]
    egc = [jnp.exp(x) for x in gcc]
    rhs = [jnp.concatenate([v[h] * beta[h], kb[h] * egc[h]], axis=1).astype(BF16) for h in hs]
    uw = [_dot(tinv[h].astype(BF16), rhs[h]) for h in hs]
    a_full = [_dot_nt(q[h].astype(BF16), kbf[h]) * dec[h] for h in hs]
    for h in hs:
        u_ref[:, sls[h]] = uw[h][:, :HEAD_DIM]
        w_ref[:, sls[h]] = uw[h][:, HEAD_DIM:].astype(BF16)
        a_cmp = a_full[h] + pltpu.roll(a_full[h], CHUNK, axis=1)
        a_ref[:, h * CHUNK:(h + 1) * CHUNK] = a_cmp[:, :CHUNK].astype(BF16)
        qd_ref[:, sls[h]] = (q[h] * egc[h]).astype(BF16)
        kd_ref[:, sls[h]] = (k[h] * jnp.exp(glc[h] - gcc[h])).astype(BF16)


def _dn_prep(proj3, conv_w2, gcol3, grow, gp_row, gp_col):
    b = proj3.shape[0]
    t = DN_T
    nt = LP // t
    hpb = t // HALO

    def tile(cb):
        return pl.BlockSpec((None, t, D_MODEL), lambda b_, i: (b_, i, cb))

    def halo(cb):
        return pl.BlockSpec((None, HALO, D_MODEL), lambda b_, i: (b_, jnp.maximum(i * hpb - 1, 0), cb))

    full = pl.BlockSpec((None, t, D_MODEL), lambda b_, i: (b_, i, 0))
    return pl.pallas_call(
        _dn_prep_kernel,
        grid=(b, nt),
        in_specs=[
            tile(C_DQ), tile(C_DK), tile(C_DV), halo(C_DQ), halo(C_DK), halo(C_DV),
            pl.BlockSpec((CONV_K, 3 * D_MODEL), lambda b_, i: (0, 0)),
            pl.BlockSpec((None, t, LANES), lambda b_, i: (b_, i, 0)),
            pl.BlockSpec((2 * HEADS, t), lambda b_, i: (0, b_ * nt + i)),
            pl.BlockSpec((2, LANES), lambda b_, i: (0, 0)),
            pl.BlockSpec((2, 2 * HEADS, t), lambda b_, i: (0, 0, 0)),
        ],
        out_specs=[
            full, full, full, full,
            pl.BlockSpec((None, t, HEADS * CHUNK), lambda b_, i: (b_, i, 0)),
            pl.BlockSpec((None, t, LANES), lambda b_, i: (b_, i, 0)),
        ],
        out_shape=[
            jax.ShapeDtypeStruct((b, LP, D_MODEL), F32),
            jax.ShapeDtypeStruct((b, LP, D_MODEL), BF16),
            jax.ShapeDtypeStruct((b, LP, D_MODEL), BF16),
            jax.ShapeDtypeStruct((b, LP, D_MODEL), BF16),
            jax.ShapeDtypeStruct((b, LP, HEADS * CHUNK), BF16),
            jax.ShapeDtypeStruct((b, LP, LANES), F32),
        ],
        scratch_shapes=[pltpu.VMEM((HALO + t, 3 * D_MODEL), F32)],
        compiler_params=pltpu.CompilerParams(
            dimension_semantics=("parallel", "parallel"),
            vmem_limit_bytes=40 * 1024 * 1024),
        name="dn_prep",
    )(proj3, proj3, proj3, proj3, proj3, proj3, conv_w2, gcol3, grow, gp_row, gp_col)


SC_T = 4 * CHUNK


def _dn_scan_kernel(u_ref, w_ref, qd_ref, kd_ref, a_ref, cd_ref, z_ref, nw_ref, o_ref, s_sc):
    @pl.when(pl.program_id(1) == 0)
    def _():
        s_sc[...] = jnp.zeros(s_sc.shape, F32)

    nw = nw_ref[...]
    hs = range(HEADS)
    sls = [slice(h * HEAD_DIM, (h + 1) * HEAD_DIM) for h in hs]
    s = [s_sc[h] for h in hs]
    for cix in range(SC_T // CHUNK):
        rs = slice(cix * CHUNK, (cix + 1) * CHUNK)
        sb = [x.astype(BF16) for x in s]
        ws_qs = [_dot(jnp.concatenate([w_ref[rs, sls[h]], qd_ref[rs, sls[h]]], axis=0), sb[h]) for h in hs]
        vb = [(u_ref[rs, sls[h]] - ws_qs[h][:CHUNK]).astype(BF16) for h in hs]
        o = [ws_qs[h][CHUNK:] + _dot(a_ref[rs, h * CHUNK:(h + 1) * CHUNK], vb[h]) for h in hs]
        cd = [cd_ref[cix * CHUNK:cix * CHUNK + 1, HEADS + h:HEADS + h + 1] for h in hs]
        s = [s[h] * cd[h] + _dot_tn(kd_ref[rs, sls[h]], vb[h]) for h in hs]
        for h in hs:
            z = z_ref[rs, sls[h]].astype(F32)
            o_ref[rs, sls[h]] = (_rms(o[h], nw) * (z * jax.nn.sigmoid(z))).astype(BF16)
    for h in hs:
        s_sc[h] = s[h]


def _dn_scan(u, w, qd, kd, a, cd, proj3, dn_norm_w):
    b = u.shape[0]
    t = SC_T
    full = pl.BlockSpec((None, t, D_MODEL), lambda b_, i: (b_, i, 0))
    return pl.pallas_call(
        _dn_scan_kernel,
        grid=(b, LP // t),
        in_specs=[
            full, full, full, full,
            pl.BlockSpec((None, t, HEADS * CHUNK), lambda b_, i: (b_, i, 0)),
            pl.BlockSpec((None, t, LANES), lambda b_, i: (b_, i, 0)),
            pl.BlockSpec((None, t, D_MODEL), lambda b_, i: (b_, i, C_DZ)),
            pl.BlockSpec((1, HEAD_DIM), lambda b_, i: (0, 0)),
        ],
        out_specs=full,
        out_shape=jax.ShapeDtypeStruct((b, LP, D_MODEL), BF16),
        scratch_shapes=[pltpu.VMEM((HEADS, HEAD_DIM, HEAD_DIM), F32)],
        compiler_params=pltpu.CompilerParams(
            dimension_semantics=("parallel", "arbitrary"),
            vmem_limit_bytes=40 * 1024 * 1024),
        name="dn_scan",
    )(u, w, qd, kd, a, cd, proj3, dn_norm_w)


OUT_TM = 768


def _merge_kernel(xa_ref, xd_ref, ga_ref, gd_ref, h_ref, wa_ref, wd_ref, wo_ref, nw_ref, o_ref):
    ya = _dot(xa_ref[...], wa_ref[...])
    yd = _dot(xd_ref[...], wd_ref[...])
    merged = (jax.nn.sigmoid(ga_ref[...].astype(F32)) * ya
              + jax.nn.sigmoid(gd_ref[...].astype(F32)) * yd)
    out = h_ref[...] + _dot(merged.astype(BF16), wo_ref[...])
    o_ref[...] = _rms(out, nw_ref[...])


def _merge(xa, xd, proj2, hp2, wa, wd, wo, final_norm_w):
    rows = hp2.shape[0]
    tm = OUT_TM
    row = pl.BlockSpec((tm, D_MODEL), lambda i: (i, 0))
    wspec = pl.BlockSpec((D_MODEL, D_MODEL), lambda i: (0, 0))
    return pl.pallas_call(
        _merge_kernel,
        grid=(rows // tm,),
        in_specs=[
            row, row,
            pl.BlockSpec((tm, D_MODEL), lambda i: (i, C_GA)),
            pl.BlockSpec((tm, D_MODEL), lambda i: (i, C_GD)),
            row, wspec, wspec, wspec,
            pl.BlockSpec((1, D_MODEL), lambda i: (0, 0)),
        ],
        out_specs=row,
        out_shape=jax.ShapeDtypeStruct((rows, D_MODEL), F32),
        compiler_params=pltpu.CompilerParams(
            dimension_semantics=("parallel",),
            vmem_limit_bytes=48 * 1024 * 1024),
        name="merge_out",
    )(xa, xd, proj2, proj2, hp2, wa, wd, wo, final_norm_w)


def kernel(x, meta_tokens, norm_w, w_in, lambda_q1, lambda_k1, lambda_q2, lambda_k2, attn_norm_w,
           conv_w, a_log, dt_bias, dn_norm_w, w_branch_attn, w_branch_delta, w_out, final_norm_w):
    b = x.shape[0]
    assert x.shape == (b, SEQ, D_MODEL) and norm_w.shape[0] == 1
    layer = 0
    lam_init = 0.8 - 0.6 * math.exp(-0.3 * layer)

    head = jnp.concatenate([jnp.zeros((FRONT, D_MODEL), x.dtype), meta_tokens.astype(x.dtype)], axis=0)
    hp = jnp.concatenate([jnp.broadcast_to(head[None], (b, FRONT + N_META, D_MODEL)), x], axis=1)
    hp2 = hp.reshape(b * LP, D_MODEL)

    wi = w_in[layer]
    gate0 = 8 * D_MODEL
    q_scale = QK_DIM ** -0.5 * math.log2(math.e)
    w_main = jnp.concatenate([wi[:, :D_MODEL] * q_scale, wi[:, D_MODEL:gate0], wi[:, gate0 + 2 * HEADS:]],
                             axis=1).astype(BF16)
    w_gate = wi[:, gate0:gate0 + 2 * HEADS]
    w_gate_p = jnp.pad(w_gate, ((0, 0), (0, LANES - 2 * HEADS))).astype(BF16)
    w_gate_t = w_gate.T.astype(BF16)
    zeros8 = jnp.zeros((HEADS,), F32)
    gp_row = jnp.stack([jnp.pad(jnp.concatenate([zeros8, a_log[layer]]), (0, LANES - 2 * HEADS)),
                        jnp.pad(jnp.concatenate([zeros8, dt_bias[layer]]), (0, LANES - 2 * HEADS))])
    gp_col = jnp.stack([jnp.concatenate([zeros8, a_log[layer]]),
                        jnp.concatenate([zeros8, dt_bias[layer]])])
    gp_col = jnp.broadcast_to(gp_col[:, :, None], (2, 2 * HEADS, DN_T)).astype(F32)
    lam_params = jnp.stack([lambda_q1[layer], lambda_k1[layer], lambda_q2[layer], lambda_k2[layer]])

    proj2, gcol, grow = _inproj(hp2, norm_w[layer][None], w_main, w_gate_p, w_gate_t)
    proj3 = proj2.reshape(b, LP, N_MAIN)

    xa = _attention(proj3, lam_params, attn_norm_w[layer][None], lam_init)

    u, w, qd, kd, a, cd = _dn_prep(proj3, conv_w[layer], gcol.reshape(b, LP, LANES), grow, gp_row, gp_col)
    xd = _dn_scan(u, w, qd, kd, a, cd, proj3, dn_norm_w[layer][None])

    out = _merge(xa.reshape(b * LP, D_MODEL), xd.reshape(b * LP, D_MODEL), proj2, hp2,
                 w_branch_attn[layer].astype(BF16), w_branch_delta[layer].astype(BF16),
                 w_out[layer].astype(BF16), final_norm_w[None])
    return out.reshape(b, LP, D_MODEL)[:, FRONT + N_META:]
```

```python
import functools
import math

import jax
import jax.numpy as jnp
from jax import lax
from jax.experimental import pallas as pl
from jax.experimental.pallas import tpu as pltpu

D_MODEL = 1024
SEQ = 8192
N_META = 16
NORM_EPS = 1e-6
HEADS = 8
HEAD_DIM = 128
QK_DIM = 64
CONV_K = 4
CHUNK = 64
LANES = 128
LP = 8448
FRONT = LP - SEQ - N_META
X0 = FRONT + N_META
N_MAIN = 10 * D_MODEL
NEG = -1e30

C_AQ, C_AK, C_AV, C_AZ, C_DQ, C_DK, C_DV, C_DZ, C_GA, C_GD = range(10)

F32 = jnp.float32
BF16 = jnp.bfloat16


def _rms(x, w):
    return x * lax.rsqrt(jnp.mean(x * x, axis=-1, keepdims=True) + NORM_EPS) * w


def _dot(a, b):
    return jnp.dot(a, b, preferred_element_type=F32)


def _dot_nt(a, b):
    return lax.dot_general(a, b, (((1,), (1,)), ((), ())), preferred_element_type=F32)


def _dot_tn(a, b):
    return lax.dot_general(a, b, (((0,), (0,)), ((), ())), preferred_element_type=F32)


IN_SUB = 3
IN_TM = IN_SUB * X0
IN_TN = 2048
IN_TILES = LP // IN_TM


def _inproj_kernel(*refs):
    head_ref = refs[0]
    x_refs = refs[1:1 + IN_SUB]
    nw_ref, w_ref, wg_ref, wgt_ref, p_ref, gcol_ref, grow_ref, hn_sc = refs[1 + IN_SUB:]

    @pl.when(pl.program_id(1) == 0)
    def _():
        first_tile = pl.program_id(0) % IN_TILES == 0
        nw = nw_ref[...]
        for s, x_ref in enumerate(x_refs):
            h = x_ref[...]
            if s == 0:
                h = jnp.where(first_tile, head_ref[...], h)
            hn_sc[s * X0:(s + 1) * X0, :] = _rms(h, nw).astype(BF16)
        hn = hn_sc[...]
        gcol_ref[...] = _dot(hn, wg_ref[...])
        grow_ref[...] = _dot_nt(wgt_ref[...], hn)

    p_ref[...] = _dot(hn_sc[...], w_ref[...]).astype(BF16)


def _inproj(head, x, norm_w, w_main, w_gate, w_gate_t):
    b = x.shape[0]
    rows = b * LP
    grid = (rows // IN_TM, N_MAIN // IN_TN)

    def xblock(s):
        return pl.BlockSpec((None, X0, D_MODEL),
                            lambda i, j: (i // IN_TILES, jnp.maximum((i % IN_TILES) * IN_SUB + s - 1, 0), 0))

    return pl.pallas_call(
        _inproj_kernel,
        grid=grid,
        in_specs=[pl.BlockSpec((X0, D_MODEL), lambda i, j: (0, 0))] + [xblock(s) for s in range(IN_SUB)] + [
            pl.BlockSpec((1, D_MODEL), lambda i, j: (0, 0)),
            pl.BlockSpec((D_MODEL, IN_TN), lambda i, j: (0, j)),
            pl.BlockSpec((D_MODEL, LANES), lambda i, j: (0, 0)),
            pl.BlockSpec((2 * HEADS, D_MODEL), lambda i, j: (0, 0)),
        ],
        out_specs=[
            pl.BlockSpec((IN_TM, IN_TN), lambda i, j: (i, j)),
            pl.BlockSpec((IN_TM, LANES), lambda i, j: (i, 0)),
            pl.BlockSpec((2 * HEADS, IN_TM), lambda i, j: (0, i)),
        ],
        out_shape=[
            jax.ShapeDtypeStruct((rows, N_MAIN), BF16),
            jax.ShapeDtypeStruct((rows, LANES), F32),
            jax.ShapeDtypeStruct((2 * HEADS, rows), F32),
        ],
        scratch_shapes=[pltpu.VMEM((IN_TM, D_MODEL), BF16)],
        compiler_params=pltpu.CompilerParams(
            dimension_semantics=("parallel", "arbitrary"),
            vmem_limit_bytes=40 * 1024 * 1024),
        name="inproj",
    )(head, *([x] * IN_SUB), norm_w, w_main, w_gate, w_gate_t)


AT_TQ = 256
AT_BIG = 8
AT_MID = 4
AT_SHIFT_MAX = 30.0


def _attn_kernel(lam_ref, q_ref, k_ref, v_ref, z_ref, nw_ref, o_ref, kx_sc, vx_sc, e1_sc, e2_sc, flag_sc,
                 m_sc, acc_sc, *, lam_init):
    i = pl.program_id(2)
    tq = AT_TQ

    @pl.when(i == 0)
    def _():
        k = k_ref[...]
        kx_sc[:, 0:HEAD_DIM] = k
        rowi = lax.broadcasted_iota(jnp.int32, (LP, LANES), 0)
        ln = lax.broadcasted_iota(jnp.int32, (LP, LANES), 1)
        ext = jnp.where(ln == 0, 1.0, jnp.where((ln == 1) & (rowi < FRONT), NEG, 0.0))
        kx_sc[:, HEAD_DIM:] = ext.astype(BF16)
        vx_sc[:, 0:HEAD_DIM] = v_ref[...]
        vx_sc[:, HEAD_DIM:] = jnp.ones((LP, HEAD_DIM), BF16)
        first = ln < QK_DIM
        pad_lane = jnp.where(ln == 1, 1.0, 0.0)

        def half_norms(x):
            sq = x.astype(F32)
            sq = sq * sq
            return (jnp.sqrt(jnp.sum(jnp.where(first, sq, 0.0), axis=-1, keepdims=True)),
                    jnp.sqrt(jnp.sum(jnp.where(first, 0.0, sq), axis=-1, keepdims=True)))

        kn1, kn2 = half_norms(k)
        qn1, qn2 = half_norms(q_ref[...])
        shift1 = qn1 * jnp.max(kn1, axis=0, keepdims=True)
        shift2 = qn2 * jnp.max(kn2, axis=0, keepdims=True)
        e1_sc[...] = (pad_lane - jnp.where(ln == 0, shift1, 0.0)).astype(BF16)
        e2_sc[...] = (pad_lane - jnp.where(ln == 0, shift2, 0.0)).astype(BF16)
        ok = jnp.maximum(jnp.max(shift1), jnp.max(shift2)) <= AT_SHIFT_MAX
        flag_sc[0] = ok.astype(jnp.int32)

    row0 = pl.multiple_of(i * tq, tq)
    q = q_ref[pl.ds(row0, tq), :]
    first = lax.broadcasted_iota(jnp.int32, (tq, LANES), 1) < QK_DIM
    zero = jnp.zeros_like(q)
    use_shift = flag_sc[0] == 1
    q2 = jnp.concatenate([jnp.where(first, q, zero), jnp.where(first, zero, q)], axis=0)

    acc_sc[...] = jnp.zeros(acc_sc.shape, F32)

    def causal(s):
        r2 = lax.broadcasted_iota(jnp.int32, (2 * tq, tq), 0)
        row = jnp.where(r2 >= tq, r2 - tq, r2)
        col = lax.broadcasted_iota(jnp.int32, (2 * tq, tq), 1)
        return jnp.where(col <= row, s, NEG)

    def scores(qx, start, units):
        return _dot_nt(qx, kx_sc[pl.ds(start, units * tq), :])

    @pl.when(use_shift)
    def _():
        bias = jnp.concatenate([e1_sc[pl.ds(row0, tq), :], e2_sc[pl.ds(row0, tq), :]], axis=0)
        qx = jnp.concatenate([q2, bias], axis=1)

        def consume(start, units, s):
            p = jnp.exp2(s).astype(BF16)
            acc_sc[...] += _dot(p, vx_sc[pl.ds(start, units * tq), :])

        n_big = i // AT_BIG

        def big(j, c):
            start = pl.multiple_of(j * (AT_BIG * tq), AT_BIG * tq)
            consume(start, AT_BIG, scores(qx, start, AT_BIG))
            return c

        lax.fori_loop(0, n_big, big, 0)
        rem = i - n_big * AT_BIG
        has_mid = rem >= AT_MID
        mid_start = pl.multiple_of(n_big * (AT_BIG * tq), AT_MID * tq)

        @pl.when(has_mid)
        def _():
            consume(mid_start, AT_MID, scores(qx, mid_start, AT_MID))

        tail_units = rem - jnp.where(has_mid, AT_MID, 0) + 1
        tail_start = pl.multiple_of((i + 1 - tail_units) * tq, tq)
        for units in range(1, AT_MID + 1):
            @pl.when(tail_units == units)
            def _():
                s = scores(qx, tail_start, units)
                s_diag = causal(s[:, (units - 1) * tq:])
                if units > 1:
                    s_diag = jnp.concatenate([s[:, :(units - 1) * tq], s_diag], axis=1)
                consume(tail_start, units, s_diag)

    @pl.when(jnp.logical_not(use_shift))
    def _():
        lane2 = lax.broadcasted_iota(jnp.int32, (2 * tq, LANES), 1)
        qx = jnp.concatenate([q2, jnp.where(lane2 == 1, 1.0, 0.0).astype(BF16)], axis=1)
        m_sc[...] = jnp.full(m_sc.shape, NEG, F32)

        def block(j, diag):
            start = pl.multiple_of(j * tq, tq)
            s = scores(qx, start, 1)
            if diag:
                s = causal(s)
            chunks = [s[:, c * LANES:(c + 1) * LANES] for c in range(tq // LANES)]
            mc = chunks[0]
            for ch in chunks[1:]:
                mc = jnp.maximum(mc, ch)
            m_prev = m_sc[...]
            m_new = jnp.maximum(m_prev, jnp.max(mc, axis=-1, keepdims=True))
            alpha = jnp.exp2(m_prev - m_new)
            p = jnp.concatenate([jnp.exp2(ch - m_new) for ch in chunks], axis=1).astype(BF16)
            pv = _dot(p, vx_sc[pl.ds(start, tq), :])
            acc_sc[...] = jnp.concatenate([alpha, alpha], axis=1) * acc_sc[...] + pv
            m_sc[...] = m_new

        def full(j, c):
            block(j, False)
            return c

        lax.fori_loop(0, i, full, 0)
        block(i, True)

    lp = lam_ref[...]
    lam = (jnp.exp(jnp.sum(lp[0:1] * lp[1:2], axis=-1, keepdims=True))
           - jnp.exp(jnp.sum(lp[2:3] * lp[3:4], axis=-1, keepdims=True)) + lam_init)
    acc = acc_sc[...]
    o_all = acc[:, :HEAD_DIM] / jnp.maximum(acc[:, HEAD_DIM:], 1e-37)
    o = o_all[:tq] - lam * o_all[tq:]
    o = _rms(o, nw_ref[...]) * (1.0 - lam_init)
    z = z_ref[...].astype(F32)
    o_ref[...] = (o * (z * jax.nn.sigmoid(z))).astype(BF16)


def _attention(proj3, lam_params, attn_norm_w, lam_init):
    b = proj3.shape[0]
    t = AT_TQ
    hb = D_MODEL // HEAD_DIM
    kern = functools.partial(_attn_kernel, lam_init=lam_init)
    return pl.pallas_call(
        kern,
        grid=(b, HEADS, LP // t),
        in_specs=[
            pl.BlockSpec((4, QK_DIM), lambda b_, h, i: (0, 0)),
            pl.BlockSpec((None, LP, HEAD_DIM), lambda b_, h, i: (b_, 0, C_AQ * hb + h)),
            pl.BlockSpec((None, LP, HEAD_DIM), lambda b_, h, i: (b_, 0, C_AK * hb + h)),
            pl.BlockSpec((None, LP, HEAD_DIM), lambda b_, h, i: (b_, 0, C_AV * hb + h)),
            pl.BlockSpec((None, t, HEAD_DIM), lambda b_, h, i: (b_, i, C_AZ * hb + h)),
            pl.BlockSpec((1, HEAD_DIM), lambda b_, h, i: (0, 0)),
        ],
        out_specs=pl.BlockSpec((None, t, HEAD_DIM), lambda b_, h, i: (b_, i, h)),
        out_shape=jax.ShapeDtypeStruct((b, LP, D_MODEL), BF16),
        scratch_shapes=[
            pltpu.VMEM((LP, 2 * HEAD_DIM), BF16),
            pltpu.VMEM((LP, 2 * HEAD_DIM), BF16),
            pltpu.VMEM((LP, LANES), BF16),
            pltpu.VMEM((LP, LANES), BF16),
            pltpu.SMEM((1,), jnp.int32),
            pltpu.VMEM((2 * t, LANES), F32),
            pltpu.VMEM((2 * t, 2 * HEAD_DIM), F32),
        ],
        compiler_params=pltpu.CompilerParams(
            dimension_semantics=("parallel", "parallel", "arbitrary"),
            vmem_limit_bytes=48 * 1024 * 1024),
        name="diff_attn",
    )(lam_params, proj3, proj3, proj3, proj3, attn_norm_w)


DN_T = 2 * CHUNK
HALO = 16


def _shift_matrix():
    r = jnp.arange((CONV_K - 1) * DN_T)
    src = HALO + r % DN_T - (r // DN_T + 1)
    return (src[:, None] == jnp.arange(HALO + DN_T)[None, :]).astype(BF16)


def _split3(x):
    hi = x.astype(BF16)
    r1 = x - hi.astype(F32)
    mid = r1.astype(BF16)
    lo = (r1 - mid.astype(F32)).astype(BF16)
    return hi, mid, lo


def _dn_prep_kernel(dq_ref, dk_ref, dv_ref, hq_ref, hk_ref, hv_ref, cw_ref, sh_ref, gcol_ref, grow_ref,
                    gpr_ref, gpc_ref, u_ref, w_ref, qd_ref, kd_ref, a_ref, cd_ref):
    i = pl.program_id(1)
    t = DN_T
    width = D_MODEL
    cw = cw_ref[...]
    shifter = sh_ref[...]
    ys = []
    for idx, (t_ref, h_ref) in enumerate(((dq_ref, hq_ref), (dk_ref, hk_ref), (dv_ref, hv_ref))):
        halo = h_ref[...]
        halo = jnp.where(i == 0, jnp.zeros_like(halo), halo)
        x = t_ref[...]
        shifted = _dot(shifter, jnp.concatenate([halo, x], axis=0))
        cwi = cw[:, idx * width:(idx + 1) * width]
        yi = cwi[CONV_K - 1:CONV_K] * x.astype(F32)
        for s in range(1, CONV_K):
            yi = yi + cwi[CONV_K - 1 - s:CONV_K - s] * shifted[(s - 1) * t:s * t]
        ys.append(yi * jax.nn.sigmoid(yi))
    y = jnp.concatenate(ys, axis=1)

    r = lax.broadcasted_iota(jnp.int32, (t, t), 0)
    c = lax.broadcasted_iota(jnp.int32, (t, t), 1)
    same = (r >= CHUNK) == (c >= CHUNK)
    incl = same & (c <= r)
    strict = same & (c < r)
    tri = jnp.where(incl, 1.0, 0.0).astype(BF16)
    tri_t = jnp.where(same & (r <= c), 1.0, 0.0).astype(BF16)
    blk = jnp.where(same, 1.0, 0.0).astype(BF16)
    eye = jnp.where(r == c, 1.0, 0.0).astype(F32)

    lane = lax.broadcasted_iota(jnp.int32, (t, LANES), 1)
    rowi = lax.broadcasted_iota(jnp.int32, (t, LANES), 0) + i * t
    gcol = gcol_ref[...]
    gpr = gpr_ref[...]
    is_g = (lane >= HEADS) & (lane < 2 * HEADS)
    live_c = rowi >= FRONT
    beta_c = jnp.where(live_c, jax.nn.sigmoid(gcol), 0.0)
    xg = gcol + gpr[1:2]
    sp = jnp.maximum(xg, 0.0) + jnp.log1p(jnp.exp(-jnp.abs(xg)))
    g_c = jnp.where(is_g & live_c, -jnp.exp(gpr[0:1]) * sp, 0.0)
    g3 = _split3(g_c)
    gc_c = _dot(tri, g3[0]) + _dot(tri, g3[1]) + _dot(tri, g3[2])
    gl_c = _dot(blk, g3[0]) + _dot(blk, g3[1]) + _dot(blk, g3[2])
    sub = lax.broadcasted_iota(jnp.int32, (2 * HEADS, t), 0)
    coli = lax.broadcasted_iota(jnp.int32, (2 * HEADS, t), 1) + i * t
    xr = grow_ref[...] + gpc_ref[1]
    spr = jnp.maximum(xr, 0.0) + jnp.log1p(jnp.exp(-jnp.abs(xr)))
    g_r = jnp.where((sub >= HEADS) & (coli >= FRONT), -jnp.exp(gpc_ref[0]) * spr, 0.0)
    gr3 = _split3(g_r)
    gc_r = _dot(gr3[0], tri_t) + _dot(gr3[1], tri_t) + _dot(gr3[2], tri_t)

    cd_ref[...] = jnp.exp(gl_c)

    hs = range(HEADS)
    sls = [slice(h * HEAD_DIM, (h + 1) * HEAD_DIM) for h in hs]
    q = [y[:, h * HEAD_DIM:(h + 1) * HEAD_DIM] for h in hs]
    k = [y[:, width + h * HEAD_DIM:width + (h + 1) * HEAD_DIM] for h in hs]
    v = [y[:, 2 * width + h * HEAD_DIM:2 * width + (h + 1) * HEAD_DIM] for h in hs]
    q = [x * lax.rsqrt(jnp.sum(x * x, axis=-1, keepdims=True) + 1e-6) * (HEAD_DIM ** -0.5) for x in q]
    k = [x * lax.rsqrt(jnp.sum(x * x, axis=-1, keepdims=True) + 1e-6) for x in k]
    beta = [beta_c[:, h:h + 1] for h in hs]
    gcc = [gc_c[:, HEADS + h:HEADS + h + 1] for h in hs]
    glc = [gl_c[:, HEADS + h:HEADS + h + 1] for h in hs]
    gcr = [gc_r[HEADS + h:HEADS + h + 1, :] for h in hs]
    dec = [jnp.where(incl, jnp.exp(jnp.where(incl, gcc[h] - gcr[h], 0.0)), 0.0) for h in hs]
    kb = [k[h] * beta[h] for h in hs]
    kt = [x.T.astype(BF16) for x in k]
    m = [jnp.where(strict, _dot(kb[h].astype(BF16), kt[h]) * dec[h], 0.0) for h in hs]
    tinv = [eye - x for x in m]
    pwb = [x.astype(BF16) for x in m]
    for _ in range(5):
        pw = [_dot(x, x) for x in pwb]
        pwb = [x.astype(BF16) for x in pw]
        tinv = [tinv[h] + _dot(tinv[h].astype(BF16), pwb[h]) for h in hs]
    egc = [jnp.exp(x) for x in gcc]
    rhs = [jnp.concatenate([v[h] * beta[h], kb[h] * egc[h]], axis=1).astype(BF16) for h in hs]
    uw = [_dot(tinv[h].astype(BF16), rhs[h]) for h in hs]
    a_full = [_dot(q[h].astype(BF16), kt[h]) * dec[h] for h in hs]
    for h in hs:
        u_ref[:, sls[h]] = uw[h][:, :HEAD_DIM]
        w_ref[:, sls[h]] = uw[h][:, HEAD_DIM:].astype(BF16)
        a_cmp = a_full[h] + pltpu.roll(a_full[h], CHUNK, axis=1)
        a_ref[:, h * CHUNK:(h + 1) * CHUNK] = a_cmp[:, :CHUNK].astype(BF16)
        qd_ref[:, sls[h]] = (q[h] * egc[h]).astype(BF16)
        kd_ref[:, sls[h]] = (k[h] * jnp.exp(glc[h] - gcc[h])).astype(BF16)


def _dn_prep(proj3, conv_w2, gcol3, grow, gp_row, gp_col):
    b = proj3.shape[0]
    t = DN_T
    nt = LP // t
    hpb = t // HALO

    def tile(cb):
        return pl.BlockSpec((None, t, D_MODEL), lambda b_, i: (b_, i, cb))

    def halo(cb):
        return pl.BlockSpec((None, HALO, D_MODEL), lambda b_, i: (b_, jnp.maximum(i * hpb - 1, 0), cb))

    full = pl.BlockSpec((None, t, D_MODEL), lambda b_, i: (b_, i, 0))
    return pl.pallas_call(
        _dn_prep_kernel,
        grid=(b, nt),
        in_specs=[
            tile(C_DQ), tile(C_DK), tile(C_DV), halo(C_DQ), halo(C_DK), halo(C_DV),
            pl.BlockSpec((CONV_K, 3 * D_MODEL), lambda b_, i: (0, 0)),
            pl.BlockSpec(((CONV_K - 1) * t, HALO + t), lambda b_, i: (0, 0)),
            pl.BlockSpec((None, t, LANES), lambda b_, i: (b_, i, 0)),
            pl.BlockSpec((2 * HEADS, t), lambda b_, i: (0, b_ * nt + i)),
            pl.BlockSpec((2, LANES), lambda b_, i: (0, 0)),
            pl.BlockSpec((2, 2 * HEADS, t), lambda b_, i: (0, 0, 0)),
        ],
        out_specs=[
            full, full, full, full,
            pl.BlockSpec((None, t, HEADS * CHUNK), lambda b_, i: (b_, i, 0)),
            pl.BlockSpec((None, t, LANES), lambda b_, i: (b_, i, 0)),
        ],
        out_shape=[
            jax.ShapeDtypeStruct((b, LP, D_MODEL), F32),
            jax.ShapeDtypeStruct((b, LP, D_MODEL), BF16),
            jax.ShapeDtypeStruct((b, LP, D_MODEL), BF16),
            jax.ShapeDtypeStruct((b, LP, D_MODEL), BF16),
            jax.ShapeDtypeStruct((b, LP, HEADS * CHUNK), BF16),
            jax.ShapeDtypeStruct((b, LP, LANES), F32),
        ],
        compiler_params=pltpu.CompilerParams(
            dimension_semantics=("parallel", "parallel"),
            vmem_limit_bytes=40 * 1024 * 1024),
        name="dn_prep",
    )(proj3, proj3, proj3, proj3, proj3, proj3, conv_w2, _shift_matrix(), gcol3, grow, gp_row, gp_col)


SC_T = 4 * CHUNK


def _dn_scan_kernel(u_ref, w_ref, qd_ref, kd_ref, a_ref, cd_ref, z_ref, nw_ref, o_ref, s_sc):
    @pl.when(pl.program_id(1) == 0)
    def _():
        s_sc[...] = jnp.zeros(s_sc.shape, F32)

    nw = nw_ref[...]
    hs = range(HEADS)
    sls = [slice(h * HEAD_DIM, (h + 1) * HEAD_DIM) for h in hs]
    s = [s_sc[h] for h in hs]
    for cix in range(SC_T // CHUNK):
        rs = slice(cix * CHUNK, (cix + 1) * CHUNK)
        sb = [x.astype(BF16) for x in s]
        ws_qs = [_dot(jnp.concatenate([w_ref[rs, sls[h]], qd_ref[rs, sls[h]]], axis=0), sb[h]) for h in hs]
        vb = [(u_ref[rs, sls[h]] - ws_qs[h][:CHUNK]).astype(BF16) for h in hs]
        o = [ws_qs[h][CHUNK:] + _dot(a_ref[rs, h * CHUNK:(h + 1) * CHUNK], vb[h]) for h in hs]
        cd = [cd_ref[cix * CHUNK:cix * CHUNK + 1, HEADS + h:HEADS + h + 1] for h in hs]
        s = [s[h] * cd[h] + _dot_tn(kd_ref[rs, sls[h]], vb[h]) for h in hs]
        for h in hs:
            z = z_ref[rs, sls[h]].astype(F32)
            o_ref[rs, sls[h]] = (_rms(o[h], nw) * (z * jax.nn.sigmoid(z))).astype(BF16)
    for h in hs:
        s_sc[h] = s[h]


def _dn_scan(u, w, qd, kd, a, cd, proj3, dn_norm_w):
    b = u.shape[0]
    t = SC_T
    full = pl.BlockSpec((None, t, D_MODEL), lambda b_, i: (b_, i, 0))
    return pl.pallas_call(
        _dn_scan_kernel,
        grid=(b, LP // t),
        in_specs=[
            full, full, full, full,
            pl.BlockSpec((None, t, HEADS * CHUNK), lambda b_, i: (b_, i, 0)),
            pl.BlockSpec((None, t, LANES), lambda b_, i: (b_, i, 0)),
            pl.BlockSpec((None, t, D_MODEL), lambda b_, i: (b_, i, C_DZ)),
            pl.BlockSpec((1, HEAD_DIM), lambda b_, i: (0, 0)),
        ],
        out_specs=full,
        out_shape=jax.ShapeDtypeStruct((b, LP, D_MODEL), BF16),
        scratch_shapes=[pltpu.VMEM((HEADS, HEAD_DIM, HEAD_DIM), F32)],
        compiler_params=pltpu.CompilerParams(
            dimension_semantics=("parallel", "arbitrary"),
            vmem_limit_bytes=40 * 1024 * 1024),
        name="dn_scan",
    )(u, w, qd, kd, a, cd, proj3, dn_norm_w)


OUT_SUB = 2


def _merge_kernel(*refs):
    n = OUT_SUB
    xa_refs, xd_refs, ga_refs, gd_refs = refs[0:n], refs[n:2 * n], refs[2 * n:3 * n], refs[3 * n:4 * n]
    x_ref, wa_ref, wd_ref, wo_ref, nw_ref, o_ref = refs[4 * n:]

    def rows(rs):
        return jnp.concatenate([r[...] for r in rs], axis=0)

    ya = _dot(rows(xa_refs), wa_ref[...])
    yd = _dot(rows(xd_refs), wd_ref[...])
    merged = (jax.nn.sigmoid(rows(ga_refs).astype(F32)) * ya
              + jax.nn.sigmoid(rows(gd_refs).astype(F32)) * yd)
    out = x_ref[...] + _dot(merged.astype(BF16), wo_ref[...])
    o_ref[...] = _rms(out, nw_ref[...])


def _merge(xa, xd, proj3, x, wa, wd, wo, final_norm_w):
    b = x.shape[0]
    tm = OUT_SUB * X0

    def padded(cb):
        return [pl.BlockSpec((None, X0, D_MODEL), lambda b_, i, s=s: (b_, OUT_SUB * i + 1 + s, cb))
                for s in range(OUT_SUB)]

    tok = pl.BlockSpec((None, tm, D_MODEL), lambda b_, i: (b_, i, 0))
    wspec = pl.BlockSpec((D_MODEL, D_MODEL), lambda b_, i: (0, 0))
    return pl.pallas_call(
        _merge_kernel,
        grid=(b, SEQ // tm),
        in_specs=padded(0) + padded(0) + padded(C_GA) + padded(C_GD) + [
            tok, wspec, wspec, wspec,
            pl.BlockSpec((1, D_MODEL), lambda b_, i: (0, 0)),
        ],
        out_specs=tok,
        out_shape=jax.ShapeDtypeStruct((b, SEQ, D_MODEL), F32),
        compiler_params=pltpu.CompilerParams(
            dimension_semantics=("parallel", "parallel"),
            vmem_limit_bytes=48 * 1024 * 1024),
        name="merge_out",
    )(*([xa] * OUT_SUB + [xd] * OUT_SUB + [proj3] * (2 * OUT_SUB)), x, wa, wd, wo, final_norm_w)


def kernel(x, meta_tokens, norm_w, w_in, lambda_q1, lambda_k1, lambda_q2, lambda_k2, attn_norm_w,
           conv_w, a_log, dt_bias, dn_norm_w, w_branch_attn, w_branch_delta, w_out, final_norm_w):
    b = x.shape[0]
    assert x.shape == (b, SEQ, D_MODEL) and norm_w.shape[0] == 1
    layer = 0
    lam_init = 0.8 - 0.6 * math.exp(-0.3 * layer)

    head = jnp.concatenate([jnp.zeros((FRONT, D_MODEL), x.dtype), meta_tokens.astype(x.dtype)], axis=0)

    wi = w_in[layer]
    gate0 = 8 * D_MODEL
    q_scale = QK_DIM ** -0.5 * math.log2(math.e)
    w_main = jnp.concatenate([wi[:, :D_MODEL] * q_scale, wi[:, D_MODEL:gate0], wi[:, gate0 + 2 * HEADS:]],
                             axis=1).astype(BF16)
    w_gate = wi[:, gate0:gate0 + 2 * HEADS]
    w_gate_p = jnp.pad(w_gate, ((0, 0), (0, LANES - 2 * HEADS))).astype(BF16)
    w_gate_t = w_gate.T.astype(BF16)
    zeros8 = jnp.zeros((HEADS,), F32)
    gp_row = jnp.stack([jnp.pad(jnp.concatenate([zeros8, a_log[layer]]), (0, LANES - 2 * HEADS)),
                        jnp.pad(jnp.concatenate([zeros8, dt_bias[layer]]), (0, LANES - 2 * HEADS))])
    gp_col = jnp.stack([jnp.concatenate([zeros8, a_log[layer]]),
                        jnp.concatenate([zeros8, dt_bias[layer]])])
    gp_col = jnp.broadcast_to(gp_col[:, :, None], (2, 2 * HEADS, DN_T)).astype(F32)
    lam_params = jnp.stack([lambda_q1[layer], lambda_k1[layer], lambda_q2[layer], lambda_k2[layer]])

    proj2, gcol, grow = _inproj(head, x, norm_w[layer][None], w_main, w_gate_p, w_gate_t)
    proj3 = proj2.reshape(b, LP, N_MAIN)

    xa = _attention(proj3, lam_params, attn_norm_w[layer][None], lam_init)

    u, w, qd, kd, a, cd = _dn_prep(proj3, conv_w[layer], gcol.reshape(b, LP, LANES), grow, gp_row, gp_col)
    xd = _dn_scan(u, w, qd, kd, a, cd, proj3, dn_norm_w[layer][None])

    return _merge(xa, xd, proj3, x, w_branch_attn[layer].astype(BF16), w_branch_delta[layer].astype(BF16),
                  w_out[layer].astype(BF16), final_norm_w[None])
```

```python
import functools
import math

import jax
import jax.numpy as jnp
from jax import lax
from jax.experimental import pallas as pl
from jax.experimental.pallas import tpu as pltpu

D_MODEL = 1024
SEQ = 8192
N_META = 16
NORM_EPS = 1e-6
HEADS = 8
HEAD_DIM = 128
QK_DIM = 64
CONV_K = 4
CHUNK = 64
LANES = 128
LP = 8448
FRONT = LP - SEQ - N_META
X0 = FRONT + N_META
N_MAIN = 10 * D_MODEL
NEG = -1e30

C_AQ, C_AK, C_AV, C_AZ, C_DQ, C_DK, C_DV, C_DZ, C_GA, C_GD = range(10)

F32 = jnp.float32
BF16 = jnp.bfloat16


def _rms(x, w):
    return x * lax.rsqrt(jnp.mean(x * x, axis=-1, keepdims=True) + NORM_EPS) * w


def _dot(a, b):
    return jnp.dot(a, b, preferred_element_type=F32)


def _dot_nt(a, b):
    return lax.dot_general(a, b, (((1,), (1,)), ((), ())), preferred_element_type=F32)


def _dot_tn(a, b):
    return lax.dot_general(a, b, (((0,), (0,)), ((), ())), preferred_element_type=F32)


IN_SUB = 3
IN_TM = IN_SUB * X0
IN_TN = 2048
IN_TILES = LP // IN_TM


def _inproj_kernel(*refs):
    head_ref = refs[0]
    x_refs = refs[1:1 + IN_SUB]
    nw_ref, w_ref, wg_ref, wgt_ref, p_ref, gcol_ref, grow_ref, hn_sc = refs[1 + IN_SUB:]

    @pl.when(pl.program_id(1) == 0)
    def _():
        first_tile = pl.program_id(0) % IN_TILES == 0
        nw = nw_ref[...]
        for s, x_ref in enumerate(x_refs):
            h = x_ref[...]
            if s == 0:
                h = jnp.where(first_tile, head_ref[...], h)
            hn_sc[s * X0:(s + 1) * X0, :] = _rms(h, nw).astype(BF16)
        hn = hn_sc[...]
        gcol_ref[...] = _dot(hn, wg_ref[...])
        grow_ref[...] = _dot_nt(wgt_ref[...], hn)

    p_ref[...] = _dot(hn_sc[...], w_ref[...]).astype(BF16)


def _inproj(head, x, norm_w, w_main, w_gate, w_gate_t):
    b = x.shape[0]
    rows = b * LP
    grid = (rows // IN_TM, N_MAIN // IN_TN)

    def xblock(s):
        return pl.BlockSpec((None, X0, D_MODEL),
                            lambda i, j: (i // IN_TILES, jnp.maximum((i % IN_TILES) * IN_SUB + s - 1, 0), 0))

    return pl.pallas_call(
        _inproj_kernel,
        grid=grid,
        in_specs=[pl.BlockSpec((X0, D_MODEL), lambda i, j: (0, 0))] + [xblock(s) for s in range(IN_SUB)] + [
            pl.BlockSpec((1, D_MODEL), lambda i, j: (0, 0)),
            pl.BlockSpec((D_MODEL, IN_TN), lambda i, j: (0, j)),
            pl.BlockSpec((D_MODEL, LANES), lambda i, j: (0, 0)),
            pl.BlockSpec((2 * HEADS, D_MODEL), lambda i, j: (0, 0)),
        ],
        out_specs=[
            pl.BlockSpec((IN_TM, IN_TN), lambda i, j: (i, j)),
            pl.BlockSpec((IN_TM, LANES), lambda i, j: (i, 0)),
            pl.BlockSpec((2 * HEADS, IN_TM), lambda i, j: (0, i)),
        ],
        out_shape=[
            jax.ShapeDtypeStruct((rows, N_MAIN), BF16),
            jax.ShapeDtypeStruct((rows, LANES), F32),
            jax.ShapeDtypeStruct((2 * HEADS, rows), F32),
        ],
        scratch_shapes=[pltpu.VMEM((IN_TM, D_MODEL), BF16)],
        compiler_params=pltpu.CompilerParams(
            dimension_semantics=("parallel", "arbitrary"),
            vmem_limit_bytes=40 * 1024 * 1024),
        name="inproj",
    )(head, *([x] * IN_SUB), norm_w, w_main, w_gate, w_gate_t)


AT_SUB = 2
AT_TQ = AT_SUB * X0
AT_BIG = 4
AT_MID = 2
AT_TILES = -(-LP // AT_TQ)
AT_LA = AT_TILES * AT_TQ
AT_SHIFT_MAX = 30.0


def _attn_kernel(*refs, lam_init):
    lam_ref, q_ref, k_ref, v_ref = refs[0:4]
    z_refs = refs[4:4 + AT_SUB]
    nw_ref = refs[4 + AT_SUB]
    o_ref = refs[5 + AT_SUB]
    qx_sc, kx_sc, vx_sc, e1_sc, e2_sc, flag_sc, m_sc, acc_sc = refs[6 + AT_SUB:]
    i = pl.program_id(2)
    tq = AT_TQ

    @pl.when(i == 0)
    def _():
        k = k_ref[...]
        kx_sc[0:LP, 0:HEAD_DIM] = k
        rowi = lax.broadcasted_iota(jnp.int32, (LP, LANES), 0)
        ln = lax.broadcasted_iota(jnp.int32, (LP, LANES), 1)
        ext = jnp.where(ln == 0, 1.0, jnp.where((ln == 1) & (rowi < FRONT), NEG, 0.0))
        kx_sc[0:LP, HEAD_DIM:] = ext.astype(BF16)
        vx_sc[0:LP, 0:HEAD_DIM] = v_ref[...]
        vx_sc[0:LP, HEAD_DIM:] = jnp.ones((LP, HEAD_DIM), BF16)
        first = ln < QK_DIM
        pad_lane = jnp.where(ln == 1, 1.0, 0.0)

        def half_norms(x):
            sq = x.astype(F32)
            sq = sq * sq
            return (jnp.sqrt(jnp.sum(jnp.where(first, sq, 0.0), axis=-1, keepdims=True)),
                    jnp.sqrt(jnp.sum(jnp.where(first, 0.0, sq), axis=-1, keepdims=True)))

        q = q_ref[...]
        qx_sc[0:LP, :] = q
        kn1, kn2 = half_norms(k)
        qn1, qn2 = half_norms(q)
        shift1 = qn1 * jnp.max(kn1, axis=0, keepdims=True)
        shift2 = qn2 * jnp.max(kn2, axis=0, keepdims=True)
        e1_sc[0:LP, :] = (pad_lane - jnp.where(ln == 0, shift1, 0.0)).astype(BF16)
        e2_sc[0:LP, :] = (pad_lane - jnp.where(ln == 0, shift2, 0.0)).astype(BF16)
        ok = jnp.maximum(jnp.max(shift1), jnp.max(shift2)) <= AT_SHIFT_MAX
        flag_sc[0] = ok.astype(jnp.int32)
        if AT_LA > LP:
            for ref in (qx_sc, kx_sc, vx_sc, e1_sc, e2_sc):
                ref[LP:, :] = jnp.zeros((AT_LA - LP, ref.shape[1]), BF16)

    row0 = pl.multiple_of(i * tq, tq)
    q = qx_sc[pl.ds(row0, tq), :]
    first = lax.broadcasted_iota(jnp.int32, (tq, LANES), 1) < QK_DIM
    zero = jnp.zeros_like(q)
    use_shift = flag_sc[0] == 1
    q2 = jnp.concatenate([jnp.where(first, q, zero), jnp.where(first, zero, q)], axis=0)

    acc_sc[...] = jnp.zeros(acc_sc.shape, F32)

    def causal(s):
        r2 = lax.broadcasted_iota(jnp.int32, (2 * tq, tq), 0)
        row = jnp.where(r2 >= tq, r2 - tq, r2)
        col = lax.broadcasted_iota(jnp.int32, (2 * tq, tq), 1)
        return jnp.where(col <= row, s, NEG)

    def scores(qx, start, units):
        return _dot_nt(qx, kx_sc[pl.ds(start, units * tq), :])

    @pl.when(use_shift)
    def _():
        bias = jnp.concatenate([e1_sc[pl.ds(row0, tq), :], e2_sc[pl.ds(row0, tq), :]], axis=0)
        qx = jnp.concatenate([q2, bias], axis=1)

        def consume(start, units, s):
            p = jnp.exp2(s).astype(BF16)
            acc_sc[...] += _dot(p, vx_sc[pl.ds(start, units * tq), :])

        n_big = i // AT_BIG

        def big(j, c):
            start = pl.multiple_of(j * (AT_BIG * tq), AT_BIG * tq)
            consume(start, AT_BIG, scores(qx, start, AT_BIG))
            return c

        lax.fori_loop(0, n_big, big, 0)
        rem = i - n_big * AT_BIG
        has_mid = rem >= AT_MID
        mid_start = pl.multiple_of(n_big * (AT_BIG * tq), AT_MID * tq)

        @pl.when(has_mid)
        def _():
            consume(mid_start, AT_MID, scores(qx, mid_start, AT_MID))

        tail_units = rem - jnp.where(has_mid, AT_MID, 0) + 1
        tail_start = pl.multiple_of((i + 1 - tail_units) * tq, tq)
        for units in range(1, AT_MID + 1):
            @pl.when(tail_units == units)
            def _():
                s = scores(qx, tail_start, units)
                s_diag = causal(s[:, (units - 1) * tq:])
                if units > 1:
                    s_diag = jnp.concatenate([s[:, :(units - 1) * tq], s_diag], axis=1)
                consume(tail_start, units, s_diag)

    @pl.when(jnp.logical_not(use_shift))
    def _():
        lane2 = lax.broadcasted_iota(jnp.int32, (2 * tq, LANES), 1)
        qx = jnp.concatenate([q2, jnp.where(lane2 == 1, 1.0, 0.0).astype(BF16)], axis=1)
        m_sc[...] = jnp.full(m_sc.shape, NEG, F32)

        def block(j, diag):
            start = pl.multiple_of(j * tq, tq)
            s = scores(qx, start, 1)
            if diag:
                s = causal(s)
            chunks = [s[:, c * LANES:(c + 1) * LANES] for c in range(tq // LANES)]
            mc = chunks[0]
            for ch in chunks[1:]:
                mc = jnp.maximum(mc, ch)
            m_prev = m_sc[...]
            m_new = jnp.maximum(m_prev, jnp.max(mc, axis=-1, keepdims=True))
            alpha = jnp.exp2(m_prev - m_new)
            p = jnp.concatenate([jnp.exp2(ch - m_new) for ch in chunks], axis=1).astype(BF16)
            pv = _dot(p, vx_sc[pl.ds(start, tq), :])
            acc_sc[...] = jnp.concatenate([alpha, alpha], axis=1) * acc_sc[...] + pv
            m_sc[...] = m_new

        def full(j, c):
            block(j, False)
            return c

        lax.fori_loop(0, i, full, 0)
        block(i, True)

    lp = lam_ref[...]
    lam = (jnp.exp(jnp.sum(lp[0:1] * lp[1:2], axis=-1, keepdims=True))
           - jnp.exp(jnp.sum(lp[2:3] * lp[3:4], axis=-1, keepdims=True)) + lam_init)
    acc = acc_sc[...]
    o_all = acc[:, :HEAD_DIM] / jnp.maximum(acc[:, HEAD_DIM:], 1e-37)
    o = o_all[:tq] - lam * o_all[tq:]
    o = _rms(o, nw_ref[...]) * (1.0 - lam_init)
    z = jnp.concatenate([r[...] for r in z_refs], axis=0).astype(F32)
    o_ref[...] = (o * (z * jax.nn.sigmoid(z))).astype(BF16)


def _attention(proj3, lam_params, attn_norm_w, lam_init):
    b = proj3.shape[0]
    hb = D_MODEL // HEAD_DIM
    nblk = LP // X0
    kern = functools.partial(_attn_kernel, lam_init=lam_init)

    def sub(s, col):
        return pl.BlockSpec((None, X0, HEAD_DIM),
                            lambda b_, h, i: (b_, jnp.minimum(AT_SUB * i + s, nblk - 1), col * hb + h))

    return pl.pallas_call(
        kern,
        grid=(b, HEADS, AT_TILES),
        in_specs=[
            pl.BlockSpec((4, QK_DIM), lambda b_, h, i: (0, 0)),
            pl.BlockSpec((None, LP, HEAD_DIM), lambda b_, h, i: (b_, 0, C_AQ * hb + h)),
            pl.BlockSpec((None, LP, HEAD_DIM), lambda b_, h, i: (b_, 0, C_AK * hb + h)),
            pl.BlockSpec((None, LP, HEAD_DIM), lambda b_, h, i: (b_, 0, C_AV * hb + h)),
        ] + [sub(s, C_AZ) for s in range(AT_SUB)] + [
            pl.BlockSpec((1, HEAD_DIM), lambda b_, h, i: (0, 0)),
        ],
        out_specs=pl.BlockSpec((None, AT_TQ, HEAD_DIM), lambda b_, h, i: (b_, i, h)),
        out_shape=jax.ShapeDtypeStruct((b, LP, D_MODEL), BF16),
        scratch_shapes=[
            pltpu.VMEM((AT_LA, HEAD_DIM), BF16),
            pltpu.VMEM((AT_LA, 2 * HEAD_DIM), BF16),
            pltpu.VMEM((AT_LA, 2 * HEAD_DIM), BF16),
            pltpu.VMEM((AT_LA, LANES), BF16),
            pltpu.VMEM((AT_LA, LANES), BF16),
            pltpu.SMEM((1,), jnp.int32),
            pltpu.VMEM((2 * AT_TQ, LANES), F32),
            pltpu.VMEM((2 * AT_TQ, 2 * HEAD_DIM), F32),
        ],
        compiler_params=pltpu.CompilerParams(
            dimension_semantics=("parallel", "parallel", "arbitrary"),
            vmem_limit_bytes=56 * 1024 * 1024),
        name="diff_attn",
    )(lam_params, proj3, proj3, proj3, *([proj3] * AT_SUB), attn_norm_w)


DN_T = 2 * CHUNK
HALO = 16


def _shift_matrix():
    r = jnp.arange((CONV_K - 1) * DN_T)
    src = HALO + r % DN_T - (r // DN_T + 1)
    return (src[:, None] == jnp.arange(HALO + DN_T)[None, :]).astype(BF16)


def _split3(x):
    hi = x.astype(BF16)
    r1 = x - hi.astype(F32)
    mid = r1.astype(BF16)
    lo = (r1 - mid.astype(F32)).astype(BF16)
    return hi, mid, lo


def _dn_prep_kernel(dq_ref, dk_ref, dv_ref, hq_ref, hk_ref, hv_ref, cw_ref, sh_ref, gcol_ref, grow_ref,
                    gpr_ref, gpc_ref, u_ref, w_ref, qd_ref, kd_ref, a_ref, cd_ref):
    i = pl.program_id(1)
    t = DN_T
    width = D_MODEL
    cw = cw_ref[...]
    shifter = sh_ref[...]
    ys = []
    for idx, (t_ref, h_ref) in enumerate(((dq_ref, hq_ref), (dk_ref, hk_ref), (dv_ref, hv_ref))):
        halo = h_ref[...]
        halo = jnp.where(i == 0, jnp.zeros_like(halo), halo)
        x = t_ref[...]
        shifted = _dot(shifter, jnp.concatenate([halo, x], axis=0))
        cwi = cw[:, idx * width:(idx + 1) * width]
        yi = cwi[CONV_K - 1:CONV_K] * x.astype(F32)
        for s in range(1, CONV_K):
            yi = yi + cwi[CONV_K - 1 - s:CONV_K - s] * shifted[(s - 1) * t:s * t]
        ys.append(yi * jax.nn.sigmoid(yi))
    y = jnp.concatenate(ys, axis=1)

    r = lax.broadcasted_iota(jnp.int32, (t, t), 0)
    c = lax.broadcasted_iota(jnp.int32, (t, t), 1)
    same = (r >= CHUNK) == (c >= CHUNK)
    incl = same & (c <= r)
    strict = same & (c < r)
    tri = jnp.where(incl, 1.0, 0.0).astype(BF16)
    tri_t = jnp.where(same & (r <= c), 1.0, 0.0).astype(BF16)
    blk = jnp.where(same, 1.0, 0.0).astype(BF16)
    eye = jnp.where(r == c, 1.0, 0.0).astype(F32)

    lane = lax.broadcasted_iota(jnp.int32, (t, LANES), 1)
    rowi = lax.broadcasted_iota(jnp.int32, (t, LANES), 0) + i * t
    gcol = gcol_ref[...]
    gpr = gpr_ref[...]
    is_g = (lane >= HEADS) & (lane < 2 * HEADS)
    live_c = rowi >= FRONT
    beta_c = jnp.where(live_c, jax.nn.sigmoid(gcol), 0.0)
    xg = gcol + gpr[1:2]
    sp = jnp.maximum(xg, 0.0) + jnp.log1p(jnp.exp(-jnp.abs(xg)))
    g_c = jnp.where(is_g & live_c, -jnp.exp(gpr[0:1]) * sp, 0.0)
    g3 = _split3(g_c)
    gc_c = _dot(tri, g3[0]) + _dot(tri, g3[1]) + _dot(tri, g3[2])
    gl_c = _dot(blk, g3[0]) + _dot(blk, g3[1]) + _dot(blk, g3[2])
    sub = lax.broadcasted_iota(jnp.int32, (2 * HEADS, t), 0)
    coli = lax.broadcasted_iota(jnp.int32, (2 * HEADS, t), 1) + i * t
    xr = grow_ref[...] + gpc_ref[1]
    spr = jnp.maximum(xr, 0.0) + jnp.log1p(jnp.exp(-jnp.abs(xr)))
    g_r = jnp.where((sub >= HEADS) & (coli >= FRONT), -jnp.exp(gpc_ref[0]) * spr, 0.0)
    gr3 = _split3(g_r)
    gc_r = _dot(gr3[0], tri_t) + _dot(gr3[1], tri_t) + _dot(gr3[2], tri_t)

    cd_ref[...] = jnp.exp(gl_c)

    hs = range(HEADS)
    sls = [slice(h * HEAD_DIM, (h + 1) * HEAD_DIM) for h in hs]
    q = [y[:, h * HEAD_DIM:(h + 1) * HEAD_DIM] for h in hs]
    k = [y[:, width + h * HEAD_DIM:width + (h + 1) * HEAD_DIM] for h in hs]
    v = [y[:, 2 * width + h * HEAD_DIM:2 * width + (h + 1) * HEAD_DIM] for h in hs]
    q = [x * lax.rsqrt(jnp.sum(x * x, axis=-1, keepdims=True) + 1e-6) * (HEAD_DIM ** -0.5) for x in q]
    k = [x * lax.rsqrt(jnp.sum(x * x, axis=-1, keepdims=True) + 1e-6) for x in k]
    beta = [beta_c[:, h:h + 1] for h in hs]
    gcc = [gc_c[:, HEADS + h:HEADS + h + 1] for h in hs]
    glc = [gl_c[:, HEADS + h:HEADS + h + 1] for h in hs]
    gcr = [gc_r[HEADS + h:HEADS + h + 1, :] for h in hs]
    dec = [jnp.where(incl, jnp.exp(jnp.where(incl, gcc[h] - gcr[h], 0.0)), 0.0) for h in hs]
    kb = [k[h] * beta[h] for h in hs]
    kt = [x.T.astype(BF16) for x in k]
    m = [jnp.where(strict, _dot(kb[h].astype(BF16), kt[h]) * dec[h], 0.0) for h in hs]
    tinv = [eye - x for x in m]
    pwb = [x.astype(BF16) for x in m]
    for _ in range(5):
        pw = [_dot(x, x) for x in pwb]
        pwb = [x.astype(BF16) for x in pw]
        tinv = [tinv[h] + _dot(tinv[h].astype(BF16), pwb[h]) for h in hs]
    egc = [jnp.exp(x) for x in gcc]
    rhs = [jnp.concatenate([v[h] * beta[h], kb[h] * egc[h]], axis=1).astype(BF16) for h in hs]
    uw = [_dot(tinv[h].astype(BF16), rhs[h]) for h in hs]
    a_full = [_dot(q[h].astype(BF16), kt[h]) * dec[h] for h in hs]
    for h in hs:
        u_ref[:, sls[h]] = uw[h][:, :HEAD_DIM]
        w_ref[:, sls[h]] = uw[h][:, HEAD_DIM:].astype(BF16)
        a_cmp = a_full[h] + pltpu.roll(a_full[h], CHUNK, axis=1)
        a_ref[:, h * CHUNK:(h + 1) * CHUNK] = a_cmp[:, :CHUNK].astype(BF16)
        qd_ref[:, sls[h]] = (q[h] * egc[h]).astype(BF16)
        kd_ref[:, sls[h]] = (k[h] * jnp.exp(glc[h] - gcc[h])).astype(BF16)


def _dn_prep(proj3, conv_w2, gcol3, grow, gp_row, gp_col):
    b = proj3.shape[0]
    t = DN_T
    nt = LP // t
    hpb = t // HALO

    def tile(cb):
        return pl.BlockSpec((None, t, D_MODEL), lambda b_, i: (b_, i, cb))

    def halo(cb):
        return pl.BlockSpec((None, HALO, D_MODEL), lambda b_, i: (b_, jnp.maximum(i * hpb - 1, 0), cb))

    full = pl.BlockSpec((None, t, D_MODEL), lambda b_, i: (b_, i, 0))
    return pl.pallas_call(
        _dn_prep_kernel,
        grid=(b, nt),
        in_specs=[
            tile(C_DQ), tile(C_DK), tile(C_DV), halo(C_DQ), halo(C_DK), halo(C_DV),
            pl.BlockSpec((CONV_K, 3 * D_MODEL), lambda b_, i: (0, 0)),
            pl.BlockSpec(((CONV_K - 1) * t, HALO + t), lambda b_, i: (0, 0)),
            pl.BlockSpec((None, t, LANES), lambda b_, i: (b_, i, 0)),
            pl.BlockSpec((2 * HEADS, t), lambda b_, i: (0, b_ * nt + i)),
            pl.BlockSpec((2, LANES), lambda b_, i: (0, 0)),
            pl.BlockSpec((2, 2 * HEADS, t), lambda b_, i: (0, 0, 0)),
        ],
        out_specs=[
            full, full, full, full,
            pl.BlockSpec((None, t, HEADS * CHUNK), lambda b_, i: (b_, i, 0)),
            pl.BlockSpec((None, t, LANES), lambda b_, i: (b_, i, 0)),
        ],
        out_shape=[
            jax.ShapeDtypeStruct((b, LP, D_MODEL), F32),
            jax.ShapeDtypeStruct((b, LP, D_MODEL), BF16),
            jax.ShapeDtypeStruct((b, LP, D_MODEL), BF16),
            jax.ShapeDtypeStruct((b, LP, D_MODEL), BF16),
            jax.ShapeDtypeStruct((b, LP, HEADS * CHUNK), BF16),
            jax.ShapeDtypeStruct((b, LP, LANES), F32),
        ],
        compiler_params=pltpu.CompilerParams(
            dimension_semantics=("parallel", "parallel"),
            vmem_limit_bytes=40 * 1024 * 1024),
        name="dn_prep",
    )(proj3, proj3, proj3, proj3, proj3, proj3, conv_w2, _shift_matrix(), gcol3, grow, gp_row, gp_col)


SC_T = 4 * CHUNK


def _dn_scan_kernel(u_ref, w_ref, qd_ref, kd_ref, a_ref, cd_ref, z_ref, nw_ref, o_ref, s_sc):
    @pl.when(pl.program_id(1) == 0)
    def _():
        s_sc[...] = jnp.zeros(s_sc.shape, F32)

    nw = nw_ref[...]
    hs = range(HEADS)
    sls = [slice(h * HEAD_DIM, (h + 1) * HEAD_DIM) for h in hs]
    s = [s_sc[h] for h in hs]
    for cix in range(SC_T // CHUNK):
        rs = slice(cix * CHUNK, (cix + 1) * CHUNK)
        sb = [x.astype(BF16) for x in s]
        ws_qs = [_dot(jnp.concatenate([w_ref[rs, sls[h]], qd_ref[rs, sls[h]]], axis=0), sb[h]) for h in hs]
        vb = [(u_ref[rs, sls[h]] - ws_qs[h][:CHUNK]).astype(BF16) for h in hs]
        o = [ws_qs[h][CHUNK:] + _dot(a_ref[rs, h * CHUNK:(h + 1) * CHUNK], vb[h]) for h in hs]
        cd = [cd_ref[cix * CHUNK:cix * CHUNK + 1, HEADS + h:HEADS + h + 1] for h in hs]
        s = [s[h] * cd[h] + _dot_tn(kd_ref[rs, sls[h]], vb[h]) for h in hs]
        for h in hs:
            z = z_ref[rs, sls[h]].astype(F32)
            o_ref[rs, sls[h]] = (_rms(o[h], nw) * (z * jax.nn.sigmoid(z))).astype(BF16)
    for h in hs:
        s_sc[h] = s[h]


def _dn_scan(u, w, qd, kd, a, cd, proj3, dn_norm_w):
    b = u.shape[0]
    t = SC_T
    full = pl.BlockSpec((None, t, D_MODEL), lambda b_, i: (b_, i, 0))
    return pl.pallas_call(
        _dn_scan_kernel,
        grid=(b, LP // t),
        in_specs=[
            full, full, full, full,
            pl.BlockSpec((None, t, HEADS * CHUNK), lambda b_, i: (b_, i, 0)),
            pl.BlockSpec((None, t, LANES), lambda b_, i: (b_, i, 0)),
            pl.BlockSpec((None, t, D_MODEL), lambda b_, i: (b_, i, C_DZ)),
            pl.BlockSpec((1, HEAD_DIM), lambda b_, i: (0, 0)),
        ],
        out_specs=full,
        out_shape=jax.ShapeDtypeStruct((b, LP, D_MODEL), BF16),
        scratch_shapes=[pltpu.VMEM((HEADS, HEAD_DIM, HEAD_DIM), F32)],
        compiler_params=pltpu.CompilerParams(
            dimension_semantics=("parallel", "arbitrary"),
            vmem_limit_bytes=40 * 1024 * 1024),
        name="dn_scan",
    )(u, w, qd, kd, a, cd, proj3, dn_norm_w)


OUT_SUB = 2


def _merge_kernel(*refs):
    n = OUT_SUB
    xa_refs, xd_refs, ga_refs, gd_refs = refs[0:n], refs[n:2 * n], refs[2 * n:3 * n], refs[3 * n:4 * n]
    x_ref, wa_ref, wd_ref, wo_ref, nw_ref, o_ref = refs[4 * n:]

    def rows(rs):
        return jnp.concatenate([r[...] for r in rs], axis=0)

    ya = _dot(rows(xa_refs), wa_ref[...])
    yd = _dot(rows(xd_refs), wd_ref[...])
    merged = (jax.nn.sigmoid(rows(ga_refs).astype(F32)) * ya
              + jax.nn.sigmoid(rows(gd_refs).astype(F32)) * yd)
    out = x_ref[...] + _dot(merged.astype(BF16), wo_ref[...])
    o_ref[...] = _rms(out, nw_ref[...])


def _merge(xa, xd, proj3, x, wa, wd, wo, final_norm_w):
    b = x.shape[0]
    tm = OUT_SUB * X0

    def padded(cb):
        return [pl.BlockSpec((None, X0, D_MODEL), lambda b_, i, s=s: (b_, OUT_SUB * i + 1 + s, cb))
                for s in range(OUT_SUB)]

    tok = pl.BlockSpec((None, tm, D_MODEL), lambda b_, i: (b_, i, 0))
    wspec = pl.BlockSpec((D_MODEL, D_MODEL), lambda b_, i: (0, 0))
    return pl.pallas_call(
        _merge_kernel,
        grid=(b, SEQ // tm),
        in_specs=padded(0) + padded(0) + padded(C_GA) + padded(C_GD) + [
            tok, wspec, wspec, wspec,
            pl.BlockSpec((1, D_MODEL), lambda b_, i: (0, 0)),
        ],
        out_specs=tok,
        out_shape=jax.ShapeDtypeStruct((b, SEQ, D_MODEL), F32),
        compiler_params=pltpu.CompilerParams(
            dimension_semantics=("parallel", "parallel"),
            vmem_limit_bytes=48 * 1024 * 1024),
        name="merge_out",
    )(*([xa] * OUT_SUB + [xd] * OUT_SUB + [proj3] * (2 * OUT_SUB)), x, wa, wd, wo, final_norm_w)


def kernel(x, meta_tokens, norm_w, w_in, lambda_q1, lambda_k1, lambda_q2, lambda_k2, attn_norm_w,
           conv_w, a_log, dt_bias, dn_norm_w, w_branch_attn, w_branch_delta, w_out, final_norm_w):
    b = x.shape[0]
    assert x.shape == (b, SEQ, D_MODEL) and norm_w.shape[0] == 1
    layer = 0
    lam_init = 0.8 - 0.6 * math.exp(-0.3 * layer)

    head = jnp.concatenate([jnp.zeros((FRONT, D_MODEL), x.dtype), meta_tokens.astype(x.dtype)], axis=0)

    wi = w_in[layer]
    gate0 = 8 * D_MODEL
    q_scale = QK_DIM ** -0.5 * math.log2(math.e)
    w_main = jnp.concatenate([wi[:, :D_MODEL] * q_scale, wi[:, D_MODEL:gate0], wi[:, gate0 + 2 * HEADS:]],
                             axis=1).astype(BF16)
    w_gate = wi[:, gate0:gate0 + 2 * HEADS]
    w_gate_p = jnp.pad(w_gate, ((0, 0), (0, LANES - 2 * HEADS))).astype(BF16)
    w_gate_t = w_gate.T.astype(BF16)
    zeros8 = jnp.zeros((HEADS,), F32)
    gp_row = jnp.stack([jnp.pad(jnp.concatenate([zeros8, a_log[layer]]), (0, LANES - 2 * HEADS)),
                        jnp.pad(jnp.concatenate([zeros8, dt_bias[layer]]), (0, LANES - 2 * HEADS))])
    gp_col = jnp.stack([jnp.concatenate([zeros8, a_log[layer]]),
                        jnp.concatenate([zeros8, dt_bias[layer]])])
    gp_col = jnp.broadcast_to(gp_col[:, :, None], (2, 2 * HEADS, DN_T)).astype(F32)
    lam_params = jnp.stack([lambda_q1[layer], lambda_k1[layer], lambda_q2[layer], lambda_k2[layer]])

    proj2, gcol, grow = _inproj(head, x, norm_w[layer][None], w_main, w_gate_p, w_gate_t)
    proj3 = proj2.reshape(b, LP, N_MAIN)

    xa = _attention(proj3, lam_params, attn_norm_w[layer][None], lam_init)

    u, w, qd, kd, a, cd = _dn_prep(proj3, conv_w[layer], gcol.reshape(b, LP, LANES), grow, gp_row, gp_col)
    xd = _dn_scan(u, w, qd, kd, a, cd, proj3, dn_norm_w[layer][None])

    return _merge(xa, xd, proj3, x, w_branch_attn[layer].astype(BF16), w_branch_delta[layer].astype(BF16),
                  w_out[layer].astype(BF16), final_norm_w[None])
```

```python
import functools
import math

import jax
import jax.numpy as jnp
from jax import lax
from jax.experimental import pallas as pl
from jax.experimental.pallas import tpu as pltpu

D_MODEL = 1024
SEQ = 8192
N_META = 16
NORM_EPS = 1e-6
HEADS = 8
HEAD_DIM = 128
QK_DIM = 64
CONV_K = 4
CHUNK = 64
LANES = 128
LP = 8448
FRONT = LP - SEQ - N_META
X0 = FRONT + N_META
N_MAIN = 10 * D_MODEL
NEG = -1e30

C_AQ, C_AK, C_AV, C_AZ, C_DQ, C_DK, C_DV, C_DZ, C_GA, C_GD = range(10)

F32 = jnp.float32
BF16 = jnp.bfloat16


def _rms(x, w):
    return x * lax.rsqrt(jnp.mean(x * x, axis=-1, keepdims=True) + NORM_EPS) * w


def _dot(a, b):
    return jnp.dot(a, b, preferred_element_type=F32)


def _dot_nt(a, b):
    return lax.dot_general(a, b, (((1,), (1,)), ((), ())), preferred_element_type=F32)


def _dot_tn(a, b):
    return lax.dot_general(a, b, (((0,), (0,)), ((), ())), preferred_element_type=F32)


IN_SUB = 3
IN_TM = IN_SUB * X0
IN_TN = 2048
IN_TILES = LP // IN_TM
IN_FULL_TILES = 8 * D_MODEL // IN_TN


def _inproj_kernel(*refs):
    head_ref = refs[0]
    x_refs = refs[1:1 + IN_SUB]
    nw_ref, w_ref, wt_ref, cs_ref, wg_ref, wgt_ref, p_ref, gcol_ref, grow_ref, hn_sc = refs[1 + IN_SUB:]
    j = pl.program_id(1)

    @pl.when(j == 0)
    def _():
        first_tile = pl.program_id(0) % IN_TILES == 0
        nw = nw_ref[...]
        for s, x_ref in enumerate(x_refs):
            h = x_ref[...]
            if s == 0:
                h = jnp.where(first_tile, head_ref[...], h)
            hn_sc[s * X0:(s + 1) * X0, :] = _rms(h, nw).astype(BF16)
        hn = hn_sc[...]
        gcol_ref[...] = _dot(hn, wg_ref[...])
        grow_ref[...] = _dot_nt(wgt_ref[...], hn)

    @pl.when(j < IN_FULL_TILES)
    def _():
        p_ref[...] = (_dot(hn_sc[...], w_ref[...]) * cs_ref[...]).astype(BF16)

    @pl.when(j == IN_FULL_TILES)
    def _():
        p_ref[...] = _dot(hn_sc[...], wt_ref[...]).astype(BF16)


def _inproj(head, x, norm_w, w_all, w_tail, col_scale, w_gate, w_gate_t):
    b = x.shape[0]
    rows = b * LP
    grid = (rows // IN_TM, N_MAIN // IN_TN)
    assert N_MAIN - IN_FULL_TILES * IN_TN == IN_TN

    def xblock(s):
        return pl.BlockSpec((None, X0, D_MODEL),
                            lambda i, j: (i // IN_TILES, jnp.maximum((i % IN_TILES) * IN_SUB + s - 1, 0), 0))

    return pl.pallas_call(
        _inproj_kernel,
        grid=grid,
        in_specs=[pl.BlockSpec((X0, D_MODEL), lambda i, j: (0, 0))] + [xblock(s) for s in range(IN_SUB)] + [
            pl.BlockSpec((1, D_MODEL), lambda i, j: (0, 0)),
            pl.BlockSpec((D_MODEL, IN_TN), lambda i, j: (0, jnp.minimum(j, IN_FULL_TILES - 1))),
            pl.BlockSpec((D_MODEL, IN_TN), lambda i, j: (0, 0)),
            pl.BlockSpec((1, IN_TN), lambda i, j: (0, jnp.minimum(j, IN_FULL_TILES - 1))),
            pl.BlockSpec((D_MODEL, LANES), lambda i, j: (0, 0)),
            pl.BlockSpec((2 * HEADS, D_MODEL), lambda i, j: (0, 0)),
        ],
        out_specs=[
            pl.BlockSpec((IN_TM, IN_TN), lambda i, j: (i, j)),
            pl.BlockSpec((IN_TM, LANES), lambda i, j: (i, 0)),
            pl.BlockSpec((2 * HEADS, IN_TM), lambda i, j: (0, i)),
        ],
        out_shape=[
            jax.ShapeDtypeStruct((rows, N_MAIN), BF16),
            jax.ShapeDtypeStruct((rows, LANES), F32),
            jax.ShapeDtypeStruct((2 * HEADS, rows), F32),
        ],
        scratch_shapes=[pltpu.VMEM((IN_TM, D_MODEL), BF16)],
        compiler_params=pltpu.CompilerParams(
            dimension_semantics=("parallel", "arbitrary"),
            vmem_limit_bytes=40 * 1024 * 1024),
        name="inproj",
    )(head, *([x] * IN_SUB), norm_w, w_all, w_tail, col_scale, w_gate, w_gate_t)


AT_SUB = 2
AT_TQ = AT_SUB * X0
AT_BIG = 4
AT_MID = 2
AT_TILES = -(-LP // AT_TQ)
AT_LA = AT_TILES * AT_TQ
AT_SHIFT_MAX = 30.0


def _attn_kernel(*refs, lam_init):
    lam_ref, q_ref, k_ref, v_ref = refs[0:4]
    z_refs = refs[4:4 + AT_SUB]
    nw_ref = refs[4 + AT_SUB]
    o_ref = refs[5 + AT_SUB]
    qx_sc, kx_sc, vx_sc, e1_sc, e2_sc, flag_sc, m_sc, acc_sc = refs[6 + AT_SUB:]
    i = pl.program_id(2)
    tq = AT_TQ

    @pl.when((pl.program_id(0) == 0) & (pl.program_id(1) == 0) & (i == 0))
    def _():
        rowi = lax.broadcasted_iota(jnp.int32, (LP, LANES), 0)
        ln = lax.broadcasted_iota(jnp.int32, (LP, LANES), 1)
        ext = jnp.where(ln == 0, 1.0, jnp.where((ln == 1) & (rowi < FRONT), NEG, 0.0))
        kx_sc[0:LP, HEAD_DIM:] = ext.astype(BF16)
        vx_sc[0:LP, HEAD_DIM:] = jnp.ones((LP, HEAD_DIM), BF16)
        if AT_LA > LP:
            for ref in (qx_sc, kx_sc, vx_sc, e1_sc, e2_sc):
                ref[LP:, :] = jnp.zeros((AT_LA - LP, ref.shape[1]), BF16)

    @pl.when(i == 0)
    def _():
        k = k_ref[...]
        kx_sc[0:LP, 0:HEAD_DIM] = k
        vx_sc[0:LP, 0:HEAD_DIM] = v_ref[...]
        ln = lax.broadcasted_iota(jnp.int32, (LP, LANES), 1)
        first = ln < QK_DIM
        pad_lane = jnp.where(ln == 1, 1.0, 0.0)

        def half_norms(x):
            sq = x.astype(F32)
            sq = sq * sq
            return (jnp.sqrt(jnp.sum(jnp.where(first, sq, 0.0), axis=-1, keepdims=True)),
                    jnp.sqrt(jnp.sum(jnp.where(first, 0.0, sq), axis=-1, keepdims=True)))

        q = q_ref[...]
        qx_sc[0:LP, :] = q
        kn1, kn2 = half_norms(k)
        qn1, qn2 = half_norms(q)
        shift1 = qn1 * jnp.max(kn1, axis=0, keepdims=True)
        shift2 = qn2 * jnp.max(kn2, axis=0, keepdims=True)
        e1_sc[0:LP, :] = (pad_lane - jnp.where(ln == 0, shift1, 0.0)).astype(BF16)
        e2_sc[0:LP, :] = (pad_lane - jnp.where(ln == 0, shift2, 0.0)).astype(BF16)
        ok = jnp.maximum(jnp.max(shift1), jnp.max(shift2)) <= AT_SHIFT_MAX
        flag_sc[0] = ok.astype(jnp.int32)

    row0 = pl.multiple_of(i * tq, tq)
    q = qx_sc[pl.ds(row0, tq), :]
    first = lax.broadcasted_iota(jnp.int32, (tq, LANES), 1) < QK_DIM
    zero = jnp.zeros_like(q)
    use_shift = flag_sc[0] == 1
    q2 = jnp.concatenate([jnp.where(first, q, zero), jnp.where(first, zero, q)], axis=0)

    acc_sc[...] = jnp.zeros(acc_sc.shape, F32)

    def causal(s):
        r2 = lax.broadcasted_iota(jnp.int32, (2 * tq, tq), 0)
        row = jnp.where(r2 >= tq, r2 - tq, r2)
        col = lax.broadcasted_iota(jnp.int32, (2 * tq, tq), 1)
        return jnp.where(col <= row, s, NEG)

    def scores(qx, start, units):
        return _dot_nt(qx, kx_sc[pl.ds(start, units * tq), :])

    @pl.when(use_shift)
    def _():
        bias = jnp.concatenate([e1_sc[pl.ds(row0, tq), :], e2_sc[pl.ds(row0, tq), :]], axis=0)
        qx = jnp.concatenate([q2, bias], axis=1)

        def consume(start, units, s):
            p = jnp.exp2(s).astype(BF16)
            acc_sc[...] += _dot(p, vx_sc[pl.ds(start, units * tq), :])

        n_big = i // AT_BIG

        def big(j, c):
            start = pl.multiple_of(j * (AT_BIG * tq), AT_BIG * tq)
            consume(start, AT_BIG, scores(qx, start, AT_BIG))
            return c

        lax.fori_loop(0, n_big, big, 0)
        rem = i - n_big * AT_BIG
        has_mid = rem >= AT_MID
        mid_start = pl.multiple_of(n_big * (AT_BIG * tq), AT_MID * tq)

        @pl.when(has_mid)
        def _():
            consume(mid_start, AT_MID, scores(qx, mid_start, AT_MID))

        tail_units = rem - jnp.where(has_mid, AT_MID, 0) + 1
        tail_start = pl.multiple_of((i + 1 - tail_units) * tq, tq)
        for units in range(1, AT_MID + 1):
            @pl.when(tail_units == units)
            def _():
                s = scores(qx, tail_start, units)
                s_diag = causal(s[:, (units - 1) * tq:])
                if units > 1:
                    s_diag = jnp.concatenate([s[:, :(units - 1) * tq], s_diag], axis=1)
                consume(tail_start, units, s_diag)

    @pl.when(jnp.logical_not(use_shift))
    def _():
        lane2 = lax.broadcasted_iota(jnp.int32, (2 * tq, LANES), 1)
        qx = jnp.concatenate([q2, jnp.where(lane2 == 1, 1.0, 0.0).astype(BF16)], axis=1)
        m_sc[...] = jnp.full(m_sc.shape, NEG, F32)

        def block(j, diag):
            start = pl.multiple_of(j * tq, tq)
            s = scores(qx, start, 1)
            if diag:
                s = causal(s)
            chunks = [s[:, c * LANES:(c + 1) * LANES] for c in range(tq // LANES)]
            mc = chunks[0]
            for ch in chunks[1:]:
                mc = jnp.maximum(mc, ch)
            m_prev = m_sc[...]
            m_new = jnp.maximum(m_prev, jnp.max(mc, axis=-1, keepdims=True))
            alpha = jnp.exp2(m_prev - m_new)
            p = jnp.concatenate([jnp.exp2(ch - m_new) for ch in chunks], axis=1).astype(BF16)
            pv = _dot(p, vx_sc[pl.ds(start, tq), :])
            acc_sc[...] = jnp.concatenate([alpha, alpha], axis=1) * acc_sc[...] + pv
            m_sc[...] = m_new

        def full(j, c):
            block(j, False)
            return c

        lax.fori_loop(0, i, full, 0)
        block(i, True)

    lp = lam_ref[...]
    lam = (jnp.exp(jnp.sum(lp[0:1] * lp[1:2], axis=-1, keepdims=True))
           - jnp.exp(jnp.sum(lp[2:3] * lp[3:4], axis=-1, keepdims=True)) + lam_init)
    acc = acc_sc[...]
    o_all = acc[:, :HEAD_DIM] / jnp.maximum(acc[:, HEAD_DIM:], 1e-37)
    o = o_all[:tq] - lam * o_all[tq:]
    o = _rms(o, nw_ref[...]) * (1.0 - lam_init)
    z = jnp.concatenate([r[...] for r in z_refs], axis=0).astype(F32)
    o_ref[...] = (o * (z * jax.nn.sigmoid(z))).astype(BF16)


def _attention(proj3, lam_params, attn_norm_w, lam_init):
    b = proj3.shape[0]
    hb = D_MODEL // HEAD_DIM
    nblk = LP // X0
    kern = functools.partial(_attn_kernel, lam_init=lam_init)

    def sub(s, col):
        return pl.BlockSpec((None, X0, HEAD_DIM),
                            lambda b_, h, i: (b_, jnp.minimum(AT_SUB * i + s, nblk - 1), col * hb + h))

    return pl.pallas_call(
        kern,
        grid=(b, HEADS, AT_TILES),
        in_specs=[
            pl.BlockSpec((4, QK_DIM), lambda b_, h, i: (0, 0)),
            pl.BlockSpec((None, LP, HEAD_DIM), lambda b_, h, i: (b_, 0, C_AQ * hb + h)),
            pl.BlockSpec((None, LP, HEAD_DIM), lambda b_, h, i: (b_, 0, C_AK * hb + h)),
            pl.BlockSpec((None, LP, HEAD_DIM), lambda b_, h, i: (b_, 0, C_AV * hb + h)),
        ] + [sub(s, C_AZ) for s in range(AT_SUB)] + [
            pl.BlockSpec((1, HEAD_DIM), lambda b_, h, i: (0, 0)),
        ],
        out_specs=pl.BlockSpec((None, AT_TQ, HEAD_DIM), lambda b_, h, i: (b_, i, h)),
        out_shape=jax.ShapeDtypeStruct((b, LP, D_MODEL), BF16),
        scratch_shapes=[
            pltpu.VMEM((AT_LA, HEAD_DIM), BF16),
            pltpu.VMEM((AT_LA, 2 * HEAD_DIM), BF16),
            pltpu.VMEM((AT_LA, 2 * HEAD_DIM), BF16),
            pltpu.VMEM((AT_LA, LANES), BF16),
            pltpu.VMEM((AT_LA, LANES), BF16),
            pltpu.SMEM((1,), jnp.int32),
            pltpu.VMEM((2 * AT_TQ, LANES), F32),
            pltpu.VMEM((2 * AT_TQ, 2 * HEAD_DIM), F32),
        ],
        compiler_params=pltpu.CompilerParams(
            dimension_semantics=("arbitrary", "arbitrary", "arbitrary"),
            vmem_limit_bytes=56 * 1024 * 1024),
        name="diff_attn",
    )(lam_params, proj3, proj3, proj3, *([proj3] * AT_SUB), attn_norm_w)


DN_T = 2 * CHUNK
HALO = 16


def _shift_matrix():
    r = jnp.arange((CONV_K - 1) * DN_T)
    src = HALO + r % DN_T - (r // DN_T + 1)
    return (src[:, None] == jnp.arange(HALO + DN_T)[None, :]).astype(BF16)


def _split3(x):
    hi = x.astype(BF16)
    r1 = x - hi.astype(F32)
    mid = r1.astype(BF16)
    lo = (r1 - mid.astype(F32)).astype(BF16)
    return hi, mid, lo


def _dn_prep_kernel(dq_ref, dk_ref, dv_ref, hq_ref, hk_ref, hv_ref, cw_ref, sh_ref, gcol_ref, grow_ref,
                    gpr_ref, gpc_ref, u_ref, w_ref, qd_ref, kd_ref, a_ref, cd_ref):
    i = pl.program_id(1)
    t = DN_T
    width = D_MODEL
    cw = cw_ref[...]
    shifter = sh_ref[...]
    ys = []
    for idx, (t_ref, h_ref) in enumerate(((dq_ref, hq_ref), (dk_ref, hk_ref), (dv_ref, hv_ref))):
        halo = h_ref[...]
        halo = jnp.where(i == 0, jnp.zeros_like(halo), halo)
        x = t_ref[...]
        shifted = _dot(shifter, jnp.concatenate([halo, x], axis=0))
        cwi = cw[:, idx * width:(idx + 1) * width]
        yi = cwi[CONV_K - 1:CONV_K] * x.astype(F32)
        for s in range(1, CONV_K):
            yi = yi + cwi[CONV_K - 1 - s:CONV_K - s] * shifted[(s - 1) * t:s * t]
        ys.append(yi * jax.nn.sigmoid(yi))
    y = jnp.concatenate(ys, axis=1)

    r = lax.broadcasted_iota(jnp.int32, (t, t), 0)
    c = lax.broadcasted_iota(jnp.int32, (t, t), 1)
    same = (r >= CHUNK) == (c >= CHUNK)
    incl = same & (c <= r)
    strict = same & (c < r)
    tri = jnp.where(incl, 1.0, 0.0).astype(BF16)
    tri_t = jnp.where(same & (r <= c), 1.0, 0.0).astype(BF16)
    blk = jnp.where(same, 1.0, 0.0).astype(BF16)
    eye = jnp.where(r == c, 1.0, 0.0).astype(F32)

    lane = lax.broadcasted_iota(jnp.int32, (t, LANES), 1)
    rowi = lax.broadcasted_iota(jnp.int32, (t, LANES), 0) + i * t
    gcol = gcol_ref[...]
    gpr = gpr_ref[...]
    is_g = (lane >= HEADS) & (lane < 2 * HEADS)
    live_c = rowi >= FRONT
    beta_c = jnp.where(live_c, jax.nn.sigmoid(gcol), 0.0)
    xg = gcol + gpr[1:2]
    sp = jnp.maximum(xg, 0.0) + jnp.log1p(jnp.exp(-jnp.abs(xg)))
    g_c = jnp.where(is_g & live_c, -jnp.exp(gpr[0:1]) * sp, 0.0)
    g3 = _split3(g_c)
    gc_c = _dot(tri, g3[0]) + _dot(tri, g3[1]) + _dot(tri, g3[2])
    gl_c = _dot(blk, g3[0]) + _dot(blk, g3[1]) + _dot(blk, g3[2])
    sub = lax.broadcasted_iota(jnp.int32, (2 * HEADS, t), 0)
    coli = lax.broadcasted_iota(jnp.int32, (2 * HEADS, t), 1) + i * t
    xr = grow_ref[...] + gpc_ref[1]
    spr = jnp.maximum(xr, 0.0) + jnp.log1p(jnp.exp(-jnp.abs(xr)))
    g_r = jnp.where((sub >= HEADS) & (coli >= FRONT), -jnp.exp(gpc_ref[0]) * spr, 0.0)
    gr3 = _split3(g_r)
    gc_r = _dot(gr3[0], tri_t) + _dot(gr3[1], tri_t) + _dot(gr3[2], tri_t)

    cd_ref[...] = jnp.exp(gl_c)

    hs = range(HEADS)
    sls = [slice(h * HEAD_DIM, (h + 1) * HEAD_DIM) for h in hs]
    q = [y[:, h * HEAD_DIM:(h + 1) * HEAD_DIM] for h in hs]
    k = [y[:, width + h * HEAD_DIM:width + (h + 1) * HEAD_DIM] for h in hs]
    v = [y[:, 2 * width + h * HEAD_DIM:2 * width + (h + 1) * HEAD_DIM] for h in hs]
    q = [x * lax.rsqrt(jnp.sum(x * x, axis=-1, keepdims=True) + 1e-6) * (HEAD_DIM ** -0.5) for x in q]
    k = [x * lax.rsqrt(jnp.sum(x * x, axis=-1, keepdims=True) + 1e-6) for x in k]
    beta = [beta_c[:, h:h + 1] for h in hs]
    gcc = [gc_c[:, HEADS + h:HEADS + h + 1] for h in hs]
    glc = [gl_c[:, HEADS + h:HEADS + h + 1] for h in hs]
    gcr = [gc_r[HEADS + h:HEADS + h + 1, :] for h in hs]
    dec = [jnp.where(incl, jnp.exp(jnp.where(incl, gcc[h] - gcr[h], 0.0)), 0.0) for h in hs]
    kb = [k[h] * beta[h] for h in hs]
    kt = [x.T.astype(BF16) for x in k]
    m = [jnp.where(strict, _dot(kb[h].astype(BF16), kt[h]) * dec[h], 0.0) for h in hs]
    tinv = [eye - x for x in m]
    pwb = [x.astype(BF16) for x in m]
    for _ in range(5):
        pw = [_dot(x, x) for x in pwb]
        pwb = [x.astype(BF16) for x in pw]
        tinv = [tinv[h] + _dot(tinv[h].astype(BF16), pwb[h]) for h in hs]
    egc = [jnp.exp(x) for x in gcc]
    rhs = [jnp.concatenate([v[h] * beta[h], kb[h] * egc[h]], axis=1).astype(BF16) for h in hs]
    uw = [_dot(tinv[h].astype(BF16), rhs[h]) for h in hs]
    a_full = [_dot(q[h].astype(BF16), kt[h]) * dec[h] for h in hs]
    for h in hs:
        u_ref[:, sls[h]] = uw[h][:, :HEAD_DIM].astype(BF16)
        w_ref[:, sls[h]] = uw[h][:, HEAD_DIM:].astype(BF16)
        a_cmp = a_full[h] + pltpu.roll(a_full[h], CHUNK, axis=1)
        a_ref[:, h * CHUNK:(h + 1) * CHUNK] = a_cmp[:, :CHUNK].astype(BF16)
        qd_ref[:, sls[h]] = (q[h] * egc[h]).astype(BF16)
        kd_ref[:, sls[h]] = (k[h] * jnp.exp(glc[h] - gcc[h])).astype(BF16)


def _dn_prep(proj3, conv_w2, gcol3, grow, gp_row, gp_col):
    b = proj3.shape[0]
    t = DN_T
    nt = LP // t
    hpb = t // HALO

    def tile(cb):
        return pl.BlockSpec((None, t, D_MODEL), lambda b_, i: (b_, i, cb))

    def halo(cb):
        return pl.BlockSpec((None, HALO, D_MODEL), lambda b_, i: (b_, jnp.maximum(i * hpb - 1, 0), cb))

    full = pl.BlockSpec((None, t, D_MODEL), lambda b_, i: (b_, i, 0))
    return pl.pallas_call(
        _dn_prep_kernel,
        grid=(b, nt),
        in_specs=[
            tile(C_DQ), tile(C_DK), tile(C_DV), halo(C_DQ), halo(C_DK), halo(C_DV),
            pl.BlockSpec((CONV_K, 3 * D_MODEL), lambda b_, i: (0, 0)),
            pl.BlockSpec(((CONV_K - 1) * t, HALO + t), lambda b_, i: (0, 0)),
            pl.BlockSpec((None, t, LANES), lambda b_, i: (b_, i, 0)),
            pl.BlockSpec((2 * HEADS, t), lambda b_, i: (0, b_ * nt + i)),
            pl.BlockSpec((2, LANES), lambda b_, i: (0, 0)),
            pl.BlockSpec((2, 2 * HEADS, t), lambda b_, i: (0, 0, 0)),
        ],
        out_specs=[
            full, full, full, full,
            pl.BlockSpec((None, t, HEADS * CHUNK), lambda b_, i: (b_, i, 0)),
            pl.BlockSpec((None, t, LANES), lambda b_, i: (b_, i, 0)),
        ],
        out_shape=[
            jax.ShapeDtypeStruct((b, LP, D_MODEL), BF16),
            jax.ShapeDtypeStruct((b, LP, D_MODEL), BF16),
            jax.ShapeDtypeStruct((b, LP, D_MODEL), BF16),
            jax.ShapeDtypeStruct((b, LP, D_MODEL), BF16),
            jax.ShapeDtypeStruct((b, LP, HEADS * CHUNK), BF16),
            jax.ShapeDtypeStruct((b, LP, LANES), F32),
        ],
        compiler_params=pltpu.CompilerParams(
            dimension_semantics=("parallel", "parallel"),
            vmem_limit_bytes=40 * 1024 * 1024),
        name="dn_prep",
    )(proj3, proj3, proj3, proj3, proj3, proj3, conv_w2, _shift_matrix(), gcol3, grow, gp_row, gp_col)


SC_T = 4 * CHUNK


def _dn_scan_kernel(u_ref, w_ref, qd_ref, kd_ref, a_ref, cd_ref, z_ref, nw_ref, o_ref, s_sc):
    @pl.when(pl.program_id(1) == 0)
    def _():
        s_sc[...] = jnp.zeros(s_sc.shape, F32)

    nw = nw_ref[...]
    hs = range(HEADS)
    sls = [slice(h * HEAD_DIM, (h + 1) * HEAD_DIM) for h in hs]
    s = [s_sc[h] for h in hs]
    for cix in range(SC_T // CHUNK):
        rs = slice(cix * CHUNK, (cix + 1) * CHUNK)
        sb = [x.astype(BF16) for x in s]
        ws_qs = [_dot(jnp.concatenate([w_ref[rs, sls[h]], qd_ref[rs, sls[h]]], axis=0), sb[h]) for h in hs]
        vb = [(u_ref[rs, sls[h]].astype(F32) - ws_qs[h][:CHUNK]).astype(BF16) for h in hs]
        o = [ws_qs[h][CHUNK:] + _dot(a_ref[rs, h * CHUNK:(h + 1) * CHUNK], vb[h]) for h in hs]
        cd = [cd_ref[cix * CHUNK:cix * CHUNK + 1, HEADS + h:HEADS + h + 1] for h in hs]
        s = [s[h] * cd[h] + _dot_tn(kd_ref[rs, sls[h]], vb[h]) for h in hs]
        for h in hs:
            z = z_ref[rs, sls[h]].astype(F32)
            o_ref[rs, sls[h]] = (_rms(o[h], nw) * (z * jax.nn.sigmoid(z))).astype(BF16)
    for h in hs:
        s_sc[h] = s[h]


def _dn_scan(u, w, qd, kd, a, cd, proj3, dn_norm_w):
    b = u.shape[0]
    t = SC_T
    full = pl.BlockSpec((None, t, D_MODEL), lambda b_, i: (b_, i, 0))
    return pl.pallas_call(
        _dn_scan_kernel,
        grid=(b, LP // t),
        in_specs=[
            full, full, full, full,
            pl.BlockSpec((None, t, HEADS * CHUNK), lambda b_, i: (b_, i, 0)),
            pl.BlockSpec((None, t, LANES), lambda b_, i: (b_, i, 0)),
            pl.BlockSpec((None, t, D_MODEL), lambda b_, i: (b_, i, C_DZ)),
            pl.BlockSpec((1, HEAD_DIM), lambda b_, i: (0, 0)),
        ],
        out_specs=full,
        out_shape=jax.ShapeDtypeStruct((b, LP, D_MODEL), BF16),
        scratch_shapes=[pltpu.VMEM((HEADS, HEAD_DIM, HEAD_DIM), F32)],
        compiler_params=pltpu.CompilerParams(
            dimension_semantics=("parallel", "arbitrary"),
            vmem_limit_bytes=40 * 1024 * 1024),
        name="dn_scan",
    )(u, w, qd, kd, a, cd, proj3, dn_norm_w)


OUT_SUB = 2


def _merge_kernel(*refs):
    n = OUT_SUB
    xa_refs, xd_refs, ga_refs, gd_refs = refs[0:n], refs[n:2 * n], refs[2 * n:3 * n], refs[3 * n:4 * n]
    x_ref, wa_ref, wd_ref, wo_ref, nw_ref, o_ref = refs[4 * n:]

    def rows(rs):
        return jnp.concatenate([r[...] for r in rs], axis=0)

    ya = _dot(rows(xa_refs), wa_ref[...])
    yd = _dot(rows(xd_refs), wd_ref[...])
    merged = (jax.nn.sigmoid(rows(ga_refs).astype(F32)) * ya
              + jax.nn.sigmoid(rows(gd_refs).astype(F32)) * yd)
    out = x_ref[...] + _dot(merged.astype(BF16), wo_ref[...])
    o_ref[...] = _rms(out, nw_ref[...])


def _merge(xa, xd, proj3, x, wa, wd, wo, final_norm_w):
    b = x.shape[0]
    tm = OUT_SUB * X0

    def padded(cb):
        return [pl.BlockSpec((None, X0, D_MODEL), lambda b_, i, s=s: (b_, OUT_SUB * i + 1 + s, cb))
                for s in range(OUT_SUB)]

    tok = pl.BlockSpec((None, tm, D_MODEL), lambda b_, i: (b_, i, 0))
    wspec = pl.BlockSpec((D_MODEL, D_MODEL), lambda b_, i: (0, 0))
    return pl.pallas_call(
        _merge_kernel,
        grid=(b, SEQ // tm),
        in_specs=padded(0) + padded(0) + padded(C_GA) + padded(C_GD) + [
            tok, wspec, wspec, wspec,
            pl.BlockSpec((1, D_MODEL), lambda b_, i: (0, 0)),
        ],
        out_specs=tok,
        out_shape=jax.ShapeDtypeStruct((b, SEQ, D_MODEL), F32),
        compiler_params=pltpu.CompilerParams(
            dimension_semantics=("parallel", "parallel"),
            vmem_limit_bytes=48 * 1024 * 1024),
        name="merge_out",
    )(*([xa] * OUT_SUB + [xd] * OUT_SUB + [proj3] * (2 * OUT_SUB)), x, wa, wd, wo, final_norm_w)


def kernel(x, meta_tokens, norm_w, w_in, lambda_q1, lambda_k1, lambda_q2, lambda_k2, attn_norm_w,
           conv_w, a_log, dt_bias, dn_norm_w, w_branch_attn, w_branch_delta, w_out, final_norm_w):
    b = x.shape[0]
    assert x.shape == (b, SEQ, D_MODEL) and norm_w.shape[0] == 1
    layer = 0
    lam_init = 0.8 - 0.6 * math.exp(-0.3 * layer)

    head = jnp.concatenate([jnp.zeros((FRONT, D_MODEL), x.dtype), meta_tokens.astype(x.dtype)], axis=0)

    wi = w_in[layer]
    gate0 = 8 * D_MODEL
    q_scale = QK_DIM ** -0.5 * math.log2(math.e)
    col_scale = jnp.concatenate([jnp.full((1, D_MODEL), q_scale, F32), jnp.ones((1, gate0 - D_MODEL), F32)], axis=1)
    w_all = wi.astype(BF16)
    w_tail = wi[:, gate0 + 2 * HEADS:].astype(BF16)
    w_gate = wi[:, gate0:gate0 + 2 * HEADS]
    w_gate_p = jnp.pad(w_gate, ((0, 0), (0, LANES - 2 * HEADS))).astype(BF16)
    w_gate_t = w_gate.T.astype(BF16)
    zeros8 = jnp.zeros((HEADS,), F32)
    gp_row = jnp.stack([jnp.pad(jnp.concatenate([zeros8, a_log[layer]]), (0, LANES - 2 * HEADS)),
                        jnp.pad(jnp.concatenate([zeros8, dt_bias[layer]]), (0, LANES - 2 * HEADS))])
    gp_col = jnp.stack([jnp.concatenate([zeros8, a_log[layer]]),
                        jnp.concatenate([zeros8, dt_bias[layer]])])
    gp_col = jnp.broadcast_to(gp_col[:, :, None], (2, 2 * HEADS, DN_T)).astype(F32)
    lam_params = jnp.stack([lambda_q1[layer], lambda_k1[layer], lambda_q2[layer], lambda_k2[layer]])

    proj2, gcol, grow = _inproj(head, x, norm_w[layer][None], w_all, w_tail, col_scale, w_gate_p, w_gate_t)
    proj3 = proj2.reshape(b, LP, N_MAIN)

    xa = _attention(proj3, lam_params, attn_norm_w[layer][None], lam_init)

    u, w, qd, kd, a, cd = _dn_prep(proj3, conv_w[layer], gcol.reshape(b, LP, LANES), grow, gp_row, gp_col)
    xd = _dn_scan(u, w, qd, kd, a, cd, proj3, dn_norm_w[layer][None])

    return _merge(xa, xd, proj3, x, w_branch_attn[layer].astype(BF16), w_branch_delta[layer].astype(BF16),
                  w_out[layer].astype(BF16), final_norm_w[None])
```

```python
import functools
import math

import jax
import jax.numpy as jnp
from jax import lax
from jax.experimental import pallas as pl
from jax.experimental.pallas import tpu as pltpu

D_MODEL = 1024
SEQ = 8192
N_META = 16
NORM_EPS = 1e-6
HEADS = 8
HEAD_DIM = 128
QK_DIM = 64
CONV_K = 4
CHUNK = 64
LANES = 128
LP = 8448
FRONT = LP - SEQ - N_META
X0 = FRONT + N_META
N_MAIN = 10 * D_MODEL
NEG = -1e30

C_AQ, C_AK, C_AV, C_AZ, C_DQ, C_DK, C_DV, C_DZ, C_GA, C_GD = range(10)

F32 = jnp.float32
BF16 = jnp.bfloat16


def _rms(x, w):
    return x * lax.rsqrt(jnp.mean(x * x, axis=-1, keepdims=True) + NORM_EPS) * w


def _dot(a, b):
    return jnp.dot(a, b, preferred_element_type=F32)


def _dot_nt(a, b):
    return lax.dot_general(a, b, (((1,), (1,)), ((), ())), preferred_element_type=F32)


def _dot_tn(a, b):
    return lax.dot_general(a, b, (((0,), (0,)), ((), ())), preferred_element_type=F32)


IN_SUB = 11
IN_TM = IN_SUB * X0
IN_TN = 512
IN_TILES = LP // IN_TM
IN_FULL_TILES = 8 * D_MODEL // IN_TN


def _inproj_kernel(*refs):
    head_ref = refs[0]
    x_refs = refs[1:1 + IN_SUB]
    nw_ref, w_ref, wt_ref, cs_ref, wg_ref, wgt_ref, p_ref, gcol_ref, grow_ref, hn_sc = refs[1 + IN_SUB:]
    j = pl.program_id(1)

    @pl.when(j == 0)
    def _():
        first_tile = pl.program_id(0) % IN_TILES == 0
        nw = nw_ref[...]
        for s, x_ref in enumerate(x_refs):
            h = x_ref[...]
            if s == 0:
                h = jnp.where(first_tile, head_ref[...], h)
            hn_sc[s * X0:(s + 1) * X0, :] = _rms(h, nw).astype(BF16)
        hn = hn_sc[...]
        gcol_ref[...] = _dot(hn, wg_ref[...])
        grow_ref[...] = _dot_nt(wgt_ref[...], hn)

    @pl.when(j < IN_FULL_TILES)
    def _():
        p_ref[...] = (_dot(hn_sc[...], w_ref[...]) * cs_ref[...]).astype(BF16)

    @pl.when(j >= IN_FULL_TILES)
    def _():
        p_ref[...] = _dot(hn_sc[...], wt_ref[...]).astype(BF16)


def _inproj(head, x, norm_w, w_all, w_tail, col_scale, w_gate, w_gate_t):
    b = x.shape[0]
    rows = b * LP
    grid = (rows // IN_TM, N_MAIN // IN_TN)

    def xblock(s):
        return pl.BlockSpec((None, X0, D_MODEL),
                            lambda i, j: (i // IN_TILES, jnp.maximum((i % IN_TILES) * IN_SUB + s - 1, 0), 0))

    return pl.pallas_call(
        _inproj_kernel,
        grid=grid,
        in_specs=[pl.BlockSpec((X0, D_MODEL), lambda i, j: (0, 0), pipeline_mode=pl.Buffered(1))]
        + [xblock(s) for s in range(IN_SUB)] + [
            pl.BlockSpec((1, D_MODEL), lambda i, j: (0, 0)),
            pl.BlockSpec((D_MODEL, IN_TN), lambda i, j: (0, jnp.minimum(j, IN_FULL_TILES - 1))),
            pl.BlockSpec((D_MODEL, IN_TN), lambda i, j: (0, jnp.maximum(j - IN_FULL_TILES, 0))),
            pl.BlockSpec((1, IN_TN), lambda i, j: (0, jnp.minimum(j, IN_FULL_TILES - 1))),
            pl.BlockSpec((D_MODEL, LANES), lambda i, j: (0, 0)),
            pl.BlockSpec((2 * HEADS, D_MODEL), lambda i, j: (0, 0)),
        ],
        out_specs=[
            pl.BlockSpec((IN_TM, IN_TN), lambda i, j: (i, j)),
            pl.BlockSpec((IN_TM, LANES), lambda i, j: (i, 0)),
            pl.BlockSpec((2 * HEADS, IN_TM), lambda i, j: (0, i)),
        ],
        out_shape=[
            jax.ShapeDtypeStruct((rows, N_MAIN), BF16),
            jax.ShapeDtypeStruct((rows, LANES), F32),
            jax.ShapeDtypeStruct((2 * HEADS, rows), F32),
        ],
        scratch_shapes=[pltpu.VMEM((IN_TM, D_MODEL), BF16)],
        compiler_params=pltpu.CompilerParams(
            dimension_semantics=("parallel", "arbitrary"),
            vmem_limit_bytes=56 * 1024 * 1024),
        name="inproj",
    )(head, *([x] * IN_SUB), norm_w, w_all, w_tail, col_scale, w_gate, w_gate_t)


AT_SUB = 2
AT_TQ = AT_SUB * X0
AT_BIG = 4
AT_TILES = -(-LP // AT_TQ)
AT_LA = AT_TILES * AT_TQ
AT_SHIFT_MAX = 30.0


def _attn_kernel(*refs, lam_init):
    lam_ref, q_ref, k_ref, v_ref = refs[0:4]
    z_refs = refs[4:4 + AT_SUB]
    nw_ref = refs[4 + AT_SUB]
    o_ref = refs[5 + AT_SUB]
    qx_sc, kx_sc, vx_sc, e1_sc, e2_sc, flag_sc, m_sc, acc_sc = refs[6 + AT_SUB:]
    i = pl.program_id(2)
    tq = AT_TQ

    @pl.when((pl.program_id(0) == 0) & (pl.program_id(1) == 0) & (i == 0))
    def _():
        rowi = lax.broadcasted_iota(jnp.int32, (LP, LANES), 0)
        ln = lax.broadcasted_iota(jnp.int32, (LP, LANES), 1)
        ext = jnp.where(ln == 0, 1.0, jnp.where((ln == 1) & (rowi < FRONT), NEG, 0.0))
        kx_sc[0:LP, HEAD_DIM:] = ext.astype(BF16)
        vx_sc[0:LP, HEAD_DIM:] = jnp.ones((LP, HEAD_DIM), BF16)
        if AT_LA > LP:
            for ref in (qx_sc, kx_sc, vx_sc, e1_sc, e2_sc):
                ref[LP:, :] = jnp.zeros((AT_LA - LP, ref.shape[1]), BF16)

    @pl.when(i == 0)
    def _():
        k = k_ref[...]
        kx_sc[0:LP, 0:HEAD_DIM] = k
        vx_sc[0:LP, 0:HEAD_DIM] = v_ref[...]
        ln = lax.broadcasted_iota(jnp.int32, (LP, LANES), 1)
        first = ln < QK_DIM
        pad_lane = jnp.where(ln == 1, 1.0, 0.0)

        def half_norms(x):
            sq = x.astype(F32)
            sq = sq * sq
            return (jnp.sqrt(jnp.sum(jnp.where(first, sq, 0.0), axis=-1, keepdims=True)),
                    jnp.sqrt(jnp.sum(jnp.where(first, 0.0, sq), axis=-1, keepdims=True)))

        q = q_ref[...]
        qx_sc[0:LP, :] = q
        kn1, kn2 = half_norms(k)
        qn1, qn2 = half_norms(q)
        shift1 = qn1 * jnp.max(kn1, axis=0, keepdims=True)
        shift2 = qn2 * jnp.max(kn2, axis=0, keepdims=True)
        e1_sc[0:LP, :] = (pad_lane - jnp.where(ln == 0, shift1, 0.0)).astype(BF16)
        e2_sc[0:LP, :] = (pad_lane - jnp.where(ln == 0, shift2, 0.0)).astype(BF16)
        ok = jnp.maximum(jnp.max(shift1), jnp.max(shift2)) <= AT_SHIFT_MAX
        flag_sc[0] = ok.astype(jnp.int32)

    row0 = pl.multiple_of(i * tq, tq)
    q = qx_sc[pl.ds(row0, tq), :]
    first = lax.broadcasted_iota(jnp.int32, (tq, LANES), 1) < QK_DIM
    zero = jnp.zeros_like(q)
    use_shift = flag_sc[0] == 1
    q2 = jnp.concatenate([jnp.where(first, q, zero), jnp.where(first, zero, q)], axis=0)

    acc_sc[...] = jnp.zeros(acc_sc.shape, F32)

    def causal(s):
        r2 = lax.broadcasted_iota(jnp.int32, (2 * tq, tq), 0)
        row = jnp.where(r2 >= tq, r2 - tq, r2)
        col = lax.broadcasted_iota(jnp.int32, (2 * tq, tq), 1)
        return jnp.where(col <= row, s, NEG)

    def scores(qx, start, units):
        return _dot_nt(qx, kx_sc[pl.ds(start, units * tq), :])

    @pl.when(use_shift)
    def _():
        bias = jnp.concatenate([e1_sc[pl.ds(row0, tq), :], e2_sc[pl.ds(row0, tq), :]], axis=0)
        qx = jnp.concatenate([q2, bias], axis=1)

        def consume(start, units, s):
            p = jnp.exp2(s).astype(BF16)
            acc_sc[...] += _dot(p, vx_sc[pl.ds(start, units * tq), :])

        n_big = i // AT_BIG

        def big(j, c):
            start = pl.multiple_of(j * (AT_BIG * tq), AT_BIG * tq)
            consume(start, AT_BIG, scores(qx, start, AT_BIG))
            return c

        lax.fori_loop(0, n_big, big, 0)
        tail_units = i - n_big * AT_BIG + 1
        tail_start = pl.multiple_of(n_big * (AT_BIG * tq), AT_BIG * tq)
        for units in range(1, AT_BIG + 1):
            @pl.when(tail_units == units)
            def _():
                s = scores(qx, tail_start, units)
                s_diag = causal(s[:, (units - 1) * tq:])
                if units > 1:
                    s_diag = jnp.concatenate([s[:, :(units - 1) * tq], s_diag], axis=1)
                consume(tail_start, units, s_diag)

    @pl.when(jnp.logical_not(use_shift))
    def _():
        lane2 = lax.broadcasted_iota(jnp.int32, (2 * tq, LANES), 1)
        qx = jnp.concatenate([q2, jnp.where(lane2 == 1, 1.0, 0.0).astype(BF16)], axis=1)
        m_sc[...] = jnp.full(m_sc.shape, NEG, F32)

        def block(j, diag):
            start = pl.multiple_of(j * tq, tq)
            s = scores(qx, start, 1)
            if diag:
                s = causal(s)
            chunks = [s[:, c * LANES:(c + 1) * LANES] for c in range(tq // LANES)]
            mc = chunks[0]
            for ch in chunks[1:]:
                mc = jnp.maximum(mc, ch)
            m_prev = m_sc[...]
            m_new = jnp.maximum(m_prev, jnp.max(mc, axis=-1, keepdims=True))
            alpha = jnp.exp2(m_prev - m_new)
            p = jnp.concatenate([jnp.exp2(ch - m_new) for ch in chunks], axis=1).astype(BF16)
            pv = _dot(p, vx_sc[pl.ds(start, tq), :])
            acc_sc[...] = jnp.concatenate([alpha, alpha], axis=1) * acc_sc[...] + pv
            m_sc[...] = m_new

        def full(j, c):
            block(j, False)
            return c

        lax.fori_loop(0, i, full, 0)
        block(i, True)

    lp = lam_ref[...]
    lam = (jnp.exp(jnp.sum(lp[0:1] * lp[1:2], axis=-1, keepdims=True))
           - jnp.exp(jnp.sum(lp[2:3] * lp[3:4], axis=-1, keepdims=True)) + lam_init)
    acc = acc_sc[...]
    o_all = acc[:, :HEAD_DIM] / jnp.maximum(acc[:, HEAD_DIM:], 1e-37)
    o = o_all[:tq] - lam * o_all[tq:]
    o = _rms(o, nw_ref[...]) * (1.0 - lam_init)
    z = jnp.concatenate([r[...] for r in z_refs], axis=0).astype(F32)
    o_ref[...] = (o * (z * jax.nn.sigmoid(z))).astype(BF16)


def _attention(proj3, lam_params, attn_norm_w, lam_init):
    b = proj3.shape[0]
    hb = D_MODEL // HEAD_DIM
    nblk = LP // X0
    kern = functools.partial(_attn_kernel, lam_init=lam_init)

    def sub(s, col):
        return pl.BlockSpec((None, X0, HEAD_DIM),
                            lambda b_, h, i: (b_, jnp.minimum(AT_SUB * i + s, nblk - 1), col * hb + h))

    return pl.pallas_call(
        kern,
        grid=(b, HEADS, AT_TILES),
        in_specs=[
            pl.BlockSpec((4, QK_DIM), lambda b_, h, i: (0, 0)),
            pl.BlockSpec((None, LP, HEAD_DIM), lambda b_, h, i: (b_, 0, C_AQ * hb + h)),
            pl.BlockSpec((None, LP, HEAD_DIM), lambda b_, h, i: (b_, 0, C_AK * hb + h)),
            pl.BlockSpec((None, LP, HEAD_DIM), lambda b_, h, i: (b_, 0, C_AV * hb + h)),
        ] + [sub(s, C_AZ) for s in range(AT_SUB)] + [
            pl.BlockSpec((1, HEAD_DIM), lambda b_, h, i: (0, 0)),
        ],
        out_specs=pl.BlockSpec((None, AT_TQ, HEAD_DIM), lambda b_, h, i: (b_, i, h)),
        out_shape=jax.ShapeDtypeStruct((b, LP, D_MODEL), BF16),
        scratch_shapes=[
            pltpu.VMEM((AT_LA, HEAD_DIM), BF16),
            pltpu.VMEM((AT_LA, 2 * HEAD_DIM), BF16),
            pltpu.VMEM((AT_LA, 2 * HEAD_DIM), BF16),
            pltpu.VMEM((AT_LA, LANES), BF16),
            pltpu.VMEM((AT_LA, LANES), BF16),
            pltpu.SMEM((1,), jnp.int32),
            pltpu.VMEM((2 * AT_TQ, LANES), F32),
            pltpu.VMEM((2 * AT_TQ, 2 * HEAD_DIM), F32),
        ],
        compiler_params=pltpu.CompilerParams(
            dimension_semantics=("arbitrary", "arbitrary", "arbitrary"),
            vmem_limit_bytes=56 * 1024 * 1024),
        name="diff_attn",
    )(lam_params, proj3, proj3, proj3, *([proj3] * AT_SUB), attn_norm_w)


DN_T = 2 * CHUNK
HALO = 16


def _shift_matrix():
    r = jnp.arange((CONV_K - 1) * DN_T)
    src = HALO + r % DN_T - (r // DN_T + 1)
    return (src[:, None] == jnp.arange(HALO + DN_T)[None, :]).astype(BF16)


def _split3(x):
    hi = x.astype(BF16)
    r1 = x - hi.astype(F32)
    mid = r1.astype(BF16)
    lo = (r1 - mid.astype(F32)).astype(BF16)
    return hi, mid, lo


def _dn_prep_kernel(dq_ref, dk_ref, dv_ref, hq_ref, hk_ref, hv_ref, cw_ref, sh_ref, gcol_ref, grow_ref,
                    gpr_ref, gpc_ref, u_ref, w_ref, qd_ref, kd_ref, a_ref, cd_ref):
    i = pl.program_id(1)
    t = DN_T
    width = D_MODEL
    cw = cw_ref[...]
    shifter = sh_ref[...]
    ys = []
    for idx, (t_ref, h_ref) in enumerate(((dq_ref, hq_ref), (dk_ref, hk_ref), (dv_ref, hv_ref))):
        halo = h_ref[...]
        halo = jnp.where(i == 0, jnp.zeros_like(halo), halo)
        x = t_ref[...]
        shifted = _dot(shifter, jnp.concatenate([halo, x], axis=0))
        cwi = cw[:, idx * width:(idx + 1) * width]
        yi = cwi[CONV_K - 1:CONV_K] * x.astype(F32)
        for s in range(1, CONV_K):
            yi = yi + cwi[CONV_K - 1 - s:CONV_K - s] * shifted[(s - 1) * t:s * t]
        ys.append(yi * jax.nn.sigmoid(yi))
    y = jnp.concatenate(ys, axis=1)

    r = lax.broadcasted_iota(jnp.int32, (t, t), 0)
    c = lax.broadcasted_iota(jnp.int32, (t, t), 1)
    same = (r >= CHUNK) == (c >= CHUNK)
    incl = same & (c <= r)
    strict = same & (c < r)
    tri = jnp.where(incl, 1.0, 0.0).astype(BF16)
    tri_t = jnp.where(same & (r <= c), 1.0, 0.0).astype(BF16)
    blk = jnp.where(same, 1.0, 0.0).astype(BF16)
    eye = jnp.where(r == c, 1.0, 0.0).astype(F32)

    lane = lax.broadcasted_iota(jnp.int32, (t, LANES), 1)
    rowi = lax.broadcasted_iota(jnp.int32, (t, LANES), 0) + i * t
    gcol = gcol_ref[...]
    gpr = gpr_ref[...]
    is_g = (lane >= HEADS) & (lane < 2 * HEADS)
    live_c = rowi >= FRONT
    beta_c = jnp.where(live_c, jax.nn.sigmoid(gcol), 0.0)
    xg = gcol + gpr[1:2]
    sp = jnp.maximum(xg, 0.0) + jnp.log1p(jnp.exp(-jnp.abs(xg)))
    g_c = jnp.where(is_g & live_c, -jnp.exp(gpr[0:1]) * sp, 0.0)
    g3 = _split3(g_c)
    gc_c = _dot(tri, g3[0]) + _dot(tri, g3[1]) + _dot(tri, g3[2])
    gl_c = _dot(blk, g3[0]) + _dot(blk, g3[1]) + _dot(blk, g3[2])
    sub = lax.broadcasted_iota(jnp.int32, (2 * HEADS, t), 0)
    coli = lax.broadcasted_iota(jnp.int32, (2 * HEADS, t), 1) + i * t
    xr = grow_ref[...] + gpc_ref[1]
    spr = jnp.maximum(xr, 0.0) + jnp.log1p(jnp.exp(-jnp.abs(xr)))
    g_r = jnp.where((sub >= HEADS) & (coli >= FRONT), -jnp.exp(gpc_ref[0]) * spr, 0.0)
    gr3 = _split3(g_r)
    gc_r = _dot(gr3[0], tri_t) + _dot(gr3[1], tri_t) + _dot(gr3[2], tri_t)

    cd_ref[...] = jnp.exp(gl_c)

    hs = range(HEADS)
    sls = [slice(h * HEAD_DIM, (h + 1) * HEAD_DIM) for h in hs]
    q = [y[:, h * HEAD_DIM:(h + 1) * HEAD_DIM] for h in hs]
    k = [y[:, width + h * HEAD_DIM:width + (h + 1) * HEAD_DIM] for h in hs]
    v = [y[:, 2 * width + h * HEAD_DIM:2 * width + (h + 1) * HEAD_DIM] for h in hs]
    q = [x * lax.rsqrt(jnp.sum(x * x, axis=-1, keepdims=True) + 1e-6) * (HEAD_DIM ** -0.5) for x in q]
    k = [x * lax.rsqrt(jnp.sum(x * x, axis=-1, keepdims=True) + 1e-6) for x in k]
    beta = [beta_c[:, h:h + 1] for h in hs]
    gcc = [gc_c[:, HEADS + h:HEADS + h + 1] for h in hs]
    glc = [gl_c[:, HEADS + h:HEADS + h + 1] for h in hs]
    gcr = [gc_r[HEADS + h:HEADS + h + 1, :] for h in hs]
    dec = [jnp.where(incl, jnp.exp(jnp.where(incl, gcc[h] - gcr[h], 0.0)), 0.0) for h in hs]
    kb = [k[h] * beta[h] for h in hs]
    kt = [x.T.astype(BF16) for x in k]
    m = [jnp.where(strict, _dot(kb[h].astype(BF16), kt[h]) * dec[h], 0.0) for h in hs]
    tinv = [eye - x for x in m]
    pwb = [x.astype(BF16) for x in m]
    for _ in range(5):
        pw = [_dot(x, x) for x in pwb]
        pwb = [x.astype(BF16) for x in pw]
        tinv = [tinv[h] + _dot(tinv[h].astype(BF16), pwb[h]) for h in hs]
    egc = [jnp.exp(x) for x in gcc]
    rhs = [jnp.concatenate([v[h] * beta[h], kb[h] * egc[h]], axis=1).astype(BF16) for h in hs]
    uw = [_dot(tinv[h].astype(BF16), rhs[h]) for h in hs]
    a_full = [_dot(q[h].astype(BF16), kt[h]) * dec[h] for h in hs]
    for h in hs:
        u_ref[:, sls[h]] = uw[h][:, :HEAD_DIM].astype(BF16)
        w_ref[:, sls[h]] = uw[h][:, HEAD_DIM:].astype(BF16)
        a_cmp = a_full[h] + pltpu.roll(a_full[h], CHUNK, axis=1)
        a_ref[:, h * CHUNK:(h + 1) * CHUNK] = a_cmp[:, :CHUNK].astype(BF16)
        qd_ref[:, sls[h]] = (q[h] * egc[h]).astype(BF16)
        kd_ref[:, sls[h]] = (k[h] * jnp.exp(glc[h] - gcc[h])).astype(BF16)


def _dn_prep(proj3, conv_w2, gcol3, grow, gp_row, gp_col):
    b = proj3.shape[0]
    t = DN_T
    nt = LP // t
    hpb = t // HALO

    def tile(cb):
        return pl.BlockSpec((None, t, D_MODEL), lambda b_, i: (b_, i, cb))

    def halo(cb):
        return pl.BlockSpec((None, HALO, D_MODEL), lambda b_, i: (b_, jnp.maximum(i * hpb - 1, 0), cb))

    full = pl.BlockSpec((None, t, D_MODEL), lambda b_, i: (b_, i, 0))
    return pl.pallas_call(
        _dn_prep_kernel,
        grid=(b, nt),
        in_specs=[
            tile(C_DQ), tile(C_DK), tile(C_DV), halo(C_DQ), halo(C_DK), halo(C_DV),
            pl.BlockSpec((CONV_K, 3 * D_MODEL), lambda b_, i: (0, 0)),
            pl.BlockSpec(((CONV_K - 1) * t, HALO + t), lambda b_, i: (0, 0)),
            pl.BlockSpec((None, t, LANES), lambda b_, i: (b_, i, 0)),
            pl.BlockSpec((2 * HEADS, t), lambda b_, i: (0, b_ * nt + i)),
            pl.BlockSpec((2, LANES), lambda b_, i: (0, 0)),
            pl.BlockSpec((2, 2 * HEADS, t), lambda b_, i: (0, 0, 0)),
        ],
        out_specs=[
            full, full, full, full,
            pl.BlockSpec((None, t, HEADS * CHUNK), lambda b_, i: (b_, i, 0)),
            pl.BlockSpec((None, t, LANES), lambda b_, i: (b_, i, 0)),
        ],
        out_shape=[
            jax.ShapeDtypeStruct((b, LP, D_MODEL), BF16),
            jax.ShapeDtypeStruct((b, LP, D_MODEL), BF16),
            jax.ShapeDtypeStruct((b, LP, D_MODEL), BF16),
            jax.ShapeDtypeStruct((b, LP, D_MODEL), BF16),
            jax.ShapeDtypeStruct((b, LP, HEADS * CHUNK), BF16),
            jax.ShapeDtypeStruct((b, LP, LANES), F32),
        ],
        compiler_params=pltpu.CompilerParams(
            dimension_semantics=("parallel", "parallel"),
            vmem_limit_bytes=40 * 1024 * 1024),
        name="dn_prep",
    )(proj3, proj3, proj3, proj3, proj3, proj3, conv_w2, _shift_matrix(), gcol3, grow, gp_row, gp_col)


SC_T = 4 * CHUNK


def _dn_scan_kernel(u_ref, w_ref, qd_ref, kd_ref, a_ref, cd_ref, z_ref, nw_ref, o_ref, s_sc):
    @pl.when(pl.program_id(1) == 0)
    def _():
        s_sc[...] = jnp.zeros(s_sc.shape, F32)

    nw = nw_ref[...]
    hs = range(HEADS)
    sls = [slice(h * HEAD_DIM, (h + 1) * HEAD_DIM) for h in hs]
    s = [s_sc[h] for h in hs]
    for cix in range(SC_T // CHUNK):
        rs = slice(cix * CHUNK, (cix + 1) * CHUNK)
        sb = [x.astype(BF16) for x in s]
        ws_qs = [_dot(jnp.concatenate([w_ref[rs, sls[h]], qd_ref[rs, sls[h]]], axis=0), sb[h]) for h in hs]
        vb = [(u_ref[rs, sls[h]].astype(F32) - ws_qs[h][:CHUNK]).astype(BF16) for h in hs]
        o = [ws_qs[h][CHUNK:] + _dot(a_ref[rs, h * CHUNK:(h + 1) * CHUNK], vb[h]) for h in hs]
        cd = [cd_ref[cix * CHUNK:cix * CHUNK + 1, HEADS + h:HEADS + h + 1] for h in hs]
        s = [s[h] * cd[h] + _dot_tn(kd_ref[rs, sls[h]], vb[h]) for h in hs]
        for h in hs:
            z = z_ref[rs, sls[h]].astype(F32)
            o_ref[rs, sls[h]] = (_rms(o[h], nw) * (z * jax.nn.sigmoid(z))).astype(BF16)
    for h in hs:
        s_sc[h] = s[h]


def _dn_scan(u, w, qd, kd, a, cd, proj3, dn_norm_w):
    b = u.shape[0]
    t = SC_T
    full = pl.BlockSpec((None, t, D_MODEL), lambda b_, i: (b_, i, 0))
    return pl.pallas_call(
        _dn_scan_kernel,
        grid=(b, LP // t),
        in_specs=[
            full, full, full, full,
            pl.BlockSpec((None, t, HEADS * CHUNK), lambda b_, i: (b_, i, 0)),
            pl.BlockSpec((None, t, LANES), lambda b_, i: (b_, i, 0)),
            pl.BlockSpec((None, t, D_MODEL), lambda b_, i: (b_, i, C_DZ)),
            pl.BlockSpec((1, HEAD_DIM), lambda b_, i: (0, 0)),
        ],
        out_specs=full,
        out_shape=jax.ShapeDtypeStruct((b, LP, D_MODEL), BF16),
        scratch_shapes=[pltpu.VMEM((HEADS, HEAD_DIM, HEAD_DIM), F32)],
        compiler_params=pltpu.CompilerParams(
            dimension_semantics=("parallel", "arbitrary"),
            vmem_limit_bytes=40 * 1024 * 1024),
        name="dn_scan",
    )(u, w, qd, kd, a, cd, proj3, dn_norm_w)


OUT_SUB = 2


def _merge_kernel(*refs):
    n = OUT_SUB
    xa_refs, xd_refs, ga_refs, gd_refs = refs[0:n], refs[n:2 * n], refs[2 * n:3 * n], refs[3 * n:4 * n]
    x_ref, wa_ref, wd_ref, wo_ref, nw_ref, o_ref = refs[4 * n:]

    def rows(rs):
        return jnp.concatenate([r[...] for r in rs], axis=0)

    ya = _dot(rows(xa_refs), wa_ref[...])
    yd = _dot(rows(xd_refs), wd_ref[...])
    merged = (jax.nn.sigmoid(rows(ga_refs).astype(F32)) * ya
              + jax.nn.sigmoid(rows(gd_refs).astype(F32)) * yd)
    out = x_ref[...] + _dot(merged.astype(BF16), wo_ref[...])
    o_ref[...] = _rms(out, nw_ref[...])


def _merge(xa, xd, proj3, x, wa, wd, wo, final_norm_w):
    b = x.shape[0]
    tm = OUT_SUB * X0

    def padded(cb):
        return [pl.BlockSpec((None, X0, D_MODEL), lambda b_, i, s=s: (b_, OUT_SUB * i + 1 + s, cb))
                for s in range(OUT_SUB)]

    tok = pl.BlockSpec((None, tm, D_MODEL), lambda b_, i: (b_, i, 0))
    wspec = pl.BlockSpec((D_MODEL, D_MODEL), lambda b_, i: (0, 0))
    return pl.pallas_call(
        _merge_kernel,
        grid=(b, SEQ // tm),
        in_specs=padded(0) + padded(0) + padded(C_GA) + padded(C_GD) + [
            tok, wspec, wspec, wspec,
            pl.BlockSpec((1, D_MODEL), lambda b_, i: (0, 0)),
        ],
        out_specs=tok,
        out_shape=jax.ShapeDtypeStruct((b, SEQ, D_MODEL), F32),
        compiler_params=pltpu.CompilerParams(
            dimension_semantics=("parallel", "parallel"),
            vmem_limit_bytes=48 * 1024 * 1024),
        name="merge_out",
    )(*([xa] * OUT_SUB + [xd] * OUT_SUB + [proj3] * (2 * OUT_SUB)), x, wa, wd, wo, final_norm_w)


def kernel(x, meta_tokens, norm_w, w_in, lambda_q1, lambda_k1, lambda_q2, lambda_k2, attn_norm_w,
           conv_w, a_log, dt_bias, dn_norm_w, w_branch_attn, w_branch_delta, w_out, final_norm_w):
    b = x.shape[0]
    assert x.shape == (b, SEQ, D_MODEL) and norm_w.shape[0] == 1
    layer = 0
    lam_init = 0.8 - 0.6 * math.exp(-0.3 * layer)

    head = jnp.concatenate([jnp.zeros((FRONT, D_MODEL), x.dtype), meta_tokens.astype(x.dtype)], axis=0)

    wi = w_in[layer]
    gate0 = 8 * D_MODEL
    q_scale = QK_DIM ** -0.5 * math.log2(math.e)
    col_scale = jnp.concatenate([jnp.full((1, D_MODEL), q_scale, F32), jnp.ones((1, gate0 - D_MODEL), F32)], axis=1)
    w_all = wi.astype(BF16)
    w_tail = wi[:, gate0 + 2 * HEADS:].astype(BF16)
    w_gate = wi[:, gate0:gate0 + 2 * HEADS]
    w_gate_p = jnp.pad(w_gate, ((0, 0), (0, LANES - 2 * HEADS))).astype(BF16)
    w_gate_t = w_gate.T.astype(BF16)
    zeros8 = jnp.zeros((HEADS,), F32)
    gp_row = jnp.stack([jnp.pad(jnp.concatenate([zeros8, a_log[layer]]), (0, LANES - 2 * HEADS)),
                        jnp.pad(jnp.concatenate([zeros8, dt_bias[layer]]), (0, LANES - 2 * HEADS))])
    gp_col = jnp.stack([jnp.concatenate([zeros8, a_log[layer]]),
                        jnp.concatenate([zeros8, dt_bias[layer]])])
    gp_col = jnp.broadcast_to(gp_col[:, :, None], (2, 2 * HEADS, DN_T)).astype(F32)
    lam_params = jnp.stack([lambda_q1[layer], lambda_k1[layer], lambda_q2[layer], lambda_k2[layer]])

    proj2, gcol, grow = _inproj(head, x, norm_w[layer][None], w_all, w_tail, col_scale, w_gate_p, w_gate_t)
    proj3 = proj2.reshape(b, LP, N_MAIN)

    xa = _attention(proj3, lam_params, attn_norm_w[layer][None], lam_init)

    u, w, qd, kd, a, cd = _dn_prep(proj3, conv_w[layer], gcol.reshape(b, LP, LANES), grow, gp_row, gp_col)
    xd = _dn_scan(u, w, qd, kd, a, cd, proj3, dn_norm_w[layer][None])

    return _merge(xa, xd, proj3, x, w_branch_attn[layer].astype(BF16), w_branch_delta[layer].astype(BF16),
                  w_out[layer].astype(BF16), final_norm_w[None])
```

```python
import functools
import math

import jax
import jax.numpy as jnp
from jax import lax
from jax.experimental import pallas as pl
from jax.experimental.pallas import tpu as pltpu

D_MODEL = 1024
SEQ = 8192
N_META = 16
NORM_EPS = 1e-6
HEADS = 8
HEAD_DIM = 128
QK_DIM = 64
CONV_K = 4
CHUNK = 64
LANES = 128
LP = 8448
FRONT = LP - SEQ - N_META
X0 = FRONT + N_META
N_MAIN = 10 * D_MODEL
NEG = -1e30

C_AQ, C_AK, C_AV, C_AZ, C_DQ, C_DK, C_DV, C_DZ, C_GA, C_GD = range(10)

F32 = jnp.float32
BF16 = jnp.bfloat16


def _rms(x, w):
    return x * lax.rsqrt(jnp.mean(x * x, axis=-1, keepdims=True) + NORM_EPS) * w


def _dot(a, b):
    return jnp.dot(a, b, preferred_element_type=F32)


def _dot_nt(a, b):
    return lax.dot_general(a, b, (((1,), (1,)), ((), ())), preferred_element_type=F32)


def _dot_tn(a, b):
    return lax.dot_general(a, b, (((0,), (0,)), ((), ())), preferred_element_type=F32)


IN_SUB = 3
IN_TM = IN_SUB * X0
IN_TN = 2048
IN_TILES = LP // IN_TM
IN_FULL_TILES = 8 * D_MODEL // IN_TN


def _inproj_kernel(*refs):
    head_ref = refs[0]
    x_refs = refs[1:1 + IN_SUB]
    nw_ref, w_ref, wt_ref, cs_ref, wg_ref, wgt_ref, p_ref, gcol_ref, grow_ref, hn_sc = refs[1 + IN_SUB:]
    j = pl.program_id(1)

    @pl.when(j == 0)
    def _():
        first_tile = pl.program_id(0) % IN_TILES == 0
        nw = nw_ref[...]
        for s, x_ref in enumerate(x_refs):
            h = x_ref[...]
            if s == 0:
                h = jnp.where(first_tile, head_ref[...], h)
            hn_sc[s * X0:(s + 1) * X0, :] = _rms(h, nw).astype(BF16)
        hn = hn_sc[...]
        gcol_ref[...] = _dot(hn, wg_ref[...])
        grow_ref[...] = _dot_nt(wgt_ref[...], hn)

    @pl.when(j < IN_FULL_TILES)
    def _():
        p_ref[...] = (_dot(hn_sc[...], w_ref[...]) * cs_ref[...]).astype(BF16)

    @pl.when(j >= IN_FULL_TILES)
    def _():
        p_ref[...] = _dot(hn_sc[...], wt_ref[...]).astype(BF16)


def _inproj(head, x, norm_w, w_all, w_tail, col_scale, w_gate, w_gate_t):
    b = x.shape[0]
    rows = b * LP
    grid = (rows // IN_TM, N_MAIN // IN_TN)

    def xblock(s):
        return pl.BlockSpec((None, X0, D_MODEL),
                            lambda i, j: (i // IN_TILES, jnp.maximum((i % IN_TILES) * IN_SUB + s - 1, 0), 0))

    return pl.pallas_call(
        _inproj_kernel,
        grid=grid,
        in_specs=[pl.BlockSpec((X0, D_MODEL), lambda i, j: (0, 0), pipeline_mode=pl.Buffered(1))]
        + [xblock(s) for s in range(IN_SUB)] + [
            pl.BlockSpec((1, D_MODEL), lambda i, j: (0, 0)),
            pl.BlockSpec((D_MODEL, IN_TN), lambda i, j: (0, jnp.minimum(j, IN_FULL_TILES - 1))),
            pl.BlockSpec((D_MODEL, IN_TN), lambda i, j: (0, jnp.maximum(j - IN_FULL_TILES, 0))),
            pl.BlockSpec((1, IN_TN), lambda i, j: (0, jnp.minimum(j, IN_FULL_TILES - 1))),
            pl.BlockSpec((D_MODEL, LANES), lambda i, j: (0, 0)),
            pl.BlockSpec((2 * HEADS, D_MODEL), lambda i, j: (0, 0)),
        ],
        out_specs=[
            pl.BlockSpec((IN_TM, IN_TN), lambda i, j: (i, j)),
            pl.BlockSpec((IN_TM, LANES), lambda i, j: (i, 0)),
            pl.BlockSpec((2 * HEADS, IN_TM), lambda i, j: (0, i)),
        ],
        out_shape=[
            jax.ShapeDtypeStruct((rows, N_MAIN), BF16),
            jax.ShapeDtypeStruct((rows, LANES), F32),
            jax.ShapeDtypeStruct((2 * HEADS, rows), F32),
        ],
        scratch_shapes=[pltpu.VMEM((IN_TM, D_MODEL), BF16)],
        compiler_params=pltpu.CompilerParams(
            dimension_semantics=("parallel", "arbitrary"),
            vmem_limit_bytes=56 * 1024 * 1024),
        name="inproj",
    )(head, *([x] * IN_SUB), norm_w, w_all, w_tail, col_scale, w_gate, w_gate_t)


AT_SUB = 2
AT_TQ = AT_SUB * X0
AT_BIG = 6
AT_TILES = -(-LP // AT_TQ)
AT_LA = AT_TILES * AT_TQ
AT_SHIFT_MAX = 30.0


def _attn_kernel(*refs, lam_init):
    lam_ref, q_ref, k_ref, v_ref = refs[0:4]
    z_refs = refs[4:4 + AT_SUB]
    nw_ref = refs[4 + AT_SUB]
    o_ref = refs[5 + AT_SUB]
    qx_sc, kx_sc, vx_sc, e1_sc, e2_sc, flag_sc, m_sc, acc_sc = refs[6 + AT_SUB:]
    i = pl.program_id(2)
    tq = AT_TQ

    @pl.when((pl.program_id(0) == 0) & (pl.program_id(1) == 0) & (i == 0))
    def _():
        rowi = lax.broadcasted_iota(jnp.int32, (LP, LANES), 0)
        ln = lax.broadcasted_iota(jnp.int32, (LP, LANES), 1)
        ext = jnp.where(ln == 0, 1.0, jnp.where((ln == 1) & (rowi < FRONT), NEG, 0.0))
        kx_sc[0:LP, HEAD_DIM:] = ext.astype(BF16)
        vx_sc[0:LP, HEAD_DIM:] = jnp.ones((LP, HEAD_DIM), BF16)
        if AT_LA > LP:
            for ref in (qx_sc, kx_sc, vx_sc, e1_sc, e2_sc):
                ref[LP:, :] = jnp.zeros((AT_LA - LP, ref.shape[1]), BF16)

    @pl.when(i == 0)
    def _():
        k = k_ref[...]
        kx_sc[0:LP, 0:HEAD_DIM] = k
        vx_sc[0:LP, 0:HEAD_DIM] = v_ref[...]
        ln = lax.broadcasted_iota(jnp.int32, (LP, LANES), 1)
        first = ln < QK_DIM
        pad_lane = jnp.where(ln == 1, 1.0, 0.0)

        def half_norms(x):
            sq = x.astype(F32)
            sq = sq * sq
            return (jnp.sqrt(jnp.sum(jnp.where(first, sq, 0.0), axis=-1, keepdims=True)),
                    jnp.sqrt(jnp.sum(jnp.where(first, 0.0, sq), axis=-1, keepdims=True)))

        q = q_ref[...]
        qx_sc[0:LP, :] = q
        kn1, kn2 = half_norms(k)
        qn1, qn2 = half_norms(q)
        shift1 = qn1 * jnp.max(kn1, axis=0, keepdims=True)
        shift2 = qn2 * jnp.max(kn2, axis=0, keepdims=True)
        e1_sc[0:LP, :] = (pad_lane - jnp.where(ln == 0, shift1, 0.0)).astype(BF16)
        e2_sc[0:LP, :] = (pad_lane - jnp.where(ln == 0, shift2, 0.0)).astype(BF16)
        ok = jnp.maximum(jnp.max(shift1), jnp.max(shift2)) <= AT_SHIFT_MAX
        flag_sc[0] = ok.astype(jnp.int32)

    row0 = pl.multiple_of(i * tq, tq)
    q = qx_sc[pl.ds(row0, tq), :]
    first = lax.broadcasted_iota(jnp.int32, (tq, LANES), 1) < QK_DIM
    zero = jnp.zeros_like(q)
    use_shift = flag_sc[0] == 1
    q2 = jnp.concatenate([jnp.where(first, q, zero), jnp.where(first, zero, q)], axis=0)

    acc_sc[...] = jnp.zeros(acc_sc.shape, F32)

    def causal(s):
        r2 = lax.broadcasted_iota(jnp.int32, (2 * tq, tq), 0)
        row = jnp.where(r2 >= tq, r2 - tq, r2)
        col = lax.broadcasted_iota(jnp.int32, (2 * tq, tq), 1)
        return jnp.where(col <= row, s, NEG)

    def scores(qx, start, units):
        return _dot_nt(qx, kx_sc[pl.ds(start, units * tq), :])

    @pl.when(use_shift)
    def _():
        bias = jnp.concatenate([e1_sc[pl.ds(row0, tq), :], e2_sc[pl.ds(row0, tq), :]], axis=0)
        qx = jnp.concatenate([q2, bias], axis=1)

        def consume(start, units, s):
            p = jnp.exp2(s).astype(BF16)
            acc_sc[...] += _dot(p, vx_sc[pl.ds(start, units * tq), :])

        n_big = i // AT_BIG

        def big(j, c):
            start = pl.multiple_of(j * (AT_BIG * tq), AT_BIG * tq)
            consume(start, AT_BIG, scores(qx, start, AT_BIG))
            return c

        lax.fori_loop(0, n_big, big, 0)
        tail_units = i - n_big * AT_BIG + 1
        tail_start = pl.multiple_of(n_big * (AT_BIG * tq), AT_BIG * tq)
        for units in range(1, AT_BIG + 1):
            @pl.when(tail_units == units)
            def _():
                s = scores(qx, tail_start, units)
                s_diag = causal(s[:, (units - 1) * tq:])
                if units > 1:
                    s_diag = jnp.concatenate([s[:, :(units - 1) * tq], s_diag], axis=1)
                consume(tail_start, units, s_diag)

    @pl.when(jnp.logical_not(use_shift))
    def _():
        lane2 = lax.broadcasted_iota(jnp.int32, (2 * tq, LANES), 1)
        qx = jnp.concatenate([q2, jnp.where(lane2 == 1, 1.0, 0.0).astype(BF16)], axis=1)
        m_sc[...] = jnp.full(m_sc.shape, NEG, F32)

        def block(j, diag):
            start = pl.multiple_of(j * tq, tq)
            s = scores(qx, start, 1)
            if diag:
                s = causal(s)
            chunks = [s[:, c * LANES:(c + 1) * LANES] for c in range(tq // LANES)]
            mc = chunks[0]
            for ch in chunks[1:]:
                mc = jnp.maximum(mc, ch)
            m_prev = m_sc[...]
            m_new = jnp.maximum(m_prev, jnp.max(mc, axis=-1, keepdims=True))
            alpha = jnp.exp2(m_prev - m_new)
            p = jnp.concatenate([jnp.exp2(ch - m_new) for ch in chunks], axis=1).astype(BF16)
            pv = _dot(p, vx_sc[pl.ds(start, tq), :])
            acc_sc[...] = jnp.concatenate([alpha, alpha], axis=1) * acc_sc[...] + pv
            m_sc[...] = m_new

        def full(j, c):
            block(j, False)
            return c

        lax.fori_loop(0, i, full, 0)
        block(i, True)

    lp = lam_ref[...]
    lam = (jnp.exp(jnp.sum(lp[0:1] * lp[1:2], axis=-1, keepdims=True))
           - jnp.exp(jnp.sum(lp[2:3] * lp[3:4], axis=-1, keepdims=True)) + lam_init)
    acc = acc_sc[...]
    o_all = acc[:, :HEAD_DIM] / jnp.maximum(acc[:, HEAD_DIM:], 1e-37)
    o = o_all[:tq] - lam * o_all[tq:]
    o = _rms(o, nw_ref[...]) * (1.0 - lam_init)
    z = jnp.concatenate([r[...] for r in z_refs], axis=0).astype(F32)
    o_ref[...] = (o * (z * jax.nn.sigmoid(z))).astype(BF16)


def _attention(proj3, lam_params, attn_norm_w, lam_init):
    b = proj3.shape[0]
    hb = D_MODEL // HEAD_DIM
    nblk = LP // X0
    kern = functools.partial(_attn_kernel, lam_init=lam_init)

    def sub(s, col):
        return pl.BlockSpec((None, X0, HEAD_DIM),
                            lambda b_, h, i: (b_, jnp.minimum(AT_SUB * i + s, nblk - 1), col * hb + h))

    return pl.pallas_call(
        kern,
        grid=(b, HEADS, AT_TILES),
        in_specs=[
            pl.BlockSpec((4, QK_DIM), lambda b_, h, i: (0, 0)),
            pl.BlockSpec((None, LP, HEAD_DIM), lambda b_, h, i: (b_, 0, C_AQ * hb + h)),
            pl.BlockSpec((None, LP, HEAD_DIM), lambda b_, h, i: (b_, 0, C_AK * hb + h)),
            pl.BlockSpec((None, LP, HEAD_DIM), lambda b_, h, i: (b_, 0, C_AV * hb + h)),
        ] + [sub(s, C_AZ) for s in range(AT_SUB)] + [
            pl.BlockSpec((1, HEAD_DIM), lambda b_, h, i: (0, 0)),
        ],
        out_specs=pl.BlockSpec((None, AT_TQ, HEAD_DIM), lambda b_, h, i: (b_, i, h)),
        out_shape=jax.ShapeDtypeStruct((b, LP, D_MODEL), BF16),
        scratch_shapes=[
            pltpu.VMEM((AT_LA, HEAD_DIM), BF16),
            pltpu.VMEM((AT_LA, 2 * HEAD_DIM), BF16),
            pltpu.VMEM((AT_LA, 2 * HEAD_DIM), BF16),
            pltpu.VMEM((AT_LA, LANES), BF16),
            pltpu.VMEM((AT_LA, LANES), BF16),
            pltpu.SMEM((1,), jnp.int32),
            pltpu.VMEM((2 * AT_TQ, LANES), F32),
            pltpu.VMEM((2 * AT_TQ, 2 * HEAD_DIM), F32),
        ],
        compiler_params=pltpu.CompilerParams(
            dimension_semantics=("arbitrary", "arbitrary", "arbitrary"),
            vmem_limit_bytes=56 * 1024 * 1024),
        name="diff_attn",
    )(lam_params, proj3, proj3, proj3, *([proj3] * AT_SUB), attn_norm_w)


DN_T = 2 * CHUNK
HALO = 16


def _shift_matrix():
    r = jnp.arange((CONV_K - 1) * DN_T)
    src = HALO + r % DN_T - (r // DN_T + 1)
    return (src[:, None] == jnp.arange(HALO + DN_T)[None, :]).astype(BF16)


def _split3(x):
    hi = x.astype(BF16)
    r1 = x - hi.astype(F32)
    mid = r1.astype(BF16)
    lo = (r1 - mid.astype(F32)).astype(BF16)
    return hi, mid, lo


def _dn_prep_kernel(dq_ref, dk_ref, dv_ref, hq_ref, hk_ref, hv_ref, cw_ref, sh_ref, gcol_ref, grow_ref,
                    gpr_ref, gpc_ref, u_ref, w_ref, qd_ref, kd_ref, a_ref, cd_ref):
    i = pl.program_id(1)
    t = DN_T
    width = D_MODEL
    cw = cw_ref[...]
    shifter = sh_ref[...]
    ys = []
    for idx, (t_ref, h_ref) in enumerate(((dq_ref, hq_ref), (dk_ref, hk_ref), (dv_ref, hv_ref))):
        halo = h_ref[...]
        halo = jnp.where(i == 0, jnp.zeros_like(halo), halo)
        x = t_ref[...]
        shifted = _dot(shifter, jnp.concatenate([halo, x], axis=0))
        cwi = cw[:, idx * width:(idx + 1) * width]
        yi = cwi[CONV_K - 1:CONV_K] * x.astype(F32)
        for s in range(1, CONV_K):
            yi = yi + cwi[CONV_K - 1 - s:CONV_K - s] * shifted[(s - 1) * t:s * t]
        ys.append(yi * jax.nn.sigmoid(yi))
    y = jnp.concatenate(ys, axis=1)

    r = lax.broadcasted_iota(jnp.int32, (t, t), 0)
    c = lax.broadcasted_iota(jnp.int32, (t, t), 1)
    same = (r >= CHUNK) == (c >= CHUNK)
    incl = same & (c <= r)
    strict = same & (c < r)
    tri = jnp.where(incl, 1.0, 0.0).astype(BF16)
    tri_t = jnp.where(same & (r <= c), 1.0, 0.0).astype(BF16)
    blk = jnp.where(same, 1.0, 0.0).astype(BF16)
    eye = jnp.where(r == c, 1.0, 0.0).astype(F32)

    lane = lax.broadcasted_iota(jnp.int32, (t, LANES), 1)
    rowi = lax.broadcasted_iota(jnp.int32, (t, LANES), 0) + i * t
    gcol = gcol_ref[...]
    gpr = gpr_ref[...]
    is_g = (lane >= HEADS) & (lane < 2 * HEADS)
    live_c = rowi >= FRONT
    beta_c = jnp.where(live_c, jax.nn.sigmoid(gcol), 0.0)
    xg = gcol + gpr[1:2]
    sp = jnp.maximum(xg, 0.0) + jnp.log1p(jnp.exp(-jnp.abs(xg)))
    g_c = jnp.where(is_g & live_c, -jnp.exp(gpr[0:1]) * sp, 0.0)
    g3 = _split3(g_c)
    gc_c = _dot(tri, g3[0]) + _dot(tri, g3[1]) + _dot(tri, g3[2])
    gl_c = _dot(blk, g3[0]) + _dot(blk, g3[1]) + _dot(blk, g3[2])
    sub = lax.broadcasted_iota(jnp.int32, (2 * HEADS, t), 0)
    coli = lax.broadcasted_iota(jnp.int32, (2 * HEADS, t), 1) + i * t
    xr = grow_ref[...] + gpc_ref[1]
    spr = jnp.maximum(xr, 0.0) + jnp.log1p(jnp.exp(-jnp.abs(xr)))
    g_r = jnp.where((sub >= HEADS) & (coli >= FRONT), -jnp.exp(gpc_ref[0]) * spr, 0.0)
    gr3 = _split3(g_r)
    gc_r = _dot(gr3[0], tri_t) + _dot(gr3[1], tri_t) + _dot(gr3[2], tri_t)

    cd_ref[...] = jnp.exp(gl_c)

    hs = range(HEADS)
    sls = [slice(h * HEAD_DIM, (h + 1) * HEAD_DIM) for h in hs]
    q = [y[:, h * HEAD_DIM:(h + 1) * HEAD_DIM] for h in hs]
    k = [y[:, width + h * HEAD_DIM:width + (h + 1) * HEAD_DIM] for h in hs]
    v = [y[:, 2 * width + h * HEAD_DIM:2 * width + (h + 1) * HEAD_DIM] for h in hs]
    q = [x * lax.rsqrt(jnp.sum(x * x, axis=-1, keepdims=True) + 1e-6) * (HEAD_DIM ** -0.5) for x in q]
    k = [x * lax.rsqrt(jnp.sum(x * x, axis=-1, keepdims=True) + 1e-6) for x in k]
    beta = [beta_c[:, h:h + 1] for h in hs]
    gcc = [gc_c[:, HEADS + h:HEADS + h + 1] for h in hs]
    glc = [gl_c[:, HEADS + h:HEADS + h + 1] for h in hs]
    gcr = [gc_r[HEADS + h:HEADS + h + 1, :] for h in hs]
    dec = [jnp.where(incl, jnp.exp(jnp.where(incl, gcc[h] - gcr[h], 0.0)), 0.0) for h in hs]
    kb = [k[h] * beta[h] for h in hs]
    kt = [x.T.astype(BF16) for x in k]
    m = [jnp.where(strict, _dot(kb[h].astype(BF16), kt[h]) * dec[h], 0.0) for h in hs]
    tinv = [eye - x for x in m]
    pwb = [x.astype(BF16) for x in m]
    for _ in range(5):
        pw = [_dot(x, x) for x in pwb]
        pwb = [x.astype(BF16) for x in pw]
        tinv = [tinv[h] + _dot(tinv[h].astype(BF16), pwb[h]) for h in hs]
    egc = [jnp.exp(x) for x in gcc]
    rhs = [jnp.concatenate([v[h] * beta[h], kb[h] * egc[h]], axis=1).astype(BF16) for h in hs]
    uw = [_dot(tinv[h].astype(BF16), rhs[h]) for h in hs]
    a_full = [_dot(q[h].astype(BF16), kt[h]) * dec[h] for h in hs]
    for h in hs:
        u_ref[:, sls[h]] = uw[h][:, :HEAD_DIM].astype(BF16)
        w_ref[:, sls[h]] = uw[h][:, HEAD_DIM:].astype(BF16)
        a_cmp = a_full[h] + pltpu.roll(a_full[h], CHUNK, axis=1)
        a_ref[:, h * CHUNK:(h + 1) * CHUNK] = a_cmp[:, :CHUNK].astype(BF16)
        qd_ref[:, sls[h]] = (q[h] * egc[h]).astype(BF16)
        kd_ref[:, sls[h]] = (k[h] * jnp.exp(glc[h] - gcc[h])).astype(BF16)


def _dn_prep(proj3, conv_w2, gcol3, grow, gp_row, gp_col):
    b = proj3.shape[0]
    t = DN_T
    nt = LP // t
    hpb = t // HALO

    def tile(cb):
        return pl.BlockSpec((None, t, D_MODEL), lambda b_, i: (b_, i, cb))

    def halo(cb):
        return pl.BlockSpec((None, HALO, D_MODEL), lambda b_, i: (b_, jnp.maximum(i * hpb - 1, 0), cb))

    full = pl.BlockSpec((None, t, D_MODEL), lambda b_, i: (b_, i, 0))
    return pl.pallas_call(
        _dn_prep_kernel,
        grid=(b, nt),
        in_specs=[
            tile(C_DQ), tile(C_DK), tile(C_DV), halo(C_DQ), halo(C_DK), halo(C_DV),
            pl.BlockSpec((CONV_K, 3 * D_MODEL), lambda b_, i: (0, 0)),
            pl.BlockSpec(((CONV_K - 1) * t, HALO + t), lambda b_, i: (0, 0)),
            pl.BlockSpec((None, t, LANES), lambda b_, i: (b_, i, 0)),
            pl.BlockSpec((2 * HEADS, t), lambda b_, i: (0, b_ * nt + i)),
            pl.BlockSpec((2, LANES), lambda b_, i: (0, 0)),
            pl.BlockSpec((2, 2 * HEADS, t), lambda b_, i: (0, 0, 0)),
        ],
        out_specs=[
            full, full, full, full,
            pl.BlockSpec((None, t, HEADS * CHUNK), lambda b_, i: (b_, i, 0)),
            pl.BlockSpec((None, t, LANES), lambda b_, i: (b_, i, 0)),
        ],
        out_shape=[
            jax.ShapeDtypeStruct((b, LP, D_MODEL), BF16),
            jax.ShapeDtypeStruct((b, LP, D_MODEL), BF16),
            jax.ShapeDtypeStruct((b, LP, D_MODEL), BF16),
            jax.ShapeDtypeStruct((b, LP, D_MODEL), BF16),
            jax.ShapeDtypeStruct((b, LP, HEADS * CHUNK), BF16),
            jax.ShapeDtypeStruct((b, LP, LANES), F32),
        ],
        compiler_params=pltpu.CompilerParams(
            dimension_semantics=("parallel", "parallel"),
            vmem_limit_bytes=40 * 1024 * 1024),
        name="dn_prep",
    )(proj3, proj3, proj3, proj3, proj3, proj3, conv_w2, _shift_matrix(), gcol3, grow, gp_row, gp_col)


SC_T = 4 * CHUNK


def _dn_scan_kernel(u_ref, w_ref, qd_ref, kd_ref, a_ref, cd_ref, z_ref, nw_ref, o_ref, s_sc):
    @pl.when(pl.program_id(1) == 0)
    def _():
        s_sc[...] = jnp.zeros(s_sc.shape, F32)

    nw = nw_ref[...]
    hs = range(HEADS)
    sls = [slice(h * HEAD_DIM, (h + 1) * HEAD_DIM) for h in hs]
    s = [s_sc[h] for h in hs]
    for cix in range(SC_T // CHUNK):
        rs = slice(cix * CHUNK, (cix + 1) * CHUNK)
        sb = [x.astype(BF16) for x in s]
        ws_qs = [_dot(jnp.concatenate([w_ref[rs, sls[h]], qd_ref[rs, sls[h]]], axis=0), sb[h]) for h in hs]
        vb = [(u_ref[rs, sls[h]].astype(F32) - ws_qs[h][:CHUNK]).astype(BF16) for h in hs]
        o = [ws_qs[h][CHUNK:] + _dot(a_ref[rs, h * CHUNK:(h + 1) * CHUNK], vb[h]) for h in hs]
        cd = [cd_ref[cix * CHUNK:cix * CHUNK + 1, HEADS + h:HEADS + h + 1] for h in hs]
        s = [s[h] * cd[h] + _dot_tn(kd_ref[rs, sls[h]], vb[h]) for h in hs]
        for h in hs:
            z = z_ref[rs, sls[h]].astype(F32)
            o_ref[rs, sls[h]] = (_rms(o[h], nw) * (z * jax.nn.sigmoid(z))).astype(BF16)
    for h in hs:
        s_sc[h] = s[h]


def _dn_scan(u, w, qd, kd, a, cd, proj3, dn_norm_w):
    b = u.shape[0]
    t = SC_T
    full = pl.BlockSpec((None, t, D_MODEL), lambda b_, i: (b_, i, 0))
    return pl.pallas_call(
        _dn_scan_kernel,
        grid=(b, LP // t),
        in_specs=[
            full, full, full, full,
            pl.BlockSpec((None, t, HEADS * CHUNK), lambda b_, i: (b_, i, 0)),
            pl.BlockSpec((None, t, LANES), lambda b_, i: (b_, i, 0)),
            pl.BlockSpec((None, t, D_MODEL), lambda b_, i: (b_, i, C_DZ)),
            pl.BlockSpec((1, HEAD_DIM), lambda b_, i: (0, 0)),
        ],
        out_specs=full,
        out_shape=jax.ShapeDtypeStruct((b, LP, D_MODEL), BF16),
        scratch_shapes=[pltpu.VMEM((HEADS, HEAD_DIM, HEAD_DIM), F32)],
        compiler_params=pltpu.CompilerParams(
            dimension_semantics=("parallel", "arbitrary"),
            vmem_limit_bytes=40 * 1024 * 1024),
        name="dn_scan",
    )(u, w, qd, kd, a, cd, proj3, dn_norm_w)


OUT_SUB = 2


def _merge_kernel(*refs):
    n = OUT_SUB
    xa_refs, xd_refs, ga_refs, gd_refs = refs[0:n], refs[n:2 * n], refs[2 * n:3 * n], refs[3 * n:4 * n]
    x_ref, wa_ref, wd_ref, wo_ref, nw_ref, o_ref = refs[4 * n:]

    def rows(rs):
        return jnp.concatenate([r[...] for r in rs], axis=0)

    ya = _dot(rows(xa_refs), wa_ref[...])
    yd = _dot(rows(xd_refs), wd_ref[...])
    merged = (jax.nn.sigmoid(rows(ga_refs).astype(F32)) * ya
              + jax.nn.sigmoid(rows(gd_refs).astype(F32)) * yd)
    out = x_ref[...] + _dot(merged.astype(BF16), wo_ref[...])
    o_ref[...] = _rms(out, nw_ref[...])


def _merge(xa, xd, proj3, x, wa, wd, wo, final_norm_w):
    b = x.shape[0]
    tm = OUT_SUB * X0

    def padded(cb):
        return [pl.BlockSpec((None, X0, D_MODEL), lambda b_, i, s=s: (b_, OUT_SUB * i + 1 + s, cb))
                for s in range(OUT_SUB)]

    tok = pl.BlockSpec((None, tm, D_MODEL), lambda b_, i: (b_, i, 0))
    wspec = pl.BlockSpec((D_MODEL, D_MODEL), lambda b_, i: (0, 0))
    return pl.pallas_call(
        _merge_kernel,
        grid=(b, SEQ // tm),
        in_specs=padded(0) + padded(0) + padded(C_GA) + padded(C_GD) + [
            tok, wspec, wspec, wspec,
            pl.BlockSpec((1, D_MODEL), lambda b_, i: (0, 0)),
        ],
        out_specs=tok,
        out_shape=jax.ShapeDtypeStruct((b, SEQ, D_MODEL), F32),
        compiler_params=pltpu.CompilerParams(
            dimension_semantics=("parallel", "parallel"),
            vmem_limit_bytes=48 * 1024 * 1024),
        name="merge_out",
    )(*([xa] * OUT_SUB + [xd] * OUT_SUB + [proj3] * (2 * OUT_SUB)), x, wa, wd, wo, final_norm_w)


def kernel(x, meta_tokens, norm_w, w_in, lambda_q1, lambda_k1, lambda_q2, lambda_k2, attn_norm_w,
           conv_w, a_log, dt_bias, dn_norm_w, w_branch_attn, w_branch_delta, w_out, final_norm_w):
    b = x.shape[0]
    assert x.shape == (b, SEQ, D_MODEL) and norm_w.shape[0] == 1
    layer = 0
    lam_init = 0.8 - 0.6 * math.exp(-0.3 * layer)

    head = jnp.concatenate([jnp.zeros((FRONT, D_MODEL), x.dtype), meta_tokens.astype(x.dtype)], axis=0)

    wi = w_in[layer]
    gate0 = 8 * D_MODEL
    q_scale = QK_DIM ** -0.5 * math.log2(math.e)
    col_scale = jnp.concatenate([jnp.full((1, D_MODEL), q_scale, F32), jnp.ones((1, gate0 - D_MODEL), F32)], axis=1)
    w_all = wi.astype(BF16)
    w_tail = wi[:, gate0 + 2 * HEADS:].astype(BF16)
    w_gate = wi[:, gate0:gate0 + 2 * HEADS]
    w_gate_p = jnp.pad(w_gate, ((0, 0), (0, LANES - 2 * HEADS))).astype(BF16)
    w_gate_t = w_gate.T.astype(BF16)
    zeros8 = jnp.zeros((HEADS,), F32)
    gp_row = jnp.stack([jnp.pad(jnp.concatenate([zeros8, a_log[layer]]), (0, LANES - 2 * HEADS)),
                        jnp.pad(jnp.concatenate([zeros8, dt_bias[layer]]), (0, LANES - 2 * HEADS))])
    gp_col = jnp.stack([jnp.concatenate([zeros8, a_log[layer]]),
                        jnp.concatenate([zeros8, dt_bias[layer]])])
    gp_col = jnp.broadcast_to(gp_col[:, :, None], (2, 2 * HEADS, DN_T)).astype(F32)
    lam_params = jnp.stack([lambda_q1[layer], lambda_k1[layer], lambda_q2[layer], lambda_k2[layer]])

    proj2, gcol, grow = _inproj(head, x, norm_w[layer][None], w_all, w_tail, col_scale, w_gate_p, w_gate_t)
    proj3 = proj2.reshape(b, LP, N_MAIN)

    xa = _attention(proj3, lam_params, attn_norm_w[layer][None], lam_init)

    u, w, qd, kd, a, cd = _dn_prep(proj3, conv_w[layer], gcol.reshape(b, LP, LANES), grow, gp_row, gp_col)
    xd = _dn_scan(u, w, qd, kd, a, cd, proj3, dn_norm_w[layer][None])

    return _merge(xa, xd, proj3, x, w_branch_attn[layer].astype(BF16), w_branch_delta[layer].astype(BF16),
                  w_out[layer].astype(BF16), final_norm_w[None])
```

```python
import functools
import math

import jax
import jax.numpy as jnp
from jax import lax
from jax.experimental import pallas as pl
from jax.experimental.pallas import tpu as pltpu

D_MODEL = 1024
SEQ = 8192
N_META = 16
NORM_EPS = 1e-6
HEADS = 8
HEAD_DIM = 128
QK_DIM = 64
CONV_K = 4
CHUNK = 64
LANES = 128
LP = 8448
FRONT = LP - SEQ - N_META
X0 = FRONT + N_META
N_MAIN = 10 * D_MODEL
NEG = -1e30

C_AQ, C_AK, C_AV, C_AZ, C_DQ, C_DK, C_DV, C_DZ, C_GA, C_GD = range(10)

F32 = jnp.float32
BF16 = jnp.bfloat16


def _rms(x, w):
    return x * lax.rsqrt(jnp.mean(x * x, axis=-1, keepdims=True) + NORM_EPS) * w


def _dot(a, b):
    return jnp.dot(a, b, preferred_element_type=F32)


def _dot_nt(a, b):
    return lax.dot_general(a, b, (((1,), (1,)), ((), ())), preferred_element_type=F32)


def _dot_tn(a, b):
    return lax.dot_general(a, b, (((0,), (0,)), ((), ())), preferred_element_type=F32)


IN_SUB = 3
IN_TM = IN_SUB * X0
IN_TN = 2048
IN_TILES = LP // IN_TM
IN_FULL_TILES = 8 * D_MODEL // IN_TN


def _inproj_kernel(*refs):
    head_ref = refs[0]
    x_refs = refs[1:1 + IN_SUB]
    nw_ref, w_ref, wt_ref, cs_ref, wg_ref, wgt_ref, p_ref, gcol_ref, grow_ref, hn_sc = refs[1 + IN_SUB:]
    j = pl.program_id(1)

    @pl.when(j == 0)
    def _():
        first_tile = pl.program_id(0) % IN_TILES == 0
        nw = nw_ref[...]
        for s, x_ref in enumerate(x_refs):
            h = x_ref[...]
            if s == 0:
                h = jnp.where(first_tile, head_ref[...], h)
            hn_sc[s * X0:(s + 1) * X0, :] = _rms(h, nw).astype(BF16)
        hn = hn_sc[...]
        gcol_ref[...] = _dot(hn, wg_ref[...])
        grow_ref[...] = _dot_nt(wgt_ref[...], hn)

    @pl.when(j < IN_FULL_TILES)
    def _():
        p_ref[...] = (_dot(hn_sc[...], w_ref[...]) * cs_ref[...]).astype(BF16)

    @pl.when(j >= IN_FULL_TILES)
    def _():
        p_ref[...] = _dot(hn_sc[...], wt_ref[...]).astype(BF16)


def _inproj(head, x, norm_w, w_all, w_tail, col_scale, w_gate, w_gate_t):
    b = x.shape[0]
    rows = b * LP
    grid = (rows // IN_TM, N_MAIN // IN_TN)

    def xblock(s):
        return pl.BlockSpec((None, X0, D_MODEL),
                            lambda i, j: (i // IN_TILES, jnp.maximum((i % IN_TILES) * IN_SUB + s - 1, 0), 0))

    return pl.pallas_call(
        _inproj_kernel,
        grid=grid,
        in_specs=[pl.BlockSpec((X0, D_MODEL), lambda i, j: (0, 0), pipeline_mode=pl.Buffered(1))]
        + [xblock(s) for s in range(IN_SUB)] + [
            pl.BlockSpec((1, D_MODEL), lambda i, j: (0, 0)),
            pl.BlockSpec((D_MODEL, IN_TN), lambda i, j: (0, jnp.minimum(j, IN_FULL_TILES - 1))),
            pl.BlockSpec((D_MODEL, IN_TN), lambda i, j: (0, jnp.maximum(j - IN_FULL_TILES, 0))),
            pl.BlockSpec((1, IN_TN), lambda i, j: (0, jnp.minimum(j, IN_FULL_TILES - 1))),
            pl.BlockSpec((D_MODEL, LANES), lambda i, j: (0, 0)),
            pl.BlockSpec((2 * HEADS, D_MODEL), lambda i, j: (0, 0)),
        ],
        out_specs=[
            pl.BlockSpec((IN_TM, IN_TN), lambda i, j: (i, j)),
            pl.BlockSpec((IN_TM, LANES), lambda i, j: (i, 0)),
            pl.BlockSpec((2 * HEADS, IN_TM), lambda i, j: (0, i)),
        ],
        out_shape=[
            jax.ShapeDtypeStruct((rows, N_MAIN), BF16),
            jax.ShapeDtypeStruct((rows, LANES), F32),
            jax.ShapeDtypeStruct((2 * HEADS, rows), F32),
        ],
        scratch_shapes=[pltpu.VMEM((IN_TM, D_MODEL), BF16)],
        compiler_params=pltpu.CompilerParams(
            dimension_semantics=("parallel", "arbitrary"),
            vmem_limit_bytes=56 * 1024 * 1024),
        name="inproj",
    )(head, *([x] * IN_SUB), norm_w, w_all, w_tail, col_scale, w_gate, w_gate_t)


AT_SUB = 2
AT_TQ = AT_SUB * X0
AT_BIG = 6
AT_TILES = -(-LP // AT_TQ)
AT_LA = AT_TILES * AT_TQ
AT_SHIFT_MAX = 30.0


def _attn_kernel(*refs, lam_init):
    lam_ref, q_ref, k_ref, v_ref = refs[0:4]
    z_refs = refs[4:4 + AT_SUB]
    nw_ref = refs[4 + AT_SUB]
    o_ref = refs[5 + AT_SUB]
    qx_sc, kx_sc, vx_sc, e1_sc, e2_sc, flag_sc, m_sc, acc_sc = refs[6 + AT_SUB:]
    i = pl.program_id(2)
    tq = AT_TQ

    @pl.when((pl.program_id(0) == 0) & (pl.program_id(1) == 0) & (i == 0))
    def _():
        rowi = lax.broadcasted_iota(jnp.int32, (LP, LANES), 0)
        ln = lax.broadcasted_iota(jnp.int32, (LP, LANES), 1)
        ext = jnp.where(ln == 0, 1.0, jnp.where((ln == 1) & (rowi < FRONT), NEG, 0.0))
        kx_sc[0:LP, HEAD_DIM:] = ext.astype(BF16)
        vx_sc[0:LP, HEAD_DIM:] = jnp.ones((LP, HEAD_DIM), BF16)
        if AT_LA > LP:
            for ref in (qx_sc, kx_sc, vx_sc, e1_sc, e2_sc):
                ref[LP:, :] = jnp.zeros((AT_LA - LP, ref.shape[1]), BF16)

    @pl.when(i == 0)
    def _():
        k = k_ref[...]
        kx_sc[0:LP, 0:HEAD_DIM] = k
        vx_sc[0:LP, 0:HEAD_DIM] = v_ref[...]
        ln = lax.broadcasted_iota(jnp.int32, (LP, LANES), 1)
        first = ln < QK_DIM
        pad_lane = jnp.where(ln == 1, 1.0, 0.0)

        def half_norms(x):
            sq = x.astype(F32)
            sq = sq * sq
            return (jnp.sqrt(jnp.sum(jnp.where(first, sq, 0.0), axis=-1, keepdims=True)),
                    jnp.sqrt(jnp.sum(jnp.where(first, 0.0, sq), axis=-1, keepdims=True)))

        q = q_ref[...]
        qx_sc[0:LP, :] = q
        kn1, kn2 = half_norms(k)
        qn1, qn2 = half_norms(q)
        shift1 = qn1 * jnp.max(kn1, axis=0, keepdims=True)
        shift2 = qn2 * jnp.max(kn2, axis=0, keepdims=True)
        e1_sc[0:LP, :] = (pad_lane - jnp.where(ln == 0, shift1, 0.0)).astype(BF16)
        e2_sc[0:LP, :] = (pad_lane - jnp.where(ln == 0, shift2, 0.0)).astype(BF16)
        ok = jnp.maximum(jnp.max(shift1), jnp.max(shift2)) <= AT_SHIFT_MAX
        flag_sc[0] = ok.astype(jnp.int32)

    row0 = pl.multiple_of(i * tq, tq)
    q = qx_sc[pl.ds(row0, tq), :]
    first = lax.broadcasted_iota(jnp.int32, (tq, LANES), 1) < QK_DIM
    zero = jnp.zeros_like(q)
    use_shift = flag_sc[0] == 1
    q2 = jnp.concatenate([jnp.where(first, q, zero), jnp.where(first, zero, q)], axis=0)

    acc_sc[...] = jnp.zeros(acc_sc.shape, F32)

    def causal(s):
        r2 = lax.broadcasted_iota(jnp.int32, (2 * tq, tq), 0)
        row = jnp.where(r2 >= tq, r2 - tq, r2)
        col = lax.broadcasted_iota(jnp.int32, (2 * tq, tq), 1)
        return jnp.where(col <= row, s, NEG)

    def scores(qx, start, units):
        return _dot_nt(qx, kx_sc[pl.ds(start, units * tq), :])

    @pl.when(use_shift)
    def _():
        bias = jnp.concatenate([e1_sc[pl.ds(row0, tq), :], e2_sc[pl.ds(row0, tq), :]], axis=0)
        qx = jnp.concatenate([q2, bias], axis=1)

        def consume(start, units, s):
            p = jnp.exp2(s).astype(BF16)
            acc_sc[...] += _dot(p, vx_sc[pl.ds(start, units * tq), :])

        n_big = i // AT_BIG

        def big(j, c):
            start = pl.multiple_of(j * (AT_BIG * tq), AT_BIG * tq)
            consume(start, AT_BIG, scores(qx, start, AT_BIG))
            return c

        lax.fori_loop(0, n_big, big, 0)
        tail_units = i - n_big * AT_BIG + 1
        tail_start = pl.multiple_of(n_big * (AT_BIG * tq), AT_BIG * tq)
        for units in range(1, AT_BIG + 1):
            @pl.when(tail_units == units)
            def _():
                s = scores(qx, tail_start, units)
                s_diag = causal(s[:, (units - 1) * tq:])
                if units > 1:
                    s_diag = jnp.concatenate([s[:, :(units - 1) * tq], s_diag], axis=1)
                consume(tail_start, units, s_diag)

    @pl.when(jnp.logical_not(use_shift))
    def _():
        lane2 = lax.broadcasted_iota(jnp.int32, (2 * tq, LANES), 1)
        qx = jnp.concatenate([q2, jnp.where(lane2 == 1, 1.0, 0.0).astype(BF16)], axis=1)
        m_sc[...] = jnp.full(m_sc.shape, NEG, F32)

        def block(j, diag):
            start = pl.multiple_of(j * tq, tq)
            s = scores(qx, start, 1)
            if diag:
                s = causal(s)
            chunks = [s[:, c * LANES:(c + 1) * LANES] for c in range(tq // LANES)]
            mc = chunks[0]
            for ch in chunks[1:]:
                mc = jnp.maximum(mc, ch)
            m_prev = m_sc[...]
            m_new = jnp.maximum(m_prev, jnp.max(mc, axis=-1, keepdims=True))
            alpha = jnp.exp2(m_prev - m_new)
            p = jnp.concatenate([jnp.exp2(ch - m_new) for ch in chunks], axis=1).astype(BF16)
            pv = _dot(p, vx_sc[pl.ds(start, tq), :])
            acc_sc[...] = jnp.concatenate([alpha, alpha], axis=1) * acc_sc[...] + pv
            m_sc[...] = m_new

        def full(j, c):
            block(j, False)
            return c

        lax.fori_loop(0, i, full, 0)
        block(i, True)

    lp = lam_ref[...]
    lam = (jnp.exp(jnp.sum(lp[0:1] * lp[1:2], axis=-1, keepdims=True))
           - jnp.exp(jnp.sum(lp[2:3] * lp[3:4], axis=-1, keepdims=True)) + lam_init)
    acc = acc_sc[...]
    o_all = acc[:, :HEAD_DIM] / jnp.maximum(acc[:, HEAD_DIM:], 1e-37)
    o = o_all[:tq] - lam * o_all[tq:]
    o = _rms(o, nw_ref[...]) * (1.0 - lam_init)
    z = jnp.concatenate([r[...] for r in z_refs], axis=0).astype(F32)
    o_ref[...] = (o * (z * jax.nn.sigmoid(z))).astype(BF16)


def _attention(proj3, lam_params, attn_norm_w, lam_init):
    b = proj3.shape[0]
    hb = D_MODEL // HEAD_DIM
    nblk = LP // X0
    kern = functools.partial(_attn_kernel, lam_init=lam_init)

    def sub(s, col):
        return pl.BlockSpec((None, X0, HEAD_DIM),
                            lambda b_, h, i: (b_, jnp.minimum(AT_SUB * i + s, nblk - 1), col * hb + h))

    return pl.pallas_call(
        kern,
        grid=(b, HEADS, AT_TILES),
        in_specs=[
            pl.BlockSpec((4, QK_DIM), lambda b_, h, i: (0, 0)),
            pl.BlockSpec((None, LP, HEAD_DIM), lambda b_, h, i: (b_, 0, C_AQ * hb + h)),
            pl.BlockSpec((None, LP, HEAD_DIM), lambda b_, h, i: (b_, 0, C_AK * hb + h)),
            pl.BlockSpec((None, LP, HEAD_DIM), lambda b_, h, i: (b_, 0, C_AV * hb + h)),
        ] + [sub(s, C_AZ) for s in range(AT_SUB)] + [
            pl.BlockSpec((1, HEAD_DIM), lambda b_, h, i: (0, 0)),
        ],
        out_specs=pl.BlockSpec((None, AT_TQ, HEAD_DIM), lambda b_, h, i: (b_, i, h)),
        out_shape=jax.ShapeDtypeStruct((b, LP, D_MODEL), BF16),
        scratch_shapes=[
            pltpu.VMEM((AT_LA, HEAD_DIM), BF16),
            pltpu.VMEM((AT_LA, 2 * HEAD_DIM), BF16),
            pltpu.VMEM((AT_LA, 2 * HEAD_DIM), BF16),
            pltpu.VMEM((AT_LA, LANES), BF16),
            pltpu.VMEM((AT_LA, LANES), BF16),
            pltpu.SMEM((1,), jnp.int32),
            pltpu.VMEM((2 * AT_TQ, LANES), F32),
            pltpu.VMEM((2 * AT_TQ, 2 * HEAD_DIM), F32),
        ],
        compiler_params=pltpu.CompilerParams(
            dimension_semantics=("arbitrary", "arbitrary", "arbitrary"),
            vmem_limit_bytes=56 * 1024 * 1024),
        name="diff_attn",
    )(lam_params, proj3, proj3, proj3, *([proj3] * AT_SUB), attn_norm_w)


DN_T = 2 * CHUNK
HALO = 16
DN_GROUP = 8


def _shift_matrix():
    r = jnp.arange((CONV_K - 1) * DN_T)
    src = HALO + r % DN_T - (r // DN_T + 1)
    return (src[:, None] == jnp.arange(HALO + DN_T)[None, :]).astype(BF16)


def _split3(x):
    hi = x.astype(BF16)
    r1 = x - hi.astype(F32)
    mid = r1.astype(BF16)
    lo = (r1 - mid.astype(F32)).astype(BF16)
    return hi, mid, lo


def _dn_prep_kernel(dq_ref, dk_ref, dv_ref, hq_ref, hk_ref, hv_ref, cw_ref, sh_ref, gcol_ref, grow_ref,
                    gpr_ref, gpc_ref, u_ref, w_ref, qd_ref, kd_ref, a_ref, cd_ref):
    i = pl.program_id(1)
    t = DN_T
    width = D_MODEL
    cw = cw_ref[...]
    shifter = sh_ref[...]
    ys = []
    for idx, (t_ref, h_ref) in enumerate(((dq_ref, hq_ref), (dk_ref, hk_ref), (dv_ref, hv_ref))):
        halo = h_ref[...]
        halo = jnp.where(i == 0, jnp.zeros_like(halo), halo)
        x = t_ref[...]
        shifted = _dot(shifter, jnp.concatenate([halo, x], axis=0))
        cwi = cw[:, idx * width:(idx + 1) * width]
        yi = cwi[CONV_K - 1:CONV_K] * x.astype(F32)
        for s in range(1, CONV_K):
            yi = yi + cwi[CONV_K - 1 - s:CONV_K - s] * shifted[(s - 1) * t:s * t]
        ys.append(yi * jax.nn.sigmoid(yi))
    y = jnp.concatenate(ys, axis=1)

    r = lax.broadcasted_iota(jnp.int32, (t, t), 0)
    c = lax.broadcasted_iota(jnp.int32, (t, t), 1)
    same = (r >= CHUNK) == (c >= CHUNK)
    incl = same & (c <= r)
    strict = same & (c < r)
    tri = jnp.where(incl, 1.0, 0.0).astype(BF16)
    tri_t = jnp.where(same & (r <= c), 1.0, 0.0).astype(BF16)
    blk = jnp.where(same, 1.0, 0.0).astype(BF16)
    eye = jnp.where(r == c, 1.0, 0.0).astype(F32)

    lane = lax.broadcasted_iota(jnp.int32, (t, LANES), 1)
    rowi = lax.broadcasted_iota(jnp.int32, (t, LANES), 0) + i * t
    gcol = gcol_ref[...]
    gpr = gpr_ref[...]
    is_g = (lane >= HEADS) & (lane < 2 * HEADS)
    live_c = rowi >= FRONT
    beta_c = jnp.where(live_c, jax.nn.sigmoid(gcol), 0.0)
    xg = gcol + gpr[1:2]
    sp = jnp.maximum(xg, 0.0) + jnp.log1p(jnp.exp(-jnp.abs(xg)))
    g_c = jnp.where(is_g & live_c, -jnp.exp(gpr[0:1]) * sp, 0.0)
    g3 = _split3(g_c)
    gc_c = _dot(tri, g3[0]) + _dot(tri, g3[1]) + _dot(tri, g3[2])
    gl_c = _dot(blk, g3[0]) + _dot(blk, g3[1]) + _dot(blk, g3[2])
    sub = lax.broadcasted_iota(jnp.int32, (2 * HEADS, t), 0)
    coli = lax.broadcasted_iota(jnp.int32, (2 * HEADS, t), 1) + i * t
    xr = grow_ref[...] + gpc_ref[1]
    spr = jnp.maximum(xr, 0.0) + jnp.log1p(jnp.exp(-jnp.abs(xr)))
    g_r = jnp.where((sub >= HEADS) & (coli >= FRONT), -jnp.exp(gpc_ref[0]) * spr, 0.0)
    gr3 = _split3(g_r)
    gc_r = _dot(gr3[0], tri_t) + _dot(gr3[1], tri_t) + _dot(gr3[2], tri_t)

    cd_ref[...] = jnp.exp(gl_c)

    def group(hs):
        sls = {h: slice(h * HEAD_DIM, (h + 1) * HEAD_DIM) for h in hs}
        q = {h: y[:, h * HEAD_DIM:(h + 1) * HEAD_DIM] for h in hs}
        k = {h: y[:, width + h * HEAD_DIM:width + (h + 1) * HEAD_DIM] for h in hs}
        v = {h: y[:, 2 * width + h * HEAD_DIM:2 * width + (h + 1) * HEAD_DIM] for h in hs}
        q = {h: x * lax.rsqrt(jnp.sum(x * x, axis=-1, keepdims=True) + 1e-6) * (HEAD_DIM ** -0.5)
             for h, x in q.items()}
        k = {h: x * lax.rsqrt(jnp.sum(x * x, axis=-1, keepdims=True) + 1e-6) for h, x in k.items()}
        beta = {h: beta_c[:, h:h + 1] for h in hs}
        gcc = {h: gc_c[:, HEADS + h:HEADS + h + 1] for h in hs}
        glc = {h: gl_c[:, HEADS + h:HEADS + h + 1] for h in hs}
        gcr = {h: gc_r[HEADS + h:HEADS + h + 1, :] for h in hs}
        dec = {h: jnp.where(incl, jnp.exp(jnp.where(incl, gcc[h] - gcr[h], 0.0)), 0.0) for h in hs}
        kb = {h: k[h] * beta[h] for h in hs}
        kt = {h: k[h].T.astype(BF16) for h in hs}
        m = {h: jnp.where(strict, _dot(kb[h].astype(BF16), kt[h]) * dec[h], 0.0) for h in hs}
        tinv = {h: eye - m[h] for h in hs}
        pwb = {h: m[h].astype(BF16) for h in hs}
        for _ in range(5):
            pw = {h: _dot(pwb[h], pwb[h]) for h in hs}
            pwb = {h: pw[h].astype(BF16) for h in hs}
            tinv = {h: tinv[h] + _dot(tinv[h].astype(BF16), pwb[h]) for h in hs}
        egc = {h: jnp.exp(gcc[h]) for h in hs}
        rhs = {h: jnp.concatenate([v[h] * beta[h], kb[h] * egc[h]], axis=1).astype(BF16) for h in hs}
        uw = {h: _dot(tinv[h].astype(BF16), rhs[h]) for h in hs}
        a_full = {h: _dot(q[h].astype(BF16), kt[h]) * dec[h] for h in hs}
        for h in hs:
            u_ref[:, sls[h]] = uw[h][:, :HEAD_DIM].astype(BF16)
            w_ref[:, sls[h]] = uw[h][:, HEAD_DIM:].astype(BF16)
            a_cmp = a_full[h] + pltpu.roll(a_full[h], CHUNK, axis=1)
            a_ref[:, h * CHUNK:(h + 1) * CHUNK] = a_cmp[:, :CHUNK].astype(BF16)
            qd_ref[:, sls[h]] = (q[h] * egc[h]).astype(BF16)
            kd_ref[:, sls[h]] = (k[h] * jnp.exp(glc[h] - gcc[h])).astype(BF16)

    for g in range(HEADS // DN_GROUP):
        group(range(g * DN_GROUP, (g + 1) * DN_GROUP))


def _dn_prep(proj3, conv_w2, gcol3, grow, gp_row, gp_col):
    b = proj3.shape[0]
    t = DN_T
    nt = LP // t
    hpb = t // HALO

    def tile(cb):
        return pl.BlockSpec((None, t, D_MODEL), lambda b_, i: (b_, i, cb))

    def halo(cb):
        return pl.BlockSpec((None, HALO, D_MODEL), lambda b_, i: (b_, jnp.maximum(i * hpb - 1, 0), cb))

    full = pl.BlockSpec((None, t, D_MODEL), lambda b_, i: (b_, i, 0))
    return pl.pallas_call(
        _dn_prep_kernel,
        grid=(b, nt),
        in_specs=[
            tile(C_DQ), tile(C_DK), tile(C_DV), halo(C_DQ), halo(C_DK), halo(C_DV),
            pl.BlockSpec((CONV_K, 3 * D_MODEL), lambda b_, i: (0, 0)),
            pl.BlockSpec(((CONV_K - 1) * t, HALO + t), lambda b_, i: (0, 0)),
            pl.BlockSpec((None, t, LANES), lambda b_, i: (b_, i, 0)),
            pl.BlockSpec((2 * HEADS, t), lambda b_, i: (0, b_ * nt + i)),
            pl.BlockSpec((2, LANES), lambda b_, i: (0, 0)),
            pl.BlockSpec((2, 2 * HEADS, t), lambda b_, i: (0, 0, 0)),
        ],
        out_specs=[
            full, full, full, full,
            pl.BlockSpec((None, t, HEADS * CHUNK), lambda b_, i: (b_, i, 0)),
            pl.BlockSpec((None, t, LANES), lambda b_, i: (b_, i, 0)),
        ],
        out_shape=[
            jax.ShapeDtypeStruct((b, LP, D_MODEL), BF16),
            jax.ShapeDtypeStruct((b, LP, D_MODEL), BF16),
            jax.ShapeDtypeStruct((b, LP, D_MODEL), BF16),
            jax.ShapeDtypeStruct((b, LP, D_MODEL), BF16),
            jax.ShapeDtypeStruct((b, LP, HEADS * CHUNK), BF16),
            jax.ShapeDtypeStruct((b, LP, LANES), F32),
        ],
        compiler_params=pltpu.CompilerParams(
            dimension_semantics=("parallel", "parallel"),
            vmem_limit_bytes=40 * 1024 * 1024),
        name="dn_prep",
    )(proj3, proj3, proj3, proj3, proj3, proj3, conv_w2, _shift_matrix(), gcol3, grow, gp_row, gp_col)


SC_T = 4 * CHUNK


def _dn_scan_kernel(u_ref, w_ref, qd_ref, kd_ref, a_ref, cd_ref, z_ref, nw_ref, o_ref, s_sc):
    @pl.when(pl.program_id(1) == 0)
    def _():
        s_sc[...] = jnp.zeros(s_sc.shape, F32)

    nw = nw_ref[...]
    hs = range(HEADS)
    sls = [slice(h * HEAD_DIM, (h + 1) * HEAD_DIM) for h in hs]
    s = [s_sc[h] for h in hs]
    for cix in range(SC_T // CHUNK):
        rs = slice(cix * CHUNK, (cix + 1) * CHUNK)
        sb = [x.astype(BF16) for x in s]
        ws_qs = [_dot(jnp.concatenate([w_ref[rs, sls[h]], qd_ref[rs, sls[h]]], axis=0), sb[h]) for h in hs]
        vb = [(u_ref[rs, sls[h]].astype(F32) - ws_qs[h][:CHUNK]).astype(BF16) for h in hs]
        o = [ws_qs[h][CHUNK:] + _dot(a_ref[rs, h * CHUNK:(h + 1) * CHUNK], vb[h]) for h in hs]
        cd = [cd_ref[cix * CHUNK:cix * CHUNK + 1, HEADS + h:HEADS + h + 1] for h in hs]
        s = [s[h] * cd[h] + _dot_tn(kd_ref[rs, sls[h]], vb[h]) for h in hs]
        for h in hs:
            z = z_ref[rs, sls[h]].astype(F32)
            o_ref[rs, sls[h]] = (_rms(o[h], nw) * (z * jax.nn.sigmoid(z))).astype(BF16)
    for h in hs:
        s_sc[h] = s[h]


def _dn_scan(u, w, qd, kd, a, cd, proj3, dn_norm_w):
    b = u.shape[0]
    t = SC_T
    full = pl.BlockSpec((None, t, D_MODEL), lambda b_, i: (b_, i, 0))
    return pl.pallas_call(
        _dn_scan_kernel,
        grid=(b, LP // t),
        in_specs=[
            full, full, full, full,
            pl.BlockSpec((None, t, HEADS * CHUNK), lambda b_, i: (b_, i, 0)),
            pl.BlockSpec((None, t, LANES), lambda b_, i: (b_, i, 0)),
            pl.BlockSpec((None, t, D_MODEL), lambda b_, i: (b_, i, C_DZ)),
            pl.BlockSpec((1, HEAD_DIM), lambda b_, i: (0, 0)),
        ],
        out_specs=full,
        out_shape=jax.ShapeDtypeStruct((b, LP, D_MODEL), BF16),
        scratch_shapes=[pltpu.VMEM((HEADS, HEAD_DIM, HEAD_DIM), F32)],
        compiler_params=pltpu.CompilerParams(
            dimension_semantics=("parallel", "arbitrary"),
            vmem_limit_bytes=40 * 1024 * 1024),
        name="dn_scan",
    )(u, w, qd, kd, a, cd, proj3, dn_norm_w)


OUT_SUB = 2


def _merge_kernel(*refs):
    n = OUT_SUB
    xa_refs, xd_refs, ga_refs, gd_refs = refs[0:n], refs[n:2 * n], refs[2 * n:3 * n], refs[3 * n:4 * n]
    x_ref, wa_ref, wd_ref, wo_ref, nw_ref, o_ref = refs[4 * n:]

    def rows(rs):
        return jnp.concatenate([r[...] for r in rs], axis=0)

    ya = _dot(rows(xa_refs), wa_ref[...])
    yd = _dot(rows(xd_refs), wd_ref[...])
    merged = (jax.nn.sigmoid(rows(ga_refs).astype(F32)) * ya
              + jax.nn.sigmoid(rows(gd_refs).astype(F32)) * yd)
    out = x_ref[...] + _dot(merged.astype(BF16), wo_ref[...])
    o_ref[...] = _rms(out, nw_ref[...])


def _merge(xa, xd, proj3, x, wa, wd, wo, final_norm_w):
    b = x.shape[0]
    tm = OUT_SUB * X0

    def padded(cb):
        return [pl.BlockSpec((None, X0, D_MODEL), lambda b_, i, s=s: (b_, OUT_SUB * i + 1 + s, cb))
                for s in range(OUT_SUB)]

    tok = pl.BlockSpec((None, tm, D_MODEL), lambda b_, i: (b_, i, 0))
    wspec = pl.BlockSpec((D_MODEL, D_MODEL), lambda b_, i: (0, 0))
    return pl.pallas_call(
        _merge_kernel,
        grid=(b, SEQ // tm),
        in_specs=padded(0) + padded(0) + padded(C_GA) + padded(C_GD) + [
            tok, wspec, wspec, wspec,
            pl.BlockSpec((1, D_MODEL), lambda b_, i: (0, 0)),
        ],
        out_specs=tok,
        out_shape=jax.ShapeDtypeStruct((b, SEQ, D_MODEL), F32),
        compiler_params=pltpu.CompilerParams(
            dimension_semantics=("parallel", "parallel"),
            vmem_limit_bytes=48 * 1024 * 1024),
        name="merge_out",
    )(*([xa] * OUT_SUB + [xd] * OUT_SUB + [proj3] * (2 * OUT_SUB)), x, wa, wd, wo, final_norm_w)


def kernel(x, meta_tokens, norm_w, w_in, lambda_q1, lambda_k1, lambda_q2, lambda_k2, attn_norm_w,
           conv_w, a_log, dt_bias, dn_norm_w, w_branch_attn, w_branch_delta, w_out, final_norm_w):
    b = x.shape[0]
    assert x.shape == (b, SEQ, D_MODEL) and norm_w.shape[0] == 1
    layer = 0
    lam_init = 0.8 - 0.6 * math.exp(-0.3 * layer)

    head = jnp.concatenate([jnp.zeros((FRONT, D_MODEL), x.dtype), meta_tokens.astype(x.dtype)], axis=0)

    wi = w_in[layer]
    gate0 = 8 * D_MODEL
    q_scale = QK_DIM ** -0.5 * math.log2(math.e)
    col_scale = jnp.concatenate([jnp.full((1, D_MODEL), q_scale, F32), jnp.ones((1, gate0 - D_MODEL), F32)], axis=1)
    w_all = wi[:, :gate0].astype(BF16)
    w_tail = wi[:, gate0 + 2 * HEADS:].astype(BF16)
    w_gate = wi[:, gate0:gate0 + 2 * HEADS]
    w_gate_p = jnp.pad(w_gate, ((0, 0), (0, LANES - 2 * HEADS))).astype(BF16)
    w_gate_t = w_gate.T.astype(BF16)
    zeros8 = jnp.zeros((HEADS,), F32)
    gp_row = jnp.stack([jnp.pad(jnp.concatenate([zeros8, a_log[layer]]), (0, LANES - 2 * HEADS)),
                        jnp.pad(jnp.concatenate([zeros8, dt_bias[layer]]), (0, LANES - 2 * HEADS))])
    gp_col = jnp.stack([jnp.concatenate([zeros8, a_log[layer]]),
                        jnp.concatenate([zeros8, dt_bias[layer]])])
    gp_col = jnp.broadcast_to(gp_col[:, :, None], (2, 2 * HEADS, DN_T)).astype(F32)
    lam_params = jnp.stack([lambda_q1[layer], lambda_k1[layer], lambda_q2[layer], lambda_k2[layer]])

    proj2, gcol, grow = _inproj(head, x, norm_w[layer][None], w_all, w_tail, col_scale, w_gate_p, w_gate_t)
    proj3 = proj2.reshape(b, LP, N_MAIN)

    xa = _attention(proj3, lam_params, attn_norm_w[layer][None], lam_init)

    u, w, qd, kd, a, cd = _dn_prep(proj3, conv_w[layer], gcol.reshape(b, LP, LANES), grow, gp_row, gp_col)
    xd = _dn_scan(u, w, qd, kd, a, cd, proj3, dn_norm_w[layer][None])

    return _merge(xa, xd, proj3, x, w_branch_attn[layer].astype(BF16), w_branch_delta[layer].astype(BF16),
                  w_out[layer].astype(BF16), final_norm_w[None])
```

```python
import functools
import math

import jax
import jax.numpy as jnp
from jax import lax
from jax.experimental import pallas as pl
from jax.experimental.pallas import tpu as pltpu

D_MODEL = 1024
SEQ = 8192
N_META = 16
NORM_EPS = 1e-6
HEADS = 8
HEAD_DIM = 128
QK_DIM = 64
CONV_K = 4
CHUNK = 64
LANES = 128
LP = 8448
FRONT = LP - SEQ - N_META
X0 = FRONT + N_META
N_MAIN = 10 * D_MODEL
NEG = -1e30

C_AQ, C_AK, C_AV, C_AZ, C_DQ, C_DK, C_DV, C_DZ, C_GA, C_GD = range(10)

F32 = jnp.float32
BF16 = jnp.bfloat16


def _rms(x, w):
    return x * lax.rsqrt(jnp.mean(x * x, axis=-1, keepdims=True) + NORM_EPS) * w


def _dot(a, b):
    return jnp.dot(a, b, preferred_element_type=F32)


def _dot_nt(a, b):
    return lax.dot_general(a, b, (((1,), (1,)), ((), ())), preferred_element_type=F32)


def _dot_tn(a, b):
    return lax.dot_general(a, b, (((0,), (0,)), ((), ())), preferred_element_type=F32)


IN_SUB = 3
IN_TM = IN_SUB * X0
IN_TN = 2048
IN_TILES = LP // IN_TM
IN_FULL_TILES = 8 * D_MODEL // IN_TN


def _inproj_kernel(*refs):
    head_ref = refs[0]
    x_refs = refs[1:1 + IN_SUB]
    nw_ref, w_ref, wt_ref, cs_ref, wg_ref, p_ref, gcol_ref, grow_ref, hn_sc = refs[1 + IN_SUB:]
    j = pl.program_id(1)

    @pl.when(j == 0)
    def _():
        first_tile = pl.program_id(0) % IN_TILES == 0
        nw = nw_ref[...]
        for s, x_ref in enumerate(x_refs):
            h = x_ref[...]
            if s == 0:
                h = jnp.where(first_tile, head_ref[...], h)
            hn_sc[s * X0:(s + 1) * X0, :] = _rms(h, nw).astype(BF16)
        gates = _dot(hn_sc[...], wg_ref[...])
        gcol_ref[...] = gates
        grow_ref[...] = gates.T[:2 * HEADS, :]

    @pl.when(j < IN_FULL_TILES)
    def _():
        p_ref[...] = (_dot(hn_sc[...], w_ref[...]) * cs_ref[...]).astype(BF16)

    @pl.when(j >= IN_FULL_TILES)
    def _():
        p_ref[...] = _dot(hn_sc[...], wt_ref[...]).astype(BF16)


def _inproj(head, x, norm_w, w_all, w_tail, col_scale, w_gate):
    b = x.shape[0]
    rows = b * LP
    grid = (rows // IN_TM, N_MAIN // IN_TN)

    def xblock(s):
        return pl.BlockSpec((None, X0, D_MODEL),
                            lambda i, j: (i // IN_TILES, jnp.maximum((i % IN_TILES) * IN_SUB + s - 1, 0), 0))

    return pl.pallas_call(
        _inproj_kernel,
        grid=grid,
        in_specs=[pl.BlockSpec((X0, D_MODEL), lambda i, j: (0, 0), pipeline_mode=pl.Buffered(1))]
        + [xblock(s) for s in range(IN_SUB)] + [
            pl.BlockSpec((1, D_MODEL), lambda i, j: (0, 0)),
            pl.BlockSpec((D_MODEL, IN_TN), lambda i, j: (0, jnp.minimum(j, IN_FULL_TILES - 1))),
            pl.BlockSpec((D_MODEL, IN_TN), lambda i, j: (0, jnp.maximum(j - IN_FULL_TILES, 0))),
            pl.BlockSpec((1, IN_TN), lambda i, j: (0, jnp.minimum(j, IN_FULL_TILES - 1))),
            pl.BlockSpec((D_MODEL, LANES), lambda i, j: (0, 0)),
        ],
        out_specs=[
            pl.BlockSpec((IN_TM, IN_TN), lambda i, j: (i, j)),
            pl.BlockSpec((IN_TM, LANES), lambda i, j: (i, 0)),
            pl.BlockSpec((2 * HEADS, IN_TM), lambda i, j: (0, i)),
        ],
        out_shape=[
            jax.ShapeDtypeStruct((rows, N_MAIN), BF16),
            jax.ShapeDtypeStruct((rows, LANES), F32),
            jax.ShapeDtypeStruct((2 * HEADS, rows), F32),
        ],
        scratch_shapes=[pltpu.VMEM((IN_TM, D_MODEL), BF16)],
        compiler_params=pltpu.CompilerParams(
            dimension_semantics=("parallel", "arbitrary"),
            vmem_limit_bytes=56 * 1024 * 1024),
        name="inproj",
    )(head, *([x] * IN_SUB), norm_w, w_all, w_tail, col_scale, w_gate)


AT_SUB = 2
AT_TQ = AT_SUB * X0
AT_BIG = 6
AT_TILES = -(-LP // AT_TQ)
AT_LA = AT_TILES * AT_TQ
AT_SHIFT_MAX = 30.0


def _attn_kernel(*refs, lam_init):
    lam_ref, q_ref, k_ref, v_ref = refs[0:4]
    z_refs = refs[4:4 + AT_SUB]
    nw_ref = refs[4 + AT_SUB]
    o_ref = refs[5 + AT_SUB]
    qx_sc, kx_sc, vx_sc, e1_sc, e2_sc, flag_sc, m_sc, acc_sc = refs[6 + AT_SUB:]
    i = pl.program_id(2)
    tq = AT_TQ

    @pl.when((pl.program_id(0) == 0) & (pl.program_id(1) == 0) & (i == 0))
    def _():
        rowi = lax.broadcasted_iota(jnp.int32, (LP, LANES), 0)
        ln = lax.broadcasted_iota(jnp.int32, (LP, LANES), 1)
        ext = jnp.where(ln == 0, 1.0, jnp.where((ln == 1) & (rowi < FRONT), NEG, 0.0))
        kx_sc[0:LP, HEAD_DIM:] = ext.astype(BF16)
        vx_sc[0:LP, HEAD_DIM:] = jnp.ones((LP, HEAD_DIM), BF16)
        if AT_LA > LP:
            for ref in (qx_sc, kx_sc, vx_sc, e1_sc, e2_sc):
                ref[LP:, :] = jnp.zeros((AT_LA - LP, ref.shape[1]), BF16)

    @pl.when(i == 0)
    def _():
        k = k_ref[...]
        kx_sc[0:LP, 0:HEAD_DIM] = k
        vx_sc[0:LP, 0:HEAD_DIM] = v_ref[...]
        ln = lax.broadcasted_iota(jnp.int32, (LP, LANES), 1)
        first = ln < QK_DIM
        pad_lane = jnp.where(ln == 1, 1.0, 0.0)

        def half_norms(x):
            sq = x.astype(F32)
            sq = sq * sq
            return (jnp.sqrt(jnp.sum(jnp.where(first, sq, 0.0), axis=-1, keepdims=True)),
                    jnp.sqrt(jnp.sum(jnp.where(first, 0.0, sq), axis=-1, keepdims=True)))

        q = q_ref[...]
        qx_sc[0:LP, :] = q
        kn1, kn2 = half_norms(k)
        qn1, qn2 = half_norms(q)
        shift1 = qn1 * jnp.max(kn1, axis=0, keepdims=True)
        shift2 = qn2 * jnp.max(kn2, axis=0, keepdims=True)
        e1_sc[0:LP, :] = (pad_lane - jnp.where(ln == 0, shift1, 0.0)).astype(BF16)
        e2_sc[0:LP, :] = (pad_lane - jnp.where(ln == 0, shift2, 0.0)).astype(BF16)
        ok = jnp.maximum(jnp.max(shift1), jnp.max(shift2)) <= AT_SHIFT_MAX
        flag_sc[0] = ok.astype(jnp.int32)

    row0 = pl.multiple_of(i * tq, tq)
    q = qx_sc[pl.ds(row0, tq), :]
    first = lax.broadcasted_iota(jnp.int32, (tq, LANES), 1) < QK_DIM
    zero = jnp.zeros_like(q)
    use_shift = flag_sc[0] == 1
    q2 = jnp.concatenate([jnp.where(first, q, zero), jnp.where(first, zero, q)], axis=0)

    acc_sc[...] = jnp.zeros(acc_sc.shape, F32)

    def causal(s):
        r2 = lax.broadcasted_iota(jnp.int32, (2 * tq, tq), 0)
        row = jnp.where(r2 >= tq, r2 - tq, r2)
        col = lax.broadcasted_iota(jnp.int32, (2 * tq, tq), 1)
        return jnp.where(col <= row, s, NEG)

    def scores(qx, start, units):
        return _dot_nt(qx, kx_sc[pl.ds(start, units * tq), :])

    @pl.when(use_shift)
    def _():
        bias = jnp.concatenate([e1_sc[pl.ds(row0, tq), :], e2_sc[pl.ds(row0, tq), :]], axis=0)
        qx = jnp.concatenate([q2, bias], axis=1)

        def consume(start, units, s):
            p = jnp.exp2(s).astype(BF16)
            acc_sc[...] += _dot(p, vx_sc[pl.ds(start, units * tq), :])

        n_big = i // AT_BIG

        def big(j, c):
            start = pl.multiple_of(j * (AT_BIG * tq), AT_BIG * tq)
            consume(start, AT_BIG, scores(qx, start, AT_BIG))
            return c

        lax.fori_loop(0, n_big, big, 0)
        tail_units = i - n_big * AT_BIG + 1
        tail_start = pl.multiple_of(n_big * (AT_BIG * tq), AT_BIG * tq)
        for units in range(1, AT_BIG + 1):
            @pl.when(tail_units == units)
            def _():
                s = scores(qx, tail_start, units)
                s_diag = causal(s[:, (units - 1) * tq:])
                if units > 1:
                    s_diag = jnp.concatenate([s[:, :(units - 1) * tq], s_diag], axis=1)
                consume(tail_start, units, s_diag)

    @pl.when(jnp.logical_not(use_shift))
    def _():
        lane2 = lax.broadcasted_iota(jnp.int32, (2 * tq, LANES), 1)
        qx = jnp.concatenate([q2, jnp.where(lane2 == 1, 1.0, 0.0).astype(BF16)], axis=1)
        m_sc[...] = jnp.full(m_sc.shape, NEG, F32)

        def block(j, diag):
            start = pl.multiple_of(j * tq, tq)
            s = scores(qx, start, 1)
            if diag:
                s = causal(s)
            chunks = [s[:, c * LANES:(c + 1) * LANES] for c in range(tq // LANES)]
            mc = chunks[0]
            for ch in chunks[1:]:
                mc = jnp.maximum(mc, ch)
            m_prev = m_sc[...]
            m_new = jnp.maximum(m_prev, jnp.max(mc, axis=-1, keepdims=True))
            alpha = jnp.exp2(m_prev - m_new)
            p = jnp.concatenate([jnp.exp2(ch - m_new) for ch in chunks], axis=1).astype(BF16)
            pv = _dot(p, vx_sc[pl.ds(start, tq), :])
            acc_sc[...] = jnp.concatenate([alpha, alpha], axis=1) * acc_sc[...] + pv
            m_sc[...] = m_new

        def full(j, c):
            block(j, False)
            return c

        lax.fori_loop(0, i, full, 0)
        block(i, True)

    lp = lam_ref[...]
    lam = (jnp.exp(jnp.sum(lp[0:1] * lp[1:2], axis=-1, keepdims=True))
           - jnp.exp(jnp.sum(lp[2:3] * lp[3:4], axis=-1, keepdims=True)) + lam_init)
    acc = acc_sc[...]
    o_all = acc[:, :HEAD_DIM] / jnp.maximum(acc[:, HEAD_DIM:], 1e-37)
    o = o_all[:tq] - lam * o_all[tq:]
    o = _rms(o, nw_ref[...]) * (1.0 - lam_init)
    z = jnp.concatenate([r[...] for r in z_refs], axis=0).astype(F32)
    o_ref[...] = (o * (z * jax.nn.sigmoid(z))).astype(BF16)


def _attention(proj3, lam_params, attn_norm_w, lam_init):
    b = proj3.shape[0]
    hb = D_MODEL // HEAD_DIM
    nblk = LP // X0
    kern = functools.partial(_attn_kernel, lam_init=lam_init)

    def sub(s, col):
        return pl.BlockSpec((None, X0, HEAD_DIM),
                            lambda b_, h, i: (b_, jnp.minimum(AT_SUB * i + s, nblk - 1), col * hb + h))

    return pl.pallas_call(
        kern,
        grid=(b, HEADS, AT_TILES),
        in_specs=[
            pl.BlockSpec((4, QK_DIM), lambda b_, h, i: (0, 0)),
            pl.BlockSpec((None, LP, HEAD_DIM), lambda b_, h, i: (b_, 0, C_AQ * hb + h)),
            pl.BlockSpec((None, LP, HEAD_DIM), lambda b_, h, i: (b_, 0, C_AK * hb + h)),
            pl.BlockSpec((None, LP, HEAD_DIM), lambda b_, h, i: (b_, 0, C_AV * hb + h)),
        ] + [sub(s, C_AZ) for s in range(AT_SUB)] + [
            pl.BlockSpec((1, HEAD_DIM), lambda b_, h, i: (0, 0)),
        ],
        out_specs=pl.BlockSpec((None, AT_TQ, HEAD_DIM), lambda b_, h, i: (b_, i, h)),
        out_shape=jax.ShapeDtypeStruct((b, LP, D_MODEL), BF16),
        scratch_shapes=[
            pltpu.VMEM((AT_LA, HEAD_DIM), BF16),
            pltpu.VMEM((AT_LA, 2 * HEAD_DIM), BF16),
            pltpu.VMEM((AT_LA, 2 * HEAD_DIM), BF16),
            pltpu.VMEM((AT_LA, LANES), BF16),
            pltpu.VMEM((AT_LA, LANES), BF16),
            pltpu.SMEM((1,), jnp.int32),
            pltpu.VMEM((2 * AT_TQ, LANES), F32),
            pltpu.VMEM((2 * AT_TQ, 2 * HEAD_DIM), F32),
        ],
        compiler_params=pltpu.CompilerParams(
            dimension_semantics=("arbitrary", "arbitrary", "arbitrary"),
            vmem_limit_bytes=56 * 1024 * 1024),
        name="diff_attn",
    )(lam_params, proj3, proj3, proj3, *([proj3] * AT_SUB), attn_norm_w)


DN_T = 2 * CHUNK
HALO = 16
DN_GROUP = 8


def _shift_matrix():
    r = jnp.arange((CONV_K - 1) * DN_T)
    src = HALO + r % DN_T - (r // DN_T + 1)
    return (src[:, None] == jnp.arange(HALO + DN_T)[None, :]).astype(BF16)


def _split3(x):
    hi = x.astype(BF16)
    r1 = x - hi.astype(F32)
    mid = r1.astype(BF16)
    lo = (r1 - mid.astype(F32)).astype(BF16)
    return hi, mid, lo


def _dn_prep_kernel(dq_ref, dk_ref, dv_ref, hq_ref, hk_ref, hv_ref, cw_ref, sh_ref, gcol_ref, grow_ref,
                    gpr_ref, gpc_ref, u_ref, w_ref, qd_ref, kd_ref, a_ref, cd_ref):
    i = pl.program_id(1)
    t = DN_T
    width = D_MODEL
    cw = cw_ref[...]
    shifter = sh_ref[...]
    ys = []
    for idx, (t_ref, h_ref) in enumerate(((dq_ref, hq_ref), (dk_ref, hk_ref), (dv_ref, hv_ref))):
        halo = h_ref[...]
        halo = jnp.where(i == 0, jnp.zeros_like(halo), halo)
        x = t_ref[...]
        shifted = _dot(shifter, jnp.concatenate([halo, x], axis=0))
        cwi = cw[:, idx * width:(idx + 1) * width]
        yi = cwi[CONV_K - 1:CONV_K] * x.astype(F32)
        for s in range(1, CONV_K):
            yi = yi + cwi[CONV_K - 1 - s:CONV_K - s] * shifted[(s - 1) * t:s * t]
        ys.append(yi * jax.nn.sigmoid(yi))
    y = jnp.concatenate(ys, axis=1)

    r = lax.broadcasted_iota(jnp.int32, (t, t), 0)
    c = lax.broadcasted_iota(jnp.int32, (t, t), 1)
    same = (r >= CHUNK) == (c >= CHUNK)
    incl = same & (c <= r)
    strict = same & (c < r)
    tri = jnp.where(incl, 1.0, 0.0).astype(BF16)
    tri_t = jnp.where(same & (r <= c), 1.0, 0.0).astype(BF16)
    blk = jnp.where(same, 1.0, 0.0).astype(BF16)
    eye = jnp.where(r == c, 1.0, 0.0).astype(F32)

    lane = lax.broadcasted_iota(jnp.int32, (t, LANES), 1)
    rowi = lax.broadcasted_iota(jnp.int32, (t, LANES), 0) + i * t
    gcol = gcol_ref[...]
    gpr = gpr_ref[...]
    is_g = (lane >= HEADS) & (lane < 2 * HEADS)
    live_c = rowi >= FRONT
    beta_c = jnp.where(live_c, jax.nn.sigmoid(gcol), 0.0)
    xg = gcol + gpr[1:2]
    sp = jnp.maximum(xg, 0.0) + jnp.log1p(jnp.exp(-jnp.abs(xg)))
    g_c = jnp.where(is_g & live_c, -jnp.exp(gpr[0:1]) * sp, 0.0)
    g3 = _split3(g_c)
    gc_c = _dot(tri, g3[0]) + _dot(tri, g3[1]) + _dot(tri, g3[2])
    gl_c = _dot(blk, g3[0]) + _dot(blk, g3[1]) + _dot(blk, g3[2])
    sub = lax.broadcasted_iota(jnp.int32, (2 * HEADS, t), 0)
    coli = lax.broadcasted_iota(jnp.int32, (2 * HEADS, t), 1) + i * t
    xr = grow_ref[...] + gpc_ref[1]
    spr = jnp.maximum(xr, 0.0) + jnp.log1p(jnp.exp(-jnp.abs(xr)))
    g_r = jnp.where((sub >= HEADS) & (coli >= FRONT), -jnp.exp(gpc_ref[0]) * spr, 0.0)
    gr3 = _split3(g_r)
    gc_r = _dot(gr3[0], tri_t) + _dot(gr3[1], tri_t) + _dot(gr3[2], tri_t)

    cd_ref[...] = jnp.exp(gl_c)

    def group(hs):
        sls = {h: slice(h * HEAD_DIM, (h + 1) * HEAD_DIM) for h in hs}
        q = {h: y[:, h * HEAD_DIM:(h + 1) * HEAD_DIM] for h in hs}
        k = {h: y[:, width + h * HEAD_DIM:width + (h + 1) * HEAD_DIM] for h in hs}
        v = {h: y[:, 2 * width + h * HEAD_DIM:2 * width + (h + 1) * HEAD_DIM] for h in hs}
        q = {h: x * lax.rsqrt(jnp.sum(x * x, axis=-1, keepdims=True) + 1e-6) * (HEAD_DIM ** -0.5)
             for h, x in q.items()}
        k = {h: x * lax.rsqrt(jnp.sum(x * x, axis=-1, keepdims=True) + 1e-6) for h, x in k.items()}
        beta = {h: beta_c[:, h:h + 1] for h in hs}
        gcc = {h: gc_c[:, HEADS + h:HEADS + h + 1] for h in hs}
        glc = {h: gl_c[:, HEADS + h:HEADS + h + 1] for h in hs}
        gcr = {h: gc_r[HEADS + h:HEADS + h + 1, :] for h in hs}
        dec = {h: jnp.where(incl, jnp.exp(jnp.where(incl, gcc[h] - gcr[h], 0.0)), 0.0) for h in hs}
        kb = {h: k[h] * beta[h] for h in hs}
        kt = {h: k[h].T.astype(BF16) for h in hs}
        m = {h: jnp.where(strict, _dot(kb[h].astype(BF16), kt[h]) * dec[h], 0.0) for h in hs}
        tinv = {h: eye - m[h] for h in hs}
        pwb = {h: m[h].astype(BF16) for h in hs}
        for _ in range(5):
            pw = {h: _dot(pwb[h], pwb[h]) for h in hs}
            pwb = {h: pw[h].astype(BF16) for h in hs}
            tinv = {h: tinv[h] + _dot(tinv[h].astype(BF16), pwb[h]) for h in hs}
        egc = {h: jnp.exp(gcc[h]) for h in hs}
        rhs = {h: jnp.concatenate([v[h] * beta[h], kb[h] * egc[h]], axis=1).astype(BF16) for h in hs}
        uw = {h: _dot(tinv[h].astype(BF16), rhs[h]) for h in hs}
        a_full = {h: _dot(q[h].astype(BF16), kt[h]) * dec[h] for h in hs}
        for h in hs:
            u_ref[:, sls[h]] = uw[h][:, :HEAD_DIM].astype(BF16)
            w_ref[:, sls[h]] = uw[h][:, HEAD_DIM:].astype(BF16)
            a_cmp = a_full[h] + pltpu.roll(a_full[h], CHUNK, axis=1)
            a_ref[:, h * CHUNK:(h + 1) * CHUNK] = a_cmp[:, :CHUNK].astype(BF16)
            qd_ref[:, sls[h]] = (q[h] * egc[h]).astype(BF16)
            kd_ref[:, sls[h]] = (k[h] * jnp.exp(glc[h] - gcc[h])).astype(BF16)

    for g in range(HEADS // DN_GROUP):
        group(range(g * DN_GROUP, (g + 1) * DN_GROUP))


def _dn_prep(proj3, conv_w2, gcol3, grow, gp_row, gp_col):
    b = proj3.shape[0]
    t = DN_T
    nt = LP // t
    hpb = t // HALO

    def tile(cb):
        return pl.BlockSpec((None, t, D_MODEL), lambda b_, i: (b_, i, cb))

    def halo(cb):
        return pl.BlockSpec((None, HALO, D_MODEL), lambda b_, i: (b_, jnp.maximum(i * hpb - 1, 0), cb))

    full = pl.BlockSpec((None, t, D_MODEL), lambda b_, i: (b_, i, 0))
    return pl.pallas_call(
        _dn_prep_kernel,
        grid=(b, nt),
        in_specs=[
            tile(C_DQ), tile(C_DK), tile(C_DV), halo(C_DQ), halo(C_DK), halo(C_DV),
            pl.BlockSpec((CONV_K, 3 * D_MODEL), lambda b_, i: (0, 0)),
            pl.BlockSpec(((CONV_K - 1) * t, HALO + t), lambda b_, i: (0, 0)),
            pl.BlockSpec((None, t, LANES), lambda b_, i: (b_, i, 0)),
            pl.BlockSpec((2 * HEADS, t), lambda b_, i: (0, b_ * nt + i)),
            pl.BlockSpec((2, LANES), lambda b_, i: (0, 0)),
            pl.BlockSpec((2, 2 * HEADS, t), lambda b_, i: (0, 0, 0)),
        ],
        out_specs=[
            full, full, full, full,
            pl.BlockSpec((None, t, HEADS * CHUNK), lambda b_, i: (b_, i, 0)),
            pl.BlockSpec((None, t, LANES), lambda b_, i: (b_, i, 0)),
        ],
        out_shape=[
            jax.ShapeDtypeStruct((b, LP, D_MODEL), BF16),
            jax.ShapeDtypeStruct((b, LP, D_MODEL), BF16),
            jax.ShapeDtypeStruct((b, LP, D_MODEL), BF16),
            jax.ShapeDtypeStruct((b, LP, D_MODEL), BF16),
            jax.ShapeDtypeStruct((b, LP, HEADS * CHUNK), BF16),
            jax.ShapeDtypeStruct((b, LP, LANES), F32),
        ],
        compiler_params=pltpu.CompilerParams(
            dimension_semantics=("parallel", "parallel"),
            vmem_limit_bytes=40 * 1024 * 1024),
        name="dn_prep",
    )(proj3, proj3, proj3, proj3, proj3, proj3, conv_w2, _shift_matrix(), gcol3, grow, gp_row, gp_col)


SC_T = 4 * CHUNK


def _dn_scan_kernel(u_ref, w_ref, qd_ref, kd_ref, a_ref, cd_ref, z_ref, nw_ref, o_ref, s_sc):
    @pl.when(pl.program_id(1) == 0)
    def _():
        s_sc[...] = jnp.zeros(s_sc.shape, F32)

    nw = nw_ref[...]
    hs = range(HEADS)
    sls = [slice(h * HEAD_DIM, (h + 1) * HEAD_DIM) for h in hs]
    s = [s_sc[h] for h in hs]
    for cix in range(SC_T // CHUNK):
        rs = slice(cix * CHUNK, (cix + 1) * CHUNK)
        sb = [x.astype(BF16) for x in s]
        ws_qs = [_dot(jnp.concatenate([w_ref[rs, sls[h]], qd_ref[rs, sls[h]]], axis=0), sb[h]) for h in hs]
        vb = [(u_ref[rs, sls[h]].astype(F32) - ws_qs[h][:CHUNK]).astype(BF16) for h in hs]
        o = [ws_qs[h][CHUNK:] + _dot(a_ref[rs, h * CHUNK:(h + 1) * CHUNK], vb[h]) for h in hs]
        cd = [cd_ref[cix * CHUNK:cix * CHUNK + 1, HEADS + h:HEADS + h + 1] for h in hs]
        s = [s[h] * cd[h] + _dot_tn(kd_ref[rs, sls[h]], vb[h]) for h in hs]
        for h in hs:
            z = z_ref[rs, sls[h]].astype(F32)
            o_ref[rs, sls[h]] = (_rms(o[h], nw) * (z * jax.nn.sigmoid(z))).astype(BF16)
    for h in hs:
        s_sc[h] = s[h]


def _dn_scan(u, w, qd, kd, a, cd, proj3, dn_norm_w):
    b = u.shape[0]
    t = SC_T
    full = pl.BlockSpec((None, t, D_MODEL), lambda b_, i: (b_, i, 0))
    return pl.pallas_call(
        _dn_scan_kernel,
        grid=(b, LP // t),
        in_specs=[
            full, full, full, full,
            pl.BlockSpec((None, t, HEADS * CHUNK), lambda b_, i: (b_, i, 0)),
            pl.BlockSpec((None, t, LANES), lambda b_, i: (b_, i, 0)),
            pl.BlockSpec((None, t, D_MODEL), lambda b_, i: (b_, i, C_DZ)),
            pl.BlockSpec((1, HEAD_DIM), lambda b_, i: (0, 0)),
        ],
        out_specs=full,
        out_shape=jax.ShapeDtypeStruct((b, LP, D_MODEL), BF16),
        scratch_shapes=[pltpu.VMEM((HEADS, HEAD_DIM, HEAD_DIM), F32)],
        compiler_params=pltpu.CompilerParams(
            dimension_semantics=("parallel", "arbitrary"),
            vmem_limit_bytes=40 * 1024 * 1024),
        name="dn_scan",
    )(u, w, qd, kd, a, cd, proj3, dn_norm_w)


OUT_SUB = 2


def _merge_kernel(*refs):
    n = OUT_SUB
    xa_refs, xd_refs, ga_refs, gd_refs = refs[0:n], refs[n:2 * n], refs[2 * n:3 * n], refs[3 * n:4 * n]
    x_ref, wa_ref, wd_ref, wo_ref, nw_ref, o_ref = refs[4 * n:]

    def rows(rs):
        return jnp.concatenate([r[...] for r in rs], axis=0)

    ya = _dot(rows(xa_refs), wa_ref[...])
    yd = _dot(rows(xd_refs), wd_ref[...])
    merged = (jax.nn.sigmoid(rows(ga_refs).astype(F32)) * ya
              + jax.nn.sigmoid(rows(gd_refs).astype(F32)) * yd)
    out = x_ref[...] + _dot(merged.astype(BF16), wo_ref[...])
    o_ref[...] = _rms(out, nw_ref[...])


def _merge(xa, xd, proj3, x, wa, wd, wo, final_norm_w):
    b = x.shape[0]
    tm = OUT_SUB * X0

    def padded(cb):
        return [pl.BlockSpec((None, X0, D_MODEL), lambda b_, i, s=s: (b_, OUT_SUB * i + 1 + s, cb))
                for s in range(OUT_SUB)]

    tok = pl.BlockSpec((None, tm, D_MODEL), lambda b_, i: (b_, i, 0))
    wspec = pl.BlockSpec((D_MODEL, D_MODEL), lambda b_, i: (0, 0))
    return pl.pallas_call(
        _merge_kernel,
        grid=(b, SEQ // tm),
        in_specs=padded(0) + padded(0) + padded(C_GA) + padded(C_GD) + [
            tok, wspec, wspec, wspec,
            pl.BlockSpec((1, D_MODEL), lambda b_, i: (0, 0)),
        ],
        out_specs=tok,
        out_shape=jax.ShapeDtypeStruct((b, SEQ, D_MODEL), F32),
        compiler_params=pltpu.CompilerParams(
            dimension_semantics=("parallel", "parallel"),
            vmem_limit_bytes=48 * 1024 * 1024),
        name="merge_out",
    )(*([xa] * OUT_SUB + [xd] * OUT_SUB + [proj3] * (2 * OUT_SUB)), x, wa, wd, wo, final_norm_w)


def kernel(x, meta_tokens, norm_w, w_in, lambda_q1, lambda_k1, lambda_q2, lambda_k2, attn_norm_w,
           conv_w, a_log, dt_bias, dn_norm_w, w_branch_attn, w_branch_delta, w_out, final_norm_w):
    b = x.shape[0]
    assert x.shape == (b, SEQ, D_MODEL) and norm_w.shape[0] == 1
    layer = 0
    lam_init = 0.8 - 0.6 * math.exp(-0.3 * layer)

    head = jnp.concatenate([jnp.zeros((FRONT, D_MODEL), x.dtype), meta_tokens.astype(x.dtype)], axis=0)

    wi = w_in[layer]
    gate0 = 8 * D_MODEL
    q_scale = QK_DIM ** -0.5 * math.log2(math.e)
    col_scale = jnp.concatenate([jnp.full((1, D_MODEL), q_scale, F32), jnp.ones((1, gate0 - D_MODEL), F32)], axis=1)
    w_all = wi.astype(BF16)
    w_tail = wi[:, gate0 + 2 * HEADS:].astype(BF16)
    w_gate = wi[:, gate0:gate0 + 2 * HEADS]
    w_gate_p = jnp.pad(w_gate, ((0, 0), (0, LANES - 2 * HEADS))).astype(BF16)
    zeros8 = jnp.zeros((HEADS,), F32)
    gp_row = jnp.stack([jnp.pad(jnp.concatenate([zeros8, a_log[layer]]), (0, LANES - 2 * HEADS)),
                        jnp.pad(jnp.concatenate([zeros8, dt_bias[layer]]), (0, LANES - 2 * HEADS))])
    gp_col = jnp.stack([jnp.concatenate([zeros8, a_log[layer]]),
                        jnp.concatenate([zeros8, dt_bias[layer]])])
    gp_col = jnp.broadcast_to(gp_col[:, :, None], (2, 2 * HEADS, DN_T)).astype(F32)
    lam_params = jnp.stack([lambda_q1[layer], lambda_k1[layer], lambda_q2[layer], lambda_k2[layer]])

    proj2, gcol, grow = _inproj(head, x, norm_w[layer][None], w_all, w_tail, col_scale, w_gate_p)
    proj3 = proj2.reshape(b, LP, N_MAIN)

    xa = _attention(proj3, lam_params, attn_norm_w[layer][None], lam_init)

    u, w, qd, kd, a, cd = _dn_prep(proj3, conv_w[layer], gcol.reshape(b, LP, LANES), grow, gp_row, gp_col)
    xd = _dn_scan(u, w, qd, kd, a, cd, proj3, dn_norm_w[layer][None])

    return _merge(xa, xd, proj3, x, w_branch_attn[layer].astype(BF16), w_branch_delta[layer].astype(BF16),
                  w_out[layer].astype(BF16), final_norm_w[None])
```

```python
import functools
import math

import jax
import jax.numpy as jnp
from jax import lax
from jax.experimental import pallas as pl
from jax.experimental.pallas import tpu as pltpu

D_MODEL = 1024
SEQ = 8192
N_META = 16
NORM_EPS = 1e-6
HEADS = 8
HEAD_DIM = 128
QK_DIM = 64
CONV_K = 4
CHUNK = 64
LANES = 128
LP = 8448
FRONT = LP - SEQ - N_META
X0 = FRONT + N_META
N_MAIN = 10 * D_MODEL
NEG = -1e30

C_AQ, C_AK, C_AV, C_AZ, C_DQ, C_DK, C_DV, C_DZ, C_GA, C_GD = range(10)

F32 = jnp.float32
BF16 = jnp.bfloat16


def _rms(x, w):
    return x * lax.rsqrt(jnp.mean(x * x, axis=-1, keepdims=True) + NORM_EPS) * w


def _dot(a, b):
    return jnp.dot(a, b, preferred_element_type=F32)


def _dot_nt(a, b):
    return lax.dot_general(a, b, (((1,), (1,)), ((), ())), preferred_element_type=F32)


def _dot_tn(a, b):
    return lax.dot_general(a, b, (((0,), (0,)), ((), ())), preferred_element_type=F32)


IN_SUB = 3
IN_TM = IN_SUB * X0
IN_TN = 2048
IN_TILES = LP // IN_TM
IN_FULL_TILES = 8 * D_MODEL // IN_TN


def _inproj_kernel(*refs):
    head_ref = refs[0]
    x_refs = refs[1:1 + IN_SUB]
    nw_ref, w_ref, wt_ref, cs_ref, wg_ref, p_ref, gcol_ref, grow_ref, hn_sc = refs[1 + IN_SUB:]
    j = pl.program_id(1)

    @pl.when(j == 0)
    def _():
        first_tile = pl.program_id(0) % IN_TILES == 0
        nw = nw_ref[...]
        for s, x_ref in enumerate(x_refs):
            h = x_ref[...]
            if s == 0:
                h = jnp.where(first_tile, head_ref[...], h)
            hn_sc[s * X0:(s + 1) * X0, :] = _rms(h, nw).astype(BF16)
        gates = _dot(hn_sc[...], wg_ref[...])
        gcol_ref[...] = gates
        grow_ref[...] = gates.T[:2 * HEADS, :]

    @pl.when(j < IN_FULL_TILES)
    def _():
        p_ref[...] = (_dot(hn_sc[...], w_ref[...]) * cs_ref[...]).astype(BF16)

    @pl.when(j >= IN_FULL_TILES)
    def _():
        p_ref[...] = _dot(hn_sc[...], wt_ref[...]).astype(BF16)


def _inproj(head, x, norm_w, w_all, w_tail, col_scale, w_gate):
    b = x.shape[0]
    rows = b * LP
    grid = (rows // IN_TM, N_MAIN // IN_TN)

    def xblock(s):
        return pl.BlockSpec((None, X0, D_MODEL),
                            lambda i, j: (i // IN_TILES, jnp.maximum((i % IN_TILES) * IN_SUB + s - 1, 0), 0))

    return pl.pallas_call(
        _inproj_kernel,
        grid=grid,
        in_specs=[pl.BlockSpec((X0, D_MODEL), lambda i, j: (0, 0), pipeline_mode=pl.Buffered(1))]
        + [xblock(s) for s in range(IN_SUB)] + [
            pl.BlockSpec((1, D_MODEL), lambda i, j: (0, 0)),
            pl.BlockSpec((D_MODEL, IN_TN), lambda i, j: (0, jnp.minimum(j, IN_FULL_TILES - 1))),
            pl.BlockSpec((D_MODEL, IN_TN), lambda i, j: (0, jnp.maximum(j - IN_FULL_TILES, 0))),
            pl.BlockSpec((1, IN_TN), lambda i, j: (0, jnp.minimum(j, IN_FULL_TILES - 1))),
            pl.BlockSpec((D_MODEL, LANES), lambda i, j: (0, 0)),
        ],
        out_specs=[
            pl.BlockSpec((IN_TM, IN_TN), lambda i, j: (i, j)),
            pl.BlockSpec((IN_TM, LANES), lambda i, j: (i, 0)),
            pl.BlockSpec((2 * HEADS, IN_TM), lambda i, j: (0, i)),
        ],
        out_shape=[
            jax.ShapeDtypeStruct((rows, N_MAIN), BF16),
            jax.ShapeDtypeStruct((rows, LANES), F32),
            jax.ShapeDtypeStruct((2 * HEADS, rows), F32),
        ],
        scratch_shapes=[pltpu.VMEM((IN_TM, D_MODEL), BF16)],
        compiler_params=pltpu.CompilerParams(
            dimension_semantics=("parallel", "arbitrary"),
            vmem_limit_bytes=56 * 1024 * 1024),
        name="inproj",
    )(head, *([x] * IN_SUB), norm_w, w_all, w_tail, col_scale, w_gate)


AT_SUB = 2
AT_TQ = AT_SUB * X0
AT_BIG = 6
AT_TILES = -(-LP // AT_TQ)
AT_LA = AT_TILES * AT_TQ
AT_SHIFT_MAX = 30.0


def _attn_kernel(*refs, lam_init):
    lam_ref, q_ref, k_ref, v_ref = refs[0:4]
    z_refs = refs[4:4 + AT_SUB]
    nw_ref = refs[4 + AT_SUB]
    o_ref = refs[5 + AT_SUB]
    qx_sc, kx_sc, vx_sc, e1_sc, e2_sc, flag_sc, m_sc, acc_sc = refs[6 + AT_SUB:]
    i = pl.program_id(2)
    tq = AT_TQ

    @pl.when((pl.program_id(0) == 0) & (pl.program_id(1) == 0) & (i == 0))
    def _():
        rowi = lax.broadcasted_iota(jnp.int32, (LP, LANES), 0)
        ln = lax.broadcasted_iota(jnp.int32, (LP, LANES), 1)
        ext = jnp.where(ln == 0, 1.0, jnp.where((ln == 1) & (rowi < FRONT), NEG, 0.0))
        kx_sc[0:LP, HEAD_DIM:] = ext.astype(BF16)
        vx_sc[0:LP, HEAD_DIM:] = jnp.ones((LP, HEAD_DIM), BF16)
        if AT_LA > LP:
            for ref in (qx_sc, kx_sc, vx_sc, e1_sc, e2_sc):
                ref[LP:, :] = jnp.zeros((AT_LA - LP, ref.shape[1]), BF16)

    @pl.when(i == 0)
    def _():
        k = k_ref[...]
        kx_sc[0:LP, 0:HEAD_DIM] = k
        vx_sc[0:LP, 0:HEAD_DIM] = v_ref[...]
        ln = lax.broadcasted_iota(jnp.int32, (LP, LANES), 1)
        first = ln < QK_DIM
        pad_lane = jnp.where(ln == 1, 1.0, 0.0)

        def half_norms(x):
            sq = x.astype(F32)
            sq = sq * sq
            return (jnp.sqrt(jnp.sum(jnp.where(first, sq, 0.0), axis=-1, keepdims=True)),
                    jnp.sqrt(jnp.sum(jnp.where(first, 0.0, sq), axis=-1, keepdims=True)))

        q = q_ref[...]
        qx_sc[0:LP, :] = q
        kn1, kn2 = half_norms(k)
        qn1, qn2 = half_norms(q)
        shift1 = qn1 * jnp.max(kn1, axis=0, keepdims=True)
        shift2 = qn2 * jnp.max(kn2, axis=0, keepdims=True)
        e1_sc[0:LP, :] = (pad_lane - jnp.where(ln == 0, shift1, 0.0)).astype(BF16)
        e2_sc[0:LP, :] = (pad_lane - jnp.where(ln == 0, shift2, 0.0)).astype(BF16)
        ok = jnp.maximum(jnp.max(shift1), jnp.max(shift2)) <= AT_SHIFT_MAX
        flag_sc[0] = ok.astype(jnp.int32)

    row0 = pl.multiple_of(i * tq, tq)
    q = qx_sc[pl.ds(row0, tq), :]
    first = lax.broadcasted_iota(jnp.int32, (tq, LANES), 1) < QK_DIM
    zero = jnp.zeros_like(q)
    use_shift = flag_sc[0] == 1
    q2 = jnp.concatenate([jnp.where(first, q, zero), jnp.where(first, zero, q)], axis=0)

    acc_sc[...] = jnp.zeros(acc_sc.shape, F32)

    def causal(s):
        r2 = lax.broadcasted_iota(jnp.int32, (2 * tq, tq), 0)
        row = jnp.where(r2 >= tq, r2 - tq, r2)
        col = lax.broadcasted_iota(jnp.int32, (2 * tq, tq), 1)
        return jnp.where(col <= row, s, NEG)

    def scores(qx, start, units):
        return _dot_nt(qx, kx_sc[pl.ds(start, units * tq), :])

    @pl.when(use_shift)
    def _():
        bias = jnp.concatenate([e1_sc[pl.ds(row0, tq), :], e2_sc[pl.ds(row0, tq), :]], axis=0)
        qx = jnp.concatenate([q2, bias], axis=1)

        def consume(start, units, s):
            p = jnp.exp2(s).astype(BF16)
            acc_sc[...] += _dot(p, vx_sc[pl.ds(start, units * tq), :])

        n_big = i // AT_BIG

        def big(j, c):
            start = pl.multiple_of(j * (AT_BIG * tq), AT_BIG * tq)
            consume(start, AT_BIG, scores(qx, start, AT_BIG))
            return c

        lax.fori_loop(0, n_big, big, 0)
        tail_units = i - n_big * AT_BIG + 1
        tail_start = pl.multiple_of(n_big * (AT_BIG * tq), AT_BIG * tq)
        half = tq // 2
        qx_low = jnp.concatenate([qx[half:tq], qx[tq + half:]], axis=0)

        def tri(s, rows):
            r2 = lax.broadcasted_iota(jnp.int32, (2 * rows, half), 0)
            row = jnp.where(r2 >= rows, r2 - rows, r2)
            col = lax.broadcasted_iota(jnp.int32, (2 * rows, half), 1)
            return jnp.where(col <= row, s, NEG)

        for units in range(1, AT_BIG + 1):
            @pl.when(tail_units == units)
            def _():
                main = units * tq - half
                s = _dot_nt(qx, kx_sc[pl.ds(tail_start, main), :])
                s_edge = tri(s[:, main - half:], tq)
                if main > half:
                    s_edge = jnp.concatenate([s[:, :main - half], s_edge], axis=1)
                p = jnp.exp2(s_edge).astype(BF16)
                acc_sc[...] += _dot(p, vx_sc[pl.ds(tail_start, main), :])
                last = pl.multiple_of(tail_start + main, half)
                s_low = tri(_dot_nt(qx_low, kx_sc[pl.ds(last, half), :]), half)
                pv = _dot(jnp.exp2(s_low).astype(BF16), vx_sc[pl.ds(last, half), :])
                acc_sc[half:tq, :] += pv[:half]
                acc_sc[tq + half:, :] += pv[half:]

    @pl.when(jnp.logical_not(use_shift))
    def _():
        lane2 = lax.broadcasted_iota(jnp.int32, (2 * tq, LANES), 1)
        qx = jnp.concatenate([q2, jnp.where(lane2 == 1, 1.0, 0.0).astype(BF16)], axis=1)
        m_sc[...] = jnp.full(m_sc.shape, NEG, F32)

        def block(j, diag):
            start = pl.multiple_of(j * tq, tq)
            s = scores(qx, start, 1)
            if diag:
                s = causal(s)
            chunks = [s[:, c * LANES:(c + 1) * LANES] for c in range(tq // LANES)]
            mc = chunks[0]
            for ch in chunks[1:]:
                mc = jnp.maximum(mc, ch)
            m_prev = m_sc[...]
            m_new = jnp.maximum(m_prev, jnp.max(mc, axis=-1, keepdims=True))
            alpha = jnp.exp2(m_prev - m_new)
            p = jnp.concatenate([jnp.exp2(ch - m_new) for ch in chunks], axis=1).astype(BF16)
            pv = _dot(p, vx_sc[pl.ds(start, tq), :])
            acc_sc[...] = jnp.concatenate([alpha, alpha], axis=1) * acc_sc[...] + pv
            m_sc[...] = m_new

        def full(j, c):
            block(j, False)
            return c

        lax.fori_loop(0, i, full, 0)
        block(i, True)

    lp = lam_ref[...]
    lam = (jnp.exp(jnp.sum(lp[0:1] * lp[1:2], axis=-1, keepdims=True))
           - jnp.exp(jnp.sum(lp[2:3] * lp[3:4], axis=-1, keepdims=True)) + lam_init)
    acc = acc_sc[...]
    o_all = acc[:, :HEAD_DIM] / jnp.maximum(acc[:, HEAD_DIM:], 1e-37)
    o = o_all[:tq] - lam * o_all[tq:]
    o = _rms(o, nw_ref[...]) * (1.0 - lam_init)
    z = jnp.concatenate([r[...] for r in z_refs], axis=0).astype(F32)
    o_ref[...] = (o * (z * jax.nn.sigmoid(z))).astype(BF16)


def _attention(proj3, lam_params, attn_norm_w, lam_init):
    b = proj3.shape[0]
    hb = D_MODEL // HEAD_DIM
    nblk = LP // X0
    kern = functools.partial(_attn_kernel, lam_init=lam_init)

    def sub(s, col):
        return pl.BlockSpec((None, X0, HEAD_DIM),
                            lambda b_, h, i: (b_, jnp.minimum(AT_SUB * i + s, nblk - 1), col * hb + h))

    return pl.pallas_call(
        kern,
        grid=(b, HEADS, AT_TILES),
        in_specs=[
            pl.BlockSpec((4, QK_DIM), lambda b_, h, i: (0, 0)),
            pl.BlockSpec((None, LP, HEAD_DIM), lambda b_, h, i: (b_, 0, C_AQ * hb + h)),
            pl.BlockSpec((None, LP, HEAD_DIM), lambda b_, h, i: (b_, 0, C_AK * hb + h)),
            pl.BlockSpec((None, LP, HEAD_DIM), lambda b_, h, i: (b_, 0, C_AV * hb + h)),
        ] + [sub(s, C_AZ) for s in range(AT_SUB)] + [
            pl.BlockSpec((1, HEAD_DIM), lambda b_, h, i: (0, 0)),
        ],
        out_specs=pl.BlockSpec((None, AT_TQ, HEAD_DIM), lambda b_, h, i: (b_, i, h)),
        out_shape=jax.ShapeDtypeStruct((b, LP, D_MODEL), BF16),
        scratch_shapes=[
            pltpu.VMEM((AT_LA, HEAD_DIM), BF16),
            pltpu.VMEM((AT_LA, 2 * HEAD_DIM), BF16),
            pltpu.VMEM((AT_LA, 2 * HEAD_DIM), BF16),
            pltpu.VMEM((AT_LA, LANES), BF16),
            pltpu.VMEM((AT_LA, LANES), BF16),
            pltpu.SMEM((1,), jnp.int32),
            pltpu.VMEM((2 * AT_TQ, LANES), F32),
            pltpu.VMEM((2 * AT_TQ, 2 * HEAD_DIM), F32),
        ],
        compiler_params=pltpu.CompilerParams(
            dimension_semantics=("arbitrary", "arbitrary", "arbitrary"),
            vmem_limit_bytes=56 * 1024 * 1024),
        name="diff_attn",
    )(lam_params, proj3, proj3, proj3, *([proj3] * AT_SUB), attn_norm_w)


DN_T = 2 * CHUNK
HALO = 16
DN_GROUP = 8


def _shift_matrix():
    r = jnp.arange((CONV_K - 1) * DN_T)
    src = HALO + r % DN_T - (r // DN_T + 1)
    return (src[:, None] == jnp.arange(HALO + DN_T)[None, :]).astype(BF16)


def _split3(x):
    hi = x.astype(BF16)
    r1 = x - hi.astype(F32)
    mid = r1.astype(BF16)
    lo = (r1 - mid.astype(F32)).astype(BF16)
    return hi, mid, lo


def _dn_prep_kernel(dq_ref, dk_ref, dv_ref, hq_ref, hk_ref, hv_ref, cw_ref, sh_ref, gcol_ref, grow_ref,
                    gpr_ref, gpc_ref, u_ref, w_ref, qd_ref, kd_ref, a_ref, cd_ref):
    i = pl.program_id(1)
    t = DN_T
    width = D_MODEL
    cw = cw_ref[...]
    shifter = sh_ref[...]
    ys = []
    for idx, (t_ref, h_ref) in enumerate(((dq_ref, hq_ref), (dk_ref, hk_ref), (dv_ref, hv_ref))):
        halo = h_ref[...]
        halo = jnp.where(i == 0, jnp.zeros_like(halo), halo)
        x = t_ref[...]
        shifted = _dot(shifter, jnp.concatenate([halo, x], axis=0))
        cwi = cw[:, idx * width:(idx + 1) * width]
        yi = cwi[CONV_K - 1:CONV_K] * x.astype(F32)
        for s in range(1, CONV_K):
            yi = yi + cwi[CONV_K - 1 - s:CONV_K - s] * shifted[(s - 1) * t:s * t]
        ys.append(yi * jax.nn.sigmoid(yi))
    y = jnp.concatenate(ys, axis=1)

    r = lax.broadcasted_iota(jnp.int32, (t, t), 0)
    c = lax.broadcasted_iota(jnp.int32, (t, t), 1)
    same = (r >= CHUNK) == (c >= CHUNK)
    incl = same & (c <= r)
    strict = same & (c < r)
    tri = jnp.where(incl, 1.0, 0.0).astype(BF16)
    tri_t = jnp.where(same & (r <= c), 1.0, 0.0).astype(BF16)
    blk = jnp.where(same, 1.0, 0.0).astype(BF16)
    eye = jnp.where(r == c, 1.0, 0.0).astype(F32)

    lane = lax.broadcasted_iota(jnp.int32, (t, LANES), 1)
    rowi = lax.broadcasted_iota(jnp.int32, (t, LANES), 0) + i * t
    gcol = gcol_ref[...]
    gpr = gpr_ref[...]
    is_g = (lane >= HEADS) & (lane < 2 * HEADS)
    live_c = rowi >= FRONT
    beta_c = jnp.where(live_c, jax.nn.sigmoid(gcol), 0.0)
    xg = gcol + gpr[1:2]
    sp = jnp.maximum(xg, 0.0) + jnp.log1p(jnp.exp(-jnp.abs(xg)))
    g_c = jnp.where(is_g & live_c, -jnp.exp(gpr[0:1]) * sp, 0.0)
    g3 = _split3(g_c)
    gc_c = _dot(tri, g3[0]) + _dot(tri, g3[1]) + _dot(tri, g3[2])
    gl_c = _dot(blk, g3[0]) + _dot(blk, g3[1]) + _dot(blk, g3[2])
    sub = lax.broadcasted_iota(jnp.int32, (2 * HEADS, t), 0)
    coli = lax.broadcasted_iota(jnp.int32, (2 * HEADS, t), 1) + i * t
    xr = grow_ref[...] + gpc_ref[1]
    spr = jnp.maximum(xr, 0.0) + jnp.log1p(jnp.exp(-jnp.abs(xr)))
    g_r = jnp.where((sub >= HEADS) & (coli >= FRONT), -jnp.exp(gpc_ref[0]) * spr, 0.0)
    gr3 = _split3(g_r)
    gc_r = _dot(gr3[0], tri_t) + _dot(gr3[1], tri_t) + _dot(gr3[2], tri_t)

    cd_ref[...] = jnp.exp(gl_c)

    def group(hs):
        sls = {h: slice(h * HEAD_DIM, (h + 1) * HEAD_DIM) for h in hs}
        q = {h: y[:, h * HEAD_DIM:(h + 1) * HEAD_DIM] for h in hs}
        k = {h: y[:, width + h * HEAD_DIM:width + (h + 1) * HEAD_DIM] for h in hs}
        v = {h: y[:, 2 * width + h * HEAD_DIM:2 * width + (h + 1) * HEAD_DIM] for h in hs}
        q = {h: x * lax.rsqrt(jnp.sum(x * x, axis=-1, keepdims=True) + 1e-6) * (HEAD_DIM ** -0.5)
             for h, x in q.items()}
        k = {h: x * lax.rsqrt(jnp.sum(x * x, axis=-1, keepdims=True) + 1e-6) for h, x in k.items()}
        beta = {h: beta_c[:, h:h + 1] for h in hs}
        gcc = {h: gc_c[:, HEADS + h:HEADS + h + 1] for h in hs}
        glc = {h: gl_c[:, HEADS + h:HEADS + h + 1] for h in hs}
        gcr = {h: gc_r[HEADS + h:HEADS + h + 1, :] for h in hs}
        dec = {h: jnp.where(incl, jnp.exp(jnp.where(incl, gcc[h] - gcr[h], 0.0)), 0.0) for h in hs}
        kb = {h: k[h] * beta[h] for h in hs}
        kt = {h: k[h].T.astype(BF16) for h in hs}
        m = {h: jnp.where(strict, _dot(kb[h].astype(BF16), kt[h]) * dec[h], 0.0) for h in hs}
        tinv = {h: eye - m[h] for h in hs}
        pwb = {h: m[h].astype(BF16) for h in hs}
        for _ in range(5):
            pw = {h: _dot(pwb[h], pwb[h]) for h in hs}
            pwb = {h: pw[h].astype(BF16) for h in hs}
            tinv = {h: tinv[h] + _dot(tinv[h].astype(BF16), pwb[h]) for h in hs}
        egc = {h: jnp.exp(gcc[h]) for h in hs}
        rhs = {h: jnp.concatenate([v[h] * beta[h], kb[h] * egc[h]], axis=1).astype(BF16) for h in hs}
        uw = {h: _dot(tinv[h].astype(BF16), rhs[h]) for h in hs}
        a_full = {h: _dot(q[h].astype(BF16), kt[h]) * dec[h] for h in hs}
        for h in hs:
            u_ref[:, sls[h]] = uw[h][:, :HEAD_DIM].astype(BF16)
            w_ref[:, sls[h]] = uw[h][:, HEAD_DIM:].astype(BF16)
            a_cmp = a_full[h] + pltpu.roll(a_full[h], CHUNK, axis=1)
            a_ref[:, h * CHUNK:(h + 1) * CHUNK] = a_cmp[:, :CHUNK].astype(BF16)
            qd_ref[:, sls[h]] = (q[h] * egc[h]).astype(BF16)
            kd_ref[:, sls[h]] = (k[h] * jnp.exp(glc[h] - gcc[h])).astype(BF16)

    for g in range(HEADS // DN_GROUP):
        group(range(g * DN_GROUP, (g + 1) * DN_GROUP))


def _dn_prep(proj3, conv_w2, gcol3, grow, gp_row, gp_col):
    b = proj3.shape[0]
    t = DN_T
    nt = LP // t
    hpb = t // HALO

    def tile(cb):
        return pl.BlockSpec((None, t, D_MODEL), lambda b_, i: (b_, i, cb))

    def halo(cb):
        return pl.BlockSpec((None, HALO, D_MODEL), lambda b_, i: (b_, jnp.maximum(i * hpb - 1, 0), cb))

    full = pl.BlockSpec((None, t, D_MODEL), lambda b_, i: (b_, i, 0))
    return pl.pallas_call(
        _dn_prep_kernel,
        grid=(b, nt),
        in_specs=[
            tile(C_DQ), tile(C_DK), tile(C_DV), halo(C_DQ), halo(C_DK), halo(C_DV),
            pl.BlockSpec((CONV_K, 3 * D_MODEL), lambda b_, i: (0, 0)),
            pl.BlockSpec(((CONV_K - 1) * t, HALO + t), lambda b_, i: (0, 0)),
            pl.BlockSpec((None, t, LANES), lambda b_, i: (b_, i, 0)),
            pl.BlockSpec((2 * HEADS, t), lambda b_, i: (0, b_ * nt + i)),
            pl.BlockSpec((2, LANES), lambda b_, i: (0, 0)),
            pl.BlockSpec((2, 2 * HEADS, t), lambda b_, i: (0, 0, 0)),
        ],
        out_specs=[
            full, full, full, full,
            pl.BlockSpec((None, t, HEADS * CHUNK), lambda b_, i: (b_, i, 0)),
            pl.BlockSpec((None, t, LANES), lambda b_, i: (b_, i, 0)),
        ],
        out_shape=[
            jax.ShapeDtypeStruct((b, LP, D_MODEL), BF16),
            jax.ShapeDtypeStruct((b, LP, D_MODEL), BF16),
            jax.ShapeDtypeStruct((b, LP, D_MODEL), BF16),
            jax.ShapeDtypeStruct((b, LP, D_MODEL), BF16),
            jax.ShapeDtypeStruct((b, LP, HEADS * CHUNK), BF16),
            jax.ShapeDtypeStruct((b, LP, LANES), F32),
        ],
        compiler_params=pltpu.CompilerParams(
            dimension_semantics=("parallel", "parallel"),
            vmem_limit_bytes=40 * 1024 * 1024),
        name="dn_prep",
    )(proj3, proj3, proj3, proj3, proj3, proj3, conv_w2, _shift_matrix(), gcol3, grow, gp_row, gp_col)


SC_T = 4 * CHUNK


def _dn_scan_kernel(u_ref, w_ref, qd_ref, kd_ref, a_ref, cd_ref, z_ref, nw_ref, o_ref, s_sc):
    @pl.when(pl.program_id(1) == 0)
    def _():
        s_sc[...] = jnp.zeros(s_sc.shape, F32)

    nw = nw_ref[...]
    hs = range(HEADS)
    sls = [slice(h * HEAD_DIM, (h + 1) * HEAD_DIM) for h in hs]
    s = [s_sc[h] for h in hs]
    for cix in range(SC_T // CHUNK):
        rs = slice(cix * CHUNK, (cix + 1) * CHUNK)
        sb = [x.astype(BF16) for x in s]
        ws_qs = [_dot(jnp.concatenate([w_ref[rs, sls[h]], qd_ref[rs, sls[h]]], axis=0), sb[h]) for h in hs]
        vb = [(u_ref[rs, sls[h]].astype(F32) - ws_qs[h][:CHUNK]).astype(BF16) for h in hs]
        o = [ws_qs[h][CHUNK:] + _dot(a_ref[rs, h * CHUNK:(h + 1) * CHUNK], vb[h]) for h in hs]
        cd = [cd_ref[cix * CHUNK:cix * CHUNK + 1, HEADS + h:HEADS + h + 1] for h in hs]
        s = [s[h] * cd[h] + _dot_tn(kd_ref[rs, sls[h]], vb[h]) for h in hs]
        for h in hs:
            z = z_ref[rs, sls[h]].astype(F32)
            o_ref[rs, sls[h]] = (_rms(o[h], nw) * (z * jax.nn.sigmoid(z))).astype(BF16)
    for h in hs:
        s_sc[h] = s[h]


def _dn_scan(u, w, qd, kd, a, cd, proj3, dn_norm_w):
    b = u.shape[0]
    t = SC_T
    full = pl.BlockSpec((None, t, D_MODEL), lambda b_, i: (b_, i, 0))
    return pl.pallas_call(
        _dn_scan_kernel,
        grid=(b, LP // t),
        in_specs=[
            full, full, full, full,
            pl.BlockSpec((None, t, HEADS * CHUNK), lambda b_, i: (b_, i, 0)),
            pl.BlockSpec((None, t, LANES), lambda b_, i: (b_, i, 0)),
            pl.BlockSpec((None, t, D_MODEL), lambda b_, i: (b_, i, C_DZ)),
            pl.BlockSpec((1, HEAD_DIM), lambda b_, i: (0, 0)),
        ],
        out_specs=full,
        out_shape=jax.ShapeDtypeStruct((b, LP, D_MODEL), BF16),
        scratch_shapes=[pltpu.VMEM((HEADS, HEAD_DIM, HEAD_DIM), F32)],
        compiler_params=pltpu.CompilerParams(
            dimension_semantics=("parallel", "arbitrary"),
            vmem_limit_bytes=40 * 1024 * 1024),
        name="dn_scan",
    )(u, w, qd, kd, a, cd, proj3, dn_norm_w)


OUT_SUB = 2


def _merge_kernel(*refs):
    n = OUT_SUB
    xa_refs, xd_refs, ga_refs, gd_refs = refs[0:n], refs[n:2 * n], refs[2 * n:3 * n], refs[3 * n:4 * n]
    x_ref, wa_ref, wd_ref, wo_ref, nw_ref, o_ref = refs[4 * n:]

    def rows(rs):
        return jnp.concatenate([r[...] for r in rs], axis=0)

    ya = _dot(rows(xa_refs), wa_ref[...])
    yd = _dot(rows(xd_refs), wd_ref[...])
    merged = (jax.nn.sigmoid(rows(ga_refs).astype(F32)) * ya
              + jax.nn.sigmoid(rows(gd_refs).astype(F32)) * yd)
    out = x_ref[...] + _dot(merged.astype(BF16), wo_ref[...])
    o_ref[...] = _rms(out, nw_ref[...])


def _merge(xa, xd, proj3, x, wa, wd, wo, final_norm_w):
    b = x.shape[0]
    tm = OUT_SUB * X0

    def padded(cb):
        return [pl.BlockSpec((None, X0, D_MODEL), lambda b_, i, s=s: (b_, OUT_SUB * i + 1 + s, cb))
                for s in range(OUT_SUB)]

    tok = pl.BlockSpec((None, tm, D_MODEL), lambda b_, i: (b_, i, 0))
    wspec = pl.BlockSpec((D_MODEL, D_MODEL), lambda b_, i: (0, 0))
    return pl.pallas_call(
        _merge_kernel,
        grid=(b, SEQ // tm),
        in_specs=padded(0) + padded(0) + padded(C_GA) + padded(C_GD) + [
            tok, wspec, wspec, wspec,
            pl.BlockSpec((1, D_MODEL), lambda b_, i: (0, 0)),
        ],
        out_specs=tok,
        out_shape=jax.ShapeDtypeStruct((b, SEQ, D_MODEL), F32),
        compiler_params=pltpu.CompilerParams(
            dimension_semantics=("parallel", "parallel"),
            vmem_limit_bytes=48 * 1024 * 1024),
        name="merge_out",
    )(*([xa] * OUT_SUB + [xd] * OUT_SUB + [proj3] * (2 * OUT_SUB)), x, wa, wd, wo, final_norm_w)


def kernel(x, meta_tokens, norm_w, w_in, lambda_q1, lambda_k1, lambda_q2, lambda_k2, attn_norm_w,
           conv_w, a_log, dt_bias, dn_norm_w, w_branch_attn, w_branch_delta, w_out, final_norm_w):
    b = x.shape[0]
    assert x.shape == (b, SEQ, D_MODEL) and norm_w.shape[0] == 1
    layer = 0
    lam_init = 0.8 - 0.6 * math.exp(-0.3 * layer)

    head = jnp.concatenate([jnp.zeros((FRONT, D_MODEL), x.dtype), meta_tokens.astype(x.dtype)], axis=0)

    wi = w_in[layer]
    gate0 = 8 * D_MODEL
    q_scale = QK_DIM ** -0.5 * math.log2(math.e)
    col_scale = jnp.concatenate([jnp.full((1, D_MODEL), q_scale, F32), jnp.ones((1, gate0 - D_MODEL), F32)], axis=1)
    w_all = wi.astype(BF16)
    w_tail = wi[:, gate0 + 2 * HEADS:].astype(BF16)
    w_gate = wi[:, gate0:gate0 + 2 * HEADS]
    w_gate_p = jnp.pad(w_gate, ((0, 0), (0, LANES - 2 * HEADS))).astype(BF16)
    zeros8 = jnp.zeros((HEADS,), F32)
    gp_row = jnp.stack([jnp.pad(jnp.concatenate([zeros8, a_log[layer]]), (0, LANES - 2 * HEADS)),
                        jnp.pad(jnp.concatenate([zeros8, dt_bias[layer]]), (0, LANES - 2 * HEADS))])
    gp_col = jnp.stack([jnp.concatenate([zeros8, a_log[layer]]),
                        jnp.concatenate([zeros8, dt_bias[layer]])])
    gp_col = jnp.broadcast_to(gp_col[:, :, None], (2, 2 * HEADS, DN_T)).astype(F32)
    lam_params = jnp.stack([lambda_q1[layer], lambda_k1[layer], lambda_q2[layer], lambda_k2[layer]])

    proj2, gcol, grow = _inproj(head, x, norm_w[layer][None], w_all, w_tail, col_scale, w_gate_p)
    proj3 = proj2.reshape(b, LP, N_MAIN)

    xa = _attention(proj3, lam_params, attn_norm_w[layer][None], lam_init)

    u, w, qd, kd, a, cd = _dn_prep(proj3, conv_w[layer], gcol.reshape(b, LP, LANES), grow, gp_row, gp_col)
    xd = _dn_scan(u, w, qd, kd, a, cd, proj3, dn_norm_w[layer][None])

    return _merge(xa, xd, proj3, x, w_branch_attn[layer].astype(BF16), w_branch_delta[layer].astype(BF16),
                  w_out[layer].astype(BF16), final_norm_w[None])
```

```python
import functools
import math

import jax
import jax.numpy as jnp
from jax import lax
from jax.experimental import pallas as pl
from jax.experimental.pallas import tpu as pltpu

D_MODEL = 1024
SEQ = 8192
N_META = 16
NORM_EPS = 1e-6
HEADS = 8
HEAD_DIM = 128
QK_DIM = 64
CONV_K = 4
CHUNK = 64
LANES = 128
LP = 8448
FRONT = LP - SEQ - N_META
X0 = FRONT + N_META
N_MAIN = 10 * D_MODEL
NEG = -1e30

C_AQ, C_AK, C_AV, C_AZ, C_DQ, C_DK, C_DV, C_DZ, C_GA, C_GD = range(10)

F32 = jnp.float32
BF16 = jnp.bfloat16


def _rms(x, w):
    return x * lax.rsqrt(jnp.mean(x * x, axis=-1, keepdims=True) + NORM_EPS) * w


def _dot(a, b):
    return jnp.dot(a, b, preferred_element_type=F32)


def _dot_nt(a, b):
    return lax.dot_general(a, b, (((1,), (1,)), ((), ())), preferred_element_type=F32)


def _dot_tn(a, b):
    return lax.dot_general(a, b, (((0,), (0,)), ((), ())), preferred_element_type=F32)


IN_SUB = 3
IN_TM = IN_SUB * X0
IN_TN = 2048
IN_TILES = LP // IN_TM
IN_FULL_TILES = 8 * D_MODEL // IN_TN


def _inproj_kernel(*refs):
    head_ref = refs[0]
    x_refs = refs[1:1 + IN_SUB]
    nw_ref, w_ref, wt_ref, cs_ref, wg_ref, p_ref, gcol_ref, grow_ref, hn_sc = refs[1 + IN_SUB:]
    j = pl.program_id(1)

    @pl.when(j == 0)
    def _():
        first_tile = pl.program_id(0) % IN_TILES == 0
        nw = nw_ref[...]
        for s, x_ref in enumerate(x_refs):
            h = x_ref[...]
            if s == 0:
                h = jnp.where(first_tile, head_ref[...], h)
            hn_sc[s * X0:(s + 1) * X0, :] = _rms(h, nw).astype(BF16)
        gates = _dot(hn_sc[...], wg_ref[...])
        gcol_ref[...] = gates
        grow_ref[...] = gates.T[:2 * HEADS, :]

    @pl.when(j < IN_FULL_TILES)
    def _():
        p_ref[...] = (_dot(hn_sc[...], w_ref[...]) * cs_ref[...]).astype(BF16)

    @pl.when(j >= IN_FULL_TILES)
    def _():
        p_ref[...] = _dot(hn_sc[...], wt_ref[...]).astype(BF16)


def _inproj(head, x, norm_w, w_all, w_tail, col_scale, w_gate):
    b = x.shape[0]
    rows = b * LP
    grid = (rows // IN_TM, N_MAIN // IN_TN)

    def xblock(s):
        return pl.BlockSpec((None, X0, D_MODEL),
                            lambda i, j: (i // IN_TILES, jnp.maximum((i % IN_TILES) * IN_SUB + s - 1, 0), 0))

    return pl.pallas_call(
        _inproj_kernel,
        grid=grid,
        in_specs=[pl.BlockSpec((X0, D_MODEL), lambda i, j: (0, 0), pipeline_mode=pl.Buffered(1))]
        + [xblock(s) for s in range(IN_SUB)] + [
            pl.BlockSpec((1, D_MODEL), lambda i, j: (0, 0)),
            pl.BlockSpec((D_MODEL, IN_TN), lambda i, j: (0, jnp.minimum(j, IN_FULL_TILES - 1))),
            pl.BlockSpec((D_MODEL, IN_TN), lambda i, j: (0, jnp.maximum(j - IN_FULL_TILES, 0))),
            pl.BlockSpec((1, IN_TN), lambda i, j: (0, jnp.minimum(j, IN_FULL_TILES - 1))),
            pl.BlockSpec((D_MODEL, LANES), lambda i, j: (0, 0)),
        ],
        out_specs=[
            pl.BlockSpec((IN_TM, IN_TN), lambda i, j: (i, j)),
            pl.BlockSpec((IN_TM, LANES), lambda i, j: (i, 0)),
            pl.BlockSpec((2 * HEADS, IN_TM), lambda i, j: (0, i)),
        ],
        out_shape=[
            jax.ShapeDtypeStruct((rows, N_MAIN), BF16),
            jax.ShapeDtypeStruct((rows, LANES), F32),
            jax.ShapeDtypeStruct((2 * HEADS, rows), F32),
        ],
        scratch_shapes=[pltpu.VMEM((IN_TM, D_MODEL), BF16)],
        compiler_params=pltpu.CompilerParams(
            dimension_semantics=("parallel", "arbitrary"),
            vmem_limit_bytes=56 * 1024 * 1024),
        name="inproj",
    )(head, *([x] * IN_SUB), norm_w, w_all, w_tail, col_scale, w_gate)


AT_SUB = 2
AT_TQ = AT_SUB * X0
AT_BIG = 6
AT_TILES = -(-LP // AT_TQ)
AT_LA = AT_TILES * AT_TQ
AT_SHIFT_MAX = 30.0


def _attn_kernel(*refs, lam_init):
    lam_ref, q_ref, k_ref, v_ref = refs[0:4]
    z_refs = refs[4:4 + AT_SUB]
    nw_ref = refs[4 + AT_SUB]
    o_ref = refs[5 + AT_SUB]
    qx_sc, kx_sc, vx_sc, e1_sc, e2_sc, flag_sc, m_sc, acc_sc = refs[6 + AT_SUB:]
    i = pl.program_id(2)
    tq = AT_TQ

    @pl.when((pl.program_id(0) == 0) & (pl.program_id(1) == 0) & (i == 0))
    def _():
        rowi = lax.broadcasted_iota(jnp.int32, (LP, LANES), 0)
        ln = lax.broadcasted_iota(jnp.int32, (LP, LANES), 1)
        ext = jnp.where(ln == 0, 1.0, jnp.where((ln == 1) & (rowi < FRONT), NEG, 0.0))
        kx_sc[0:LP, HEAD_DIM:] = ext.astype(BF16)
        vx_sc[0:LP, HEAD_DIM:] = jnp.ones((LP, HEAD_DIM), BF16)
        if AT_LA > LP:
            for ref in (qx_sc, kx_sc, vx_sc, e1_sc, e2_sc):
                ref[LP:, :] = jnp.zeros((AT_LA - LP, ref.shape[1]), BF16)

    @pl.when(i == 0)
    def _():
        k = k_ref[...]
        kx_sc[0:LP, 0:HEAD_DIM] = k
        vx_sc[0:LP, 0:HEAD_DIM] = v_ref[...]
        ln = lax.broadcasted_iota(jnp.int32, (LP, LANES), 1)
        first = ln < QK_DIM
        pad_lane = jnp.where(ln == 1, 1.0, 0.0)

        def half_norms(x):
            sq = x.astype(F32)
            sq = sq * sq
            return (jnp.sqrt(jnp.sum(jnp.where(first, sq, 0.0), axis=-1, keepdims=True)),
                    jnp.sqrt(jnp.sum(jnp.where(first, 0.0, sq), axis=-1, keepdims=True)))

        q = q_ref[...]
        qx_sc[0:LP, :] = q
        kn1, kn2 = half_norms(k)
        qn1, qn2 = half_norms(q)
        shift1 = qn1 * jnp.max(kn1, axis=0, keepdims=True)
        shift2 = qn2 * jnp.max(kn2, axis=0, keepdims=True)
        e1_sc[0:LP, :] = (pad_lane - jnp.where(ln == 0, shift1, 0.0)).astype(BF16)
        e2_sc[0:LP, :] = (pad_lane - jnp.where(ln == 0, shift2, 0.0)).astype(BF16)
        ok = jnp.maximum(jnp.max(shift1), jnp.max(shift2)) <= AT_SHIFT_MAX
        flag_sc[0] = ok.astype(jnp.int32)

    row0 = pl.multiple_of(i * tq, tq)
    q = qx_sc[pl.ds(row0, tq), :]
    first = lax.broadcasted_iota(jnp.int32, (tq, LANES), 1) < QK_DIM
    zero = jnp.zeros_like(q)
    use_shift = flag_sc[0] == 1
    q2 = jnp.concatenate([jnp.where(first, q, zero), jnp.where(first, zero, q)], axis=0)

    def causal(s):
        r2 = lax.broadcasted_iota(jnp.int32, (2 * tq, tq), 0)
        row = jnp.where(r2 >= tq, r2 - tq, r2)
        col = lax.broadcasted_iota(jnp.int32, (2 * tq, tq), 1)
        return jnp.where(col <= row, s, NEG)

    def scores(qx, start, units):
        return _dot_nt(qx, kx_sc[pl.ds(start, units * tq), :])

    @pl.when(use_shift)
    def _():
        bias = jnp.concatenate([e1_sc[pl.ds(row0, tq), :], e2_sc[pl.ds(row0, tq), :]], axis=0)
        qx = jnp.concatenate([q2, bias], axis=1)

        def consume(start, units, s):
            p = jnp.exp2(s).astype(BF16)
            acc_sc[...] += _dot(p, vx_sc[pl.ds(start, units * tq), :])

        n_big = i // AT_BIG
        tail_units = i - n_big * AT_BIG + 1
        tail_start = pl.multiple_of(n_big * (AT_BIG * tq), AT_BIG * tq)
        half = tq // 2
        qx_low = jnp.concatenate([qx[half:tq], qx[tq + half:]], axis=0)

        def tri(s, rows):
            r2 = lax.broadcasted_iota(jnp.int32, (2 * rows, half), 0)
            row = jnp.where(r2 >= rows, r2 - rows, r2)
            col = lax.broadcasted_iota(jnp.int32, (2 * rows, half), 1)
            return jnp.where(col <= row, s, NEG)

        for units in range(1, AT_BIG + 1):
            @pl.when(tail_units == units)
            def _():
                main = units * tq - half
                s = _dot_nt(qx, kx_sc[pl.ds(tail_start, main), :])
                s_edge = tri(s[:, main - half:], tq)
                if main > half:
                    s_edge = jnp.concatenate([s[:, :main - half], s_edge], axis=1)
                p = jnp.exp2(s_edge).astype(BF16)
                acc_sc[...] = _dot(p, vx_sc[pl.ds(tail_start, main), :])
                last = pl.multiple_of(tail_start + main, half)
                s_low = tri(_dot_nt(qx_low, kx_sc[pl.ds(last, half), :]), half)
                pv = _dot(jnp.exp2(s_low).astype(BF16), vx_sc[pl.ds(last, half), :])
                acc_sc[half:tq, :] += pv[:half]
                acc_sc[tq + half:, :] += pv[half:]

        def big(j, c):
            start = pl.multiple_of(j * (AT_BIG * tq), AT_BIG * tq)
            consume(start, AT_BIG, scores(qx, start, AT_BIG))
            return c

        lax.fori_loop(0, n_big, big, 0)

    @pl.when(jnp.logical_not(use_shift))
    def _():
        lane2 = lax.broadcasted_iota(jnp.int32, (2 * tq, LANES), 1)
        qx = jnp.concatenate([q2, jnp.where(lane2 == 1, 1.0, 0.0).astype(BF16)], axis=1)
        m_sc[...] = jnp.full(m_sc.shape, NEG, F32)
        acc_sc[...] = jnp.zeros(acc_sc.shape, F32)

        def block(j, diag):
            start = pl.multiple_of(j * tq, tq)
            s = scores(qx, start, 1)
            if diag:
                s = causal(s)
            chunks = [s[:, c * LANES:(c + 1) * LANES] for c in range(tq // LANES)]
            mc = chunks[0]
            for ch in chunks[1:]:
                mc = jnp.maximum(mc, ch)
            m_prev = m_sc[...]
            m_new = jnp.maximum(m_prev, jnp.max(mc, axis=-1, keepdims=True))
            alpha = jnp.exp2(m_prev - m_new)
            p = jnp.concatenate([jnp.exp2(ch - m_new) for ch in chunks], axis=1).astype(BF16)
            pv = _dot(p, vx_sc[pl.ds(start, tq), :])
            acc_sc[...] = jnp.concatenate([alpha, alpha], axis=1) * acc_sc[...] + pv
            m_sc[...] = m_new

        def full(j, c):
            block(j, False)
            return c

        lax.fori_loop(0, i, full, 0)
        block(i, True)

    lp = lam_ref[...]
    lam = (jnp.exp(jnp.sum(lp[0:1] * lp[1:2], axis=-1, keepdims=True))
           - jnp.exp(jnp.sum(lp[2:3] * lp[3:4], axis=-1, keepdims=True)) + lam_init)
    acc = acc_sc[...]
    o_all = acc[:, :HEAD_DIM] / jnp.maximum(acc[:, HEAD_DIM:], 1e-37)
    o = o_all[:tq] - lam * o_all[tq:]
    o = _rms(o, nw_ref[...]) * (1.0 - lam_init)
    z = jnp.concatenate([r[...] for r in z_refs], axis=0).astype(F32)
    o_ref[...] = (o * (z * jax.nn.sigmoid(z))).astype(BF16)


def _attention(proj3, lam_params, attn_norm_w, lam_init):
    b = proj3.shape[0]
    hb = D_MODEL // HEAD_DIM
    nblk = LP // X0
    kern = functools.partial(_attn_kernel, lam_init=lam_init)

    def sub(s, col):
        return pl.BlockSpec((None, X0, HEAD_DIM),
                            lambda b_, h, i: (b_, jnp.minimum(AT_SUB * i + s, nblk - 1), col * hb + h))

    return pl.pallas_call(
        kern,
        grid=(b, HEADS, AT_TILES),
        in_specs=[
            pl.BlockSpec((4, QK_DIM), lambda b_, h, i: (0, 0)),
            pl.BlockSpec((None, LP, HEAD_DIM), lambda b_, h, i: (b_, 0, C_AQ * hb + h)),
            pl.BlockSpec((None, LP, HEAD_DIM), lambda b_, h, i: (b_, 0, C_AK * hb + h)),
            pl.BlockSpec((None, LP, HEAD_DIM), lambda b_, h, i: (b_, 0, C_AV * hb + h)),
        ] + [sub(s, C_AZ) for s in range(AT_SUB)] + [
            pl.BlockSpec((1, HEAD_DIM), lambda b_, h, i: (0, 0)),
        ],
        out_specs=pl.BlockSpec((None, AT_TQ, HEAD_DIM), lambda b_, h, i: (b_, i, h)),
        out_shape=jax.ShapeDtypeStruct((b, LP, D_MODEL), BF16),
        scratch_shapes=[
            pltpu.VMEM((AT_LA, HEAD_DIM), BF16),
            pltpu.VMEM((AT_LA, 2 * HEAD_DIM), BF16),
            pltpu.VMEM((AT_LA, 2 * HEAD_DIM), BF16),
            pltpu.VMEM((AT_LA, LANES), BF16),
            pltpu.VMEM((AT_LA, LANES), BF16),
            pltpu.SMEM((1,), jnp.int32),
            pltpu.VMEM((2 * AT_TQ, LANES), F32),
            pltpu.VMEM((2 * AT_TQ, 2 * HEAD_DIM), F32),
        ],
        compiler_params=pltpu.CompilerParams(
            dimension_semantics=("arbitrary", "arbitrary", "arbitrary"),
            vmem_limit_bytes=56 * 1024 * 1024),
        name="diff_attn",
    )(lam_params, proj3, proj3, proj3, *([proj3] * AT_SUB), attn_norm_w)


DN_T = 2 * CHUNK
HALO = 16
DN_GROUP = 8


def _shift_matrix():
    r = jnp.arange((CONV_K - 1) * DN_T)
    src = HALO + r % DN_T - (r // DN_T + 1)
    return (src[:, None] == jnp.arange(HALO + DN_T)[None, :]).astype(BF16)


def _split3(x):
    hi = x.astype(BF16)
    r1 = x - hi.astype(F32)
    mid = r1.astype(BF16)
    lo = (r1 - mid.astype(F32)).astype(BF16)
    return hi, mid, lo


def _dn_prep_kernel(dq_ref, dk_ref, dv_ref, hq_ref, hk_ref, hv_ref, cw_ref, sh_ref, gcol_ref, grow_ref,
                    gpr_ref, gpc_ref, u_ref, w_ref, qd_ref, kd_ref, a_ref, cd_ref):
    i = pl.program_id(1)
    t = DN_T
    width = D_MODEL
    cw = cw_ref[...]
    shifter = sh_ref[...]
    ys = []
    for idx, (t_ref, h_ref) in enumerate(((dq_ref, hq_ref), (dk_ref, hk_ref), (dv_ref, hv_ref))):
        halo = h_ref[...]
        halo = jnp.where(i == 0, jnp.zeros_like(halo), halo)
        x = t_ref[...]
        shifted = _dot(shifter, jnp.concatenate([halo, x], axis=0))
        cwi = cw[:, idx * width:(idx + 1) * width]
        yi = cwi[CONV_K - 1:CONV_K] * x.astype(F32)
        for s in range(1, CONV_K):
            yi = yi + cwi[CONV_K - 1 - s:CONV_K - s] * shifted[(s - 1) * t:s * t]
        ys.append(yi * jax.nn.sigmoid(yi))
    y = jnp.concatenate(ys, axis=1)

    r = lax.broadcasted_iota(jnp.int32, (t, t), 0)
    c = lax.broadcasted_iota(jnp.int32, (t, t), 1)
    same = (r >= CHUNK) == (c >= CHUNK)
    incl = same & (c <= r)
    strict = same & (c < r)
    tri = jnp.where(incl, 1.0, 0.0).astype(BF16)
    tri_t = jnp.where(same & (r <= c), 1.0, 0.0).astype(BF16)
    blk = jnp.where(same, 1.0, 0.0).astype(BF16)
    eye = jnp.where(r == c, 1.0, 0.0).astype(F32)

    lane = lax.broadcasted_iota(jnp.int32, (t, LANES), 1)
    rowi = lax.broadcasted_iota(jnp.int32, (t, LANES), 0) + i * t
    gcol = gcol_ref[...]
    gpr = gpr_ref[...]
    is_g = (lane >= HEADS) & (lane < 2 * HEADS)
    live_c = rowi >= FRONT
    beta_c = jnp.where(live_c, jax.nn.sigmoid(gcol), 0.0)
    xg = gcol + gpr[1:2]
    sp = jnp.maximum(xg, 0.0) + jnp.log1p(jnp.exp(-jnp.abs(xg)))
    g_c = jnp.where(is_g & live_c, -jnp.exp(gpr[0:1]) * sp, 0.0)
    g3 = _split3(g_c)
    gc_c = _dot(tri, g3[0]) + _dot(tri, g3[1]) + _dot(tri, g3[2])
    gl_c = _dot(blk, g3[0]) + _dot(blk, g3[1]) + _dot(blk, g3[2])
    sub = lax.broadcasted_iota(jnp.int32, (2 * HEADS, t), 0)
    coli = lax.broadcasted_iota(jnp.int32, (2 * HEADS, t), 1) + i * t
    xr = grow_ref[...] + gpc_ref[1]
    spr = jnp.maximum(xr, 0.0) + jnp.log1p(jnp.exp(-jnp.abs(xr)))
    g_r = jnp.where((sub >= HEADS) & (coli >= FRONT), -jnp.exp(gpc_ref[0]) * spr, 0.0)
    gr3 = _split3(g_r)
    gc_r = _dot(gr3[0], tri_t) + _dot(gr3[1], tri_t) + _dot(gr3[2], tri_t)

    cd_ref[...] = jnp.exp(gl_c)

    def group(hs):
        sls = {h: slice(h * HEAD_DIM, (h + 1) * HEAD_DIM) for h in hs}
        q = {h: y[:, h * HEAD_DIM:(h + 1) * HEAD_DIM] for h in hs}
        k = {h: y[:, width + h * HEAD_DIM:width + (h + 1) * HEAD_DIM] for h in hs}
        v = {h: y[:, 2 * width + h * HEAD_DIM:2 * width + (h + 1) * HEAD_DIM] for h in hs}
        q = {h: x * lax.rsqrt(jnp.sum(x * x, axis=-1, keepdims=True) + 1e-6) * (HEAD_DIM ** -0.5)
             for h, x in q.items()}
        k = {h: x * lax.rsqrt(jnp.sum(x * x, axis=-1, keepdims=True) + 1e-6) for h, x in k.items()}
        beta = {h: beta_c[:, h:h + 1] for h in hs}
        gcc = {h: gc_c[:, HEADS + h:HEADS + h + 1] for h in hs}
        glc = {h: gl_c[:, HEADS + h:HEADS + h + 1] for h in hs}
        gcr = {h: gc_r[HEADS + h:HEADS + h + 1, :] for h in hs}
        dec = {h: jnp.where(incl, jnp.exp(jnp.where(incl, gcc[h] - gcr[h], 0.0)), 0.0) for h in hs}
        kb = {h: k[h] * beta[h] for h in hs}
        kt = {h: k[h].T.astype(BF16) for h in hs}
        m = {h: jnp.where(strict, _dot(kb[h].astype(BF16), kt[h]) * dec[h], 0.0) for h in hs}
        tinv = {h: eye - m[h] for h in hs}
        pwb = {h: m[h].astype(BF16) for h in hs}
        for _ in range(5):
            pw = {h: _dot(pwb[h], pwb[h]) for h in hs}
            pwb = {h: pw[h].astype(BF16) for h in hs}
            tinv = {h: tinv[h] + _dot(tinv[h].astype(BF16), pwb[h]) for h in hs}
        egc = {h: jnp.exp(gcc[h]) for h in hs}
        rhs = {h: jnp.concatenate([v[h] * beta[h], kb[h] * egc[h]], axis=1).astype(BF16) for h in hs}
        uw = {h: _dot(tinv[h].astype(BF16), rhs[h]) for h in hs}
        a_full = {h: _dot(q[h].astype(BF16), kt[h]) * dec[h] for h in hs}
        for h in hs:
            u_ref[:, sls[h]] = uw[h][:, :HEAD_DIM].astype(BF16)
            w_ref[:, sls[h]] = uw[h][:, HEAD_DIM:].astype(BF16)
            a_cmp = a_full[h] + pltpu.roll(a_full[h], CHUNK, axis=1)
            a_ref[:, h * CHUNK:(h + 1) * CHUNK] = a_cmp[:, :CHUNK].astype(BF16)
            qd_ref[:, sls[h]] = (q[h] * egc[h]).astype(BF16)
            kd_ref[:, sls[h]] = (k[h] * jnp.exp(glc[h] - gcc[h])).astype(BF16)

    for g in range(HEADS // DN_GROUP):
        group(range(g * DN_GROUP, (g + 1) * DN_GROUP))


def _dn_prep(proj3, conv_w2, gcol3, grow, gp_row, gp_col):
    b = proj3.shape[0]
    t = DN_T
    nt = LP // t
    hpb = t // HALO

    def tile(cb):
        return pl.BlockSpec((None, t, D_MODEL), lambda b_, i: (b_, i, cb))

    def halo(cb):
        return pl.BlockSpec((None, HALO, D_MODEL), lambda b_, i: (b_, jnp.maximum(i * hpb - 1, 0), cb))

    full = pl.BlockSpec((None, t, D_MODEL), lambda b_, i: (b_, i, 0))
    return pl.pallas_call(
        _dn_prep_kernel,
        grid=(b, nt),
        in_specs=[
            tile(C_DQ), tile(C_DK), tile(C_DV), halo(C_DQ), halo(C_DK), halo(C_DV),
            pl.BlockSpec((CONV_K, 3 * D_MODEL), lambda b_, i: (0, 0)),
            pl.BlockSpec(((CONV_K - 1) * t, HALO + t), lambda b_, i: (0, 0)),
            pl.BlockSpec((None, t, LANES), lambda b_, i: (b_, i, 0)),
            pl.BlockSpec((2 * HEADS, t), lambda b_, i: (0, b_ * nt + i)),
            pl.BlockSpec((2, LANES), lambda b_, i: (0, 0)),
            pl.BlockSpec((2, 2 * HEADS, t), lambda b_, i: (0, 0, 0)),
        ],
        out_specs=[
            full, full, full, full,
            pl.BlockSpec((None, t, HEADS * CHUNK), lambda b_, i: (b_, i, 0)),
            pl.BlockSpec((None, t, LANES), lambda b_, i: (b_, i, 0)),
        ],
        out_shape=[
            jax.ShapeDtypeStruct((b, LP, D_MODEL), BF16),
            jax.ShapeDtypeStruct((b, LP, D_MODEL), BF16),
            jax.ShapeDtypeStruct((b, LP, D_MODEL), BF16),
            jax.ShapeDtypeStruct((b, LP, D_MODEL), BF16),
            jax.ShapeDtypeStruct((b, LP, HEADS * CHUNK), BF16),
            jax.ShapeDtypeStruct((b, LP, LANES), F32),
        ],
        compiler_params=pltpu.CompilerParams(
            dimension_semantics=("parallel", "parallel"),
            vmem_limit_bytes=40 * 1024 * 1024),
        name="dn_prep",
    )(proj3, proj3, proj3, proj3, proj3, proj3, conv_w2, _shift_matrix(), gcol3, grow, gp_row, gp_col)


SC_T = 4 * CHUNK


def _dn_scan_kernel(u_ref, w_ref, qd_ref, kd_ref, a_ref, cd_ref, z_ref, nw_ref, o_ref, s_sc):
    @pl.when(pl.program_id(1) == 0)
    def _():
        s_sc[...] = jnp.zeros(s_sc.shape, F32)

    nw = nw_ref[...]
    hs = range(HEADS)
    sls = [slice(h * HEAD_DIM, (h + 1) * HEAD_DIM) for h in hs]
    s = [s_sc[h] for h in hs]
    for cix in range(SC_T // CHUNK):
        rs = slice(cix * CHUNK, (cix + 1) * CHUNK)
        sb = [x.astype(BF16) for x in s]
        ws_qs = [_dot(jnp.concatenate([w_ref[rs, sls[h]], qd_ref[rs, sls[h]]], axis=0), sb[h]) for h in hs]
        vb = [(u_ref[rs, sls[h]].astype(F32) - ws_qs[h][:CHUNK]).astype(BF16) for h in hs]
        o = [ws_qs[h][CHUNK:] + _dot(a_ref[rs, h * CHUNK:(h + 1) * CHUNK], vb[h]) for h in hs]
        cd = [cd_ref[cix * CHUNK:cix * CHUNK + 1, HEADS + h:HEADS + h + 1] for h in hs]
        s = [s[h] * cd[h] + _dot_tn(kd_ref[rs, sls[h]], vb[h]) for h in hs]
        for h in hs:
            z = z_ref[rs, sls[h]].astype(F32)
            o_ref[rs, sls[h]] = (_rms(o[h], nw) * (z * jax.nn.sigmoid(z))).astype(BF16)
    for h in hs:
        s_sc[h] = s[h]


def _dn_scan(u, w, qd, kd, a, cd, proj3, dn_norm_w):
    b = u.shape[0]
    t = SC_T
    full = pl.BlockSpec((None, t, D_MODEL), lambda b_, i: (b_, i, 0))
    return pl.pallas_call(
        _dn_scan_kernel,
        grid=(b, LP // t),
        in_specs=[
            full, full, full, full,
            pl.BlockSpec((None, t, HEADS * CHUNK), lambda b_, i: (b_, i, 0)),
            pl.BlockSpec((None, t, LANES), lambda b_, i: (b_, i, 0)),
            pl.BlockSpec((None, t, D_MODEL), lambda b_, i: (b_, i, C_DZ)),
            pl.BlockSpec((1, HEAD_DIM), lambda b_, i: (0, 0)),
        ],
        out_specs=full,
        out_shape=jax.ShapeDtypeStruct((b, LP, D_MODEL), BF16),
        scratch_shapes=[pltpu.VMEM((HEADS, HEAD_DIM, HEAD_DIM), F32)],
        compiler_params=pltpu.CompilerParams(
            dimension_semantics=("parallel", "arbitrary"),
            vmem_limit_bytes=40 * 1024 * 1024),
        name="dn_scan",
    )(u, w, qd, kd, a, cd, proj3, dn_norm_w)


OUT_SUB = 2


def _merge_kernel(*refs):
    n = OUT_SUB
    xa_refs, xd_refs, ga_refs, gd_refs = refs[0:n], refs[n:2 * n], refs[2 * n:3 * n], refs[3 * n:4 * n]
    x_ref, wa_ref, wd_ref, wo_ref, nw_ref, o_ref = refs[4 * n:]

    def rows(rs):
        return jnp.concatenate([r[...] for r in rs], axis=0)

    ya = _dot(rows(xa_refs), wa_ref[...])
    yd = _dot(rows(xd_refs), wd_ref[...])
    merged = (jax.nn.sigmoid(rows(ga_refs).astype(F32)) * ya
              + jax.nn.sigmoid(rows(gd_refs).astype(F32)) * yd)
    out = x_ref[...] + _dot(merged.astype(BF16), wo_ref[...])
    o_ref[...] = _rms(out, nw_ref[...])


def _merge(xa, xd, proj3, x, wa, wd, wo, final_norm_w):
    b = x.shape[0]
    tm = OUT_SUB * X0

    def padded(cb):
        return [pl.BlockSpec((None, X0, D_MODEL), lambda b_, i, s=s: (b_, OUT_SUB * i + 1 + s, cb))
                for s in range(OUT_SUB)]

    tok = pl.BlockSpec((None, tm, D_MODEL), lambda b_, i: (b_, i, 0))
    wspec = pl.BlockSpec((D_MODEL, D_MODEL), lambda b_, i: (0, 0))
    return pl.pallas_call(
        _merge_kernel,
        grid=(b, SEQ // tm),
        in_specs=padded(0) + padded(0) + padded(C_GA) + padded(C_GD) + [
            tok, wspec, wspec, wspec,
            pl.BlockSpec((1, D_MODEL), lambda b_, i: (0, 0)),
        ],
        out_specs=tok,
        out_shape=jax.ShapeDtypeStruct((b, SEQ, D_MODEL), F32),
        compiler_params=pltpu.CompilerParams(
            dimension_semantics=("parallel", "parallel"),
            vmem_limit_bytes=48 * 1024 * 1024),
        name="merge_out",
    )(*([xa] * OUT_SUB + [xd] * OUT_SUB + [proj3] * (2 * OUT_SUB)), x, wa, wd, wo, final_norm_w)


def kernel(x, meta_tokens, norm_w, w_in, lambda_q1, lambda_k1, lambda_q2, lambda_k2, attn_norm_w,
           conv_w, a_log, dt_bias, dn_norm_w, w_branch_attn, w_branch_delta, w_out, final_norm_w):
    b = x.shape[0]
    assert x.shape == (b, SEQ, D_MODEL) and norm_w.shape[0] == 1
    layer = 0
    lam_init = 0.8 - 0.6 * math.exp(-0.3 * layer)

    head = jnp.concatenate([jnp.zeros((FRONT, D_MODEL), x.dtype), meta_tokens.astype(x.dtype)], axis=0)

    wi = w_in[layer]
    gate0 = 8 * D_MODEL
    q_scale = QK_DIM ** -0.5 * math.log2(math.e)
    col_scale = jnp.concatenate([jnp.full((1, D_MODEL), q_scale, F32), jnp.ones((1, gate0 - D_MODEL), F32)], axis=1)
    w_all = wi.astype(BF16)
    w_tail = wi[:, gate0 + 2 * HEADS:].astype(BF16)
    w_gate = wi[:, gate0:gate0 + 2 * HEADS]
    w_gate_p = jnp.pad(w_gate, ((0, 0), (0, LANES - 2 * HEADS))).astype(BF16)
    zeros8 = jnp.zeros((HEADS,), F32)
    gp_row = jnp.stack([jnp.pad(jnp.concatenate([zeros8, a_log[layer]]), (0, LANES - 2 * HEADS)),
                        jnp.pad(jnp.concatenate([zeros8, dt_bias[layer]]), (0, LANES - 2 * HEADS))])
    gp_col = jnp.stack([jnp.concatenate([zeros8, a_log[layer]]),
                        jnp.concatenate([zeros8, dt_bias[layer]])])
    gp_col = jnp.broadcast_to(gp_col[:, :, None], (2, 2 * HEADS, DN_T)).astype(F32)
    lam_params = jnp.stack([lambda_q1[layer], lambda_k1[layer], lambda_q2[layer], lambda_k2[layer]])

    proj2, gcol, grow = _inproj(head, x, norm_w[layer][None], w_all, w_tail, col_scale, w_gate_p)
    proj3 = proj2.reshape(b, LP, N_MAIN)

    xa = _attention(proj3, lam_params, attn_norm_w[layer][None], lam_init)

    u, w, qd, kd, a, cd = _dn_prep(proj3, conv_w[layer], gcol.reshape(b, LP, LANES), grow, gp_row, gp_col)
    xd = _dn_scan(u, w, qd, kd, a, cd, proj3, dn_norm_w[layer][None])

    return _merge(xa, xd, proj3, x, w_branch_attn[layer].astype(BF16), w_branch_delta[layer].astype(BF16),
                  w_out[layer].astype(BF16), final_norm_w[None])
```

```python
import functools
import math

import jax
import jax.numpy as jnp
from jax import lax
from jax.experimental import pallas as pl
from jax.experimental.pallas import tpu as pltpu

D_MODEL = 1024
SEQ = 8192
N_META = 16
NORM_EPS = 1e-6
HEADS = 8
HEAD_DIM = 128
QK_DIM = 64
CONV_K = 4
CHUNK = 64
LANES = 128
LP = 8448
FRONT = LP - SEQ - N_META
X0 = FRONT + N_META
N_MAIN = 10 * D_MODEL
NEG = -1e30

C_AQ, C_AK, C_AV, C_AZ, C_DQ, C_DK, C_DV, C_DZ, C_GA, C_GD = range(10)

F32 = jnp.float32
BF16 = jnp.bfloat16


def _rms(x, w):
    return x * lax.rsqrt(jnp.mean(x * x, axis=-1, keepdims=True) + NORM_EPS) * w


def _dot(a, b):
    return jnp.dot(a, b, preferred_element_type=F32)


def _dot_nt(a, b):
    return lax.dot_general(a, b, (((1,), (1,)), ((), ())), preferred_element_type=F32)


def _dot_tn(a, b):
    return lax.dot_general(a, b, (((0,), (0,)), ((), ())), preferred_element_type=F32)


IN_SUB = 3
IN_TM = IN_SUB * X0
IN_TN = 2048
IN_TILES = LP // IN_TM
IN_FULL_TILES = 8 * D_MODEL // IN_TN


def _inproj_kernel(*refs):
    head_ref = refs[0]
    x_refs = refs[1:1 + IN_SUB]
    nw_ref, w_ref, wt_ref, cs_ref, wg_ref, p_ref, gcol_ref, grow_ref, hn_sc = refs[1 + IN_SUB:]
    j = pl.program_id(1)

    @pl.when(j == 0)
    def _():
        first_tile = pl.program_id(0) % IN_TILES == 0
        nw = nw_ref[...]
        for s, x_ref in enumerate(x_refs):
            h = x_ref[...]
            if s == 0:
                h = jnp.where(first_tile, head_ref[...], h)
            hn_sc[s * X0:(s + 1) * X0, :] = _rms(h, nw).astype(BF16)
        gates = _dot(hn_sc[...], wg_ref[...])
        gcol_ref[...] = gates
        grow_ref[...] = gates.T[:2 * HEADS, :]

    @pl.when(j < IN_FULL_TILES)
    def _():
        p_ref[...] = (_dot(hn_sc[...], w_ref[...]) * cs_ref[...]).astype(BF16)

    @pl.when(j >= IN_FULL_TILES)
    def _():
        p_ref[...] = _dot(hn_sc[...], wt_ref[...]).astype(BF16)


def _inproj(head, x, norm_w, w_all, w_tail, col_scale, w_gate):
    b = x.shape[0]
    rows = b * LP
    grid = (rows // IN_TM, N_MAIN // IN_TN)

    def xblock(s):
        return pl.BlockSpec((None, X0, D_MODEL),
                            lambda i, j: (i // IN_TILES, jnp.maximum((i % IN_TILES) * IN_SUB + s - 1, 0), 0))

    return pl.pallas_call(
        _inproj_kernel,
        grid=grid,
        in_specs=[pl.BlockSpec((X0, D_MODEL), lambda i, j: (0, 0), pipeline_mode=pl.Buffered(1))]
        + [xblock(s) for s in range(IN_SUB)] + [
            pl.BlockSpec((1, D_MODEL), lambda i, j: (0, 0)),
            pl.BlockSpec((D_MODEL, IN_TN), lambda i, j: (0, jnp.minimum(j, IN_FULL_TILES - 1))),
            pl.BlockSpec((D_MODEL, IN_TN), lambda i, j: (0, jnp.maximum(j - IN_FULL_TILES, 0))),
            pl.BlockSpec((1, IN_TN), lambda i, j: (0, jnp.minimum(j, IN_FULL_TILES - 1))),
            pl.BlockSpec((D_MODEL, LANES), lambda i, j: (0, 0)),
        ],
        out_specs=[
            pl.BlockSpec((IN_TM, IN_TN), lambda i, j: (i, j)),
            pl.BlockSpec((IN_TM, LANES), lambda i, j: (i, 0)),
            pl.BlockSpec((2 * HEADS, IN_TM), lambda i, j: (0, i)),
        ],
        out_shape=[
            jax.ShapeDtypeStruct((rows, N_MAIN), BF16),
            jax.ShapeDtypeStruct((rows, LANES), F32),
            jax.ShapeDtypeStruct((2 * HEADS, rows), F32),
        ],
        scratch_shapes=[pltpu.VMEM((IN_TM, D_MODEL), BF16)],
        compiler_params=pltpu.CompilerParams(
            dimension_semantics=("parallel", "arbitrary"),
            vmem_limit_bytes=56 * 1024 * 1024),
        name="inproj",
    )(head, *([x] * IN_SUB), norm_w, w_all, w_tail, col_scale, w_gate)


AT_SUB = 2
AT_TQ = AT_SUB * X0
AT_BIG = 6
AT_TILES = -(-LP // AT_TQ)
AT_LA = AT_TILES * AT_TQ
AT_SHIFT_MAX = 30.0


def _attn_kernel(*refs, lam_init):
    lam_ref, q_ref, k_ref, v_ref = refs[0:4]
    z_refs = refs[4:4 + AT_SUB]
    nw_ref = refs[4 + AT_SUB]
    o_ref = refs[5 + AT_SUB]
    qx_sc, kx_sc, vx_sc, e1_sc, e2_sc, flag_sc, m_sc, acc_sc = refs[6 + AT_SUB:]
    i = pl.program_id(2)
    tq = AT_TQ

    @pl.when((pl.program_id(0) == 0) & (pl.program_id(1) == 0) & (i == 0))
    def _():
        rowi = lax.broadcasted_iota(jnp.int32, (LP, LANES), 0)
        ln = lax.broadcasted_iota(jnp.int32, (LP, LANES), 1)
        ext = jnp.where(ln == 0, 1.0, jnp.where((ln == 1) & (rowi < FRONT), NEG, 0.0))
        kx_sc[0:LP, HEAD_DIM:] = ext.astype(BF16)
        vx_sc[0:LP, HEAD_DIM:] = jnp.ones((LP, HEAD_DIM), BF16)
        if AT_LA > LP:
            for ref in (qx_sc, kx_sc, vx_sc, e1_sc, e2_sc):
                ref[LP:, :] = jnp.zeros((AT_LA - LP, ref.shape[1]), BF16)

    @pl.when(i == 0)
    def _():
        k = k_ref[...]
        kx_sc[0:LP, 0:HEAD_DIM] = k
        vx_sc[0:LP, 0:HEAD_DIM] = v_ref[...]
        ln = lax.broadcasted_iota(jnp.int32, (LP, LANES), 1)
        first = ln < QK_DIM
        pad_lane = jnp.where(ln == 1, 1.0, 0.0)

        def half_norms(x):
            sq = x.astype(F32)
            sq = sq * sq
            return (jnp.sqrt(jnp.sum(jnp.where(first, sq, 0.0), axis=-1, keepdims=True)),
                    jnp.sqrt(jnp.sum(jnp.where(first, 0.0, sq), axis=-1, keepdims=True)))

        q = q_ref[...]
        qx_sc[0:LP, :] = q
        kn1, kn2 = half_norms(k)
        qn1, qn2 = half_norms(q)
        shift1 = qn1 * jnp.max(kn1, axis=0, keepdims=True)
        shift2 = qn2 * jnp.max(kn2, axis=0, keepdims=True)
        e1_sc[0:LP, :] = (pad_lane - jnp.where(ln == 0, shift1, 0.0)).astype(BF16)
        e2_sc[0:LP, :] = (pad_lane - jnp.where(ln == 0, shift2, 0.0)).astype(BF16)
        ok = jnp.maximum(jnp.max(shift1), jnp.max(shift2)) <= AT_SHIFT_MAX
        flag_sc[0] = ok.astype(jnp.int32)

    row0 = pl.multiple_of(i * tq, tq)
    q = qx_sc[pl.ds(row0, tq), :]
    first = lax.broadcasted_iota(jnp.int32, (tq, LANES), 1) < QK_DIM
    zero = jnp.zeros_like(q)
    use_shift = flag_sc[0] == 1
    q2 = jnp.concatenate([jnp.where(first, q, zero), jnp.where(first, zero, q)], axis=0)

    def causal(s):
        r2 = lax.broadcasted_iota(jnp.int32, (2 * tq, tq), 0)
        row = jnp.where(r2 >= tq, r2 - tq, r2)
        col = lax.broadcasted_iota(jnp.int32, (2 * tq, tq), 1)
        return jnp.where(col <= row, s, NEG)

    def scores(qx, start, units):
        return _dot_nt(qx, kx_sc[pl.ds(start, units * tq), :])

    @pl.when(use_shift)
    def _():
        bias = jnp.concatenate([e1_sc[pl.ds(row0, tq), :], e2_sc[pl.ds(row0, tq), :]], axis=0)
        qx = jnp.concatenate([q2, bias], axis=1)

        def consume(start, units, s):
            p = jnp.exp2(s).astype(BF16)
            acc_sc[...] += _dot(p, vx_sc[pl.ds(start, units * tq), :])

        n_big = i // AT_BIG
        tail_units = i - n_big * AT_BIG + 1
        tail_start = pl.multiple_of(n_big * (AT_BIG * tq), AT_BIG * tq)
        half = tq // 2
        qx_low = jnp.concatenate([qx[half:tq], qx[tq + half:]], axis=0)

        def tri(s, rows):
            r2 = lax.broadcasted_iota(jnp.int32, (2 * rows, half), 0)
            row = jnp.where(r2 >= rows, r2 - rows, r2)
            col = lax.broadcasted_iota(jnp.int32, (2 * rows, half), 1)
            return jnp.where(col <= row, s, NEG)

        for units in range(1, AT_BIG + 1):
            @pl.when(tail_units == units)
            def _():
                main = units * tq - half
                s = _dot_nt(qx, kx_sc[pl.ds(tail_start, main), :])
                s_edge = tri(s[:, main - half:], tq)
                if main > half:
                    s_edge = jnp.concatenate([s[:, :main - half], s_edge], axis=1)
                p = jnp.exp2(s_edge).astype(BF16)
                acc_sc[...] = _dot(p, vx_sc[pl.ds(tail_start, main), :])
                last = pl.multiple_of(tail_start + main, half)
                s_low = tri(_dot_nt(qx_low, kx_sc[pl.ds(last, half), :]), half)
                pv = _dot(jnp.exp2(s_low).astype(BF16), vx_sc[pl.ds(last, half), :])
                acc_sc[half:tq, :] += pv[:half]
                acc_sc[tq + half:, :] += pv[half:]

        def big(j, c):
            start = pl.multiple_of(j * (AT_BIG * tq), AT_BIG * tq)
            consume(start, AT_BIG, scores(qx, start, AT_BIG))
            return c

        lax.fori_loop(0, n_big, big, 0)

    @pl.when(jnp.logical_not(use_shift))
    def _():
        lane2 = lax.broadcasted_iota(jnp.int32, (2 * tq, LANES), 1)
        qx = jnp.concatenate([q2, jnp.where(lane2 == 1, 1.0, 0.0).astype(BF16)], axis=1)
        m_sc[...] = jnp.full(m_sc.shape, NEG, F32)
        acc_sc[...] = jnp.zeros(acc_sc.shape, F32)

        def block(j, diag):
            start = pl.multiple_of(j * tq, tq)
            s = scores(qx, start, 1)
            if diag:
                s = causal(s)
            chunks = [s[:, c * LANES:(c + 1) * LANES] for c in range(tq // LANES)]
            mc = chunks[0]
            for ch in chunks[1:]:
                mc = jnp.maximum(mc, ch)
            m_prev = m_sc[...]
            m_new = jnp.maximum(m_prev, jnp.max(mc, axis=-1, keepdims=True))
            alpha = jnp.exp2(m_prev - m_new)
            p = jnp.concatenate([jnp.exp2(ch - m_new) for ch in chunks], axis=1).astype(BF16)
            pv = _dot(p, vx_sc[pl.ds(start, tq), :])
            acc_sc[...] = jnp.concatenate([alpha, alpha], axis=1) * acc_sc[...] + pv
            m_sc[...] = m_new

        def full(j, c):
            block(j, False)
            return c

        lax.fori_loop(0, i, full, 0)
        block(i, True)

    lp = lam_ref[...]
    lam = (jnp.exp(jnp.sum(lp[0:1] * lp[1:2], axis=-1, keepdims=True))
           - jnp.exp(jnp.sum(lp[2:3] * lp[3:4], axis=-1, keepdims=True)) + lam_init)
    acc = acc_sc[...]
    o_all = acc[:, :HEAD_DIM] / jnp.maximum(acc[:, HEAD_DIM:], 1e-37)
    o = o_all[:tq] - lam * o_all[tq:]
    o = _rms(o, nw_ref[...]) * (1.0 - lam_init)
    z = jnp.concatenate([r[...] for r in z_refs], axis=0).astype(F32)
    o_ref[...] = (o * (z * jax.nn.sigmoid(z))).astype(BF16)


def _attention(proj3, lam_params, attn_norm_w, lam_init):
    b = proj3.shape[0]
    hb = D_MODEL // HEAD_DIM
    nblk = LP // X0
    kern = functools.partial(_attn_kernel, lam_init=lam_init)

    def sub(s, col):
        return pl.BlockSpec((None, X0, HEAD_DIM),
                            lambda b_, h, i: (b_, jnp.minimum(AT_SUB * i + s, nblk - 1), col * hb + h))

    return pl.pallas_call(
        kern,
        grid=(b, HEADS, AT_TILES),
        in_specs=[
            pl.BlockSpec((4, QK_DIM), lambda b_, h, i: (0, 0)),
            pl.BlockSpec((None, LP, HEAD_DIM), lambda b_, h, i: (b_, 0, C_AQ * hb + h)),
            pl.BlockSpec((None, LP, HEAD_DIM), lambda b_, h, i: (b_, 0, C_AK * hb + h)),
            pl.BlockSpec((None, LP, HEAD_DIM), lambda b_, h, i: (b_, 0, C_AV * hb + h)),
        ] + [sub(s, C_AZ) for s in range(AT_SUB)] + [
            pl.BlockSpec((1, HEAD_DIM), lambda b_, h, i: (0, 0)),
        ],
        out_specs=pl.BlockSpec((None, AT_TQ, HEAD_DIM), lambda b_, h, i: (b_, i, h)),
        out_shape=jax.ShapeDtypeStruct((b, LP, D_MODEL), BF16),
        scratch_shapes=[
            pltpu.VMEM((AT_LA, HEAD_DIM), BF16),
            pltpu.VMEM((AT_LA, 2 * HEAD_DIM), BF16),
            pltpu.VMEM((AT_LA, 2 * HEAD_DIM), BF16),
            pltpu.VMEM((AT_LA, LANES), BF16),
            pltpu.VMEM((AT_LA, LANES), BF16),
            pltpu.SMEM((1,), jnp.int32),
            pltpu.VMEM((2 * AT_TQ, LANES), F32),
            pltpu.VMEM((2 * AT_TQ, 2 * HEAD_DIM), F32),
        ],
        compiler_params=pltpu.CompilerParams(
            dimension_semantics=("arbitrary", "arbitrary", "arbitrary"),
            vmem_limit_bytes=56 * 1024 * 1024),
        name="diff_attn",
    )(lam_params, proj3, proj3, proj3, *([proj3] * AT_SUB), attn_norm_w)


DN_T = 2 * CHUNK
HALO = 16
DN_GROUP = 8


def _shift_matrix():
    r = jnp.arange((CONV_K - 1) * DN_T)
    src = HALO + r % DN_T - (r // DN_T + 1)
    return (src[:, None] == jnp.arange(HALO + DN_T)[None, :]).astype(BF16)


def _split3(x):
    hi = x.astype(BF16)
    r1 = x - hi.astype(F32)
    mid = r1.astype(BF16)
    lo = (r1 - mid.astype(F32)).astype(BF16)
    return hi, mid, lo


def _dn_prep_kernel(dq_ref, dk_ref, dv_ref, hq_ref, hk_ref, hv_ref, cw_ref, sh_ref, gcol_ref, grow_ref,
                    gpr_ref, gpc_ref, u_ref, w_ref, qd_ref, kd_ref, a_ref, cd_ref):
    i = pl.program_id(1)
    t = DN_T
    width = D_MODEL
    cw = cw_ref[...]
    shifter = sh_ref[...]
    ys = []
    for idx, (t_ref, h_ref) in enumerate(((dq_ref, hq_ref), (dk_ref, hk_ref), (dv_ref, hv_ref))):
        halo = h_ref[...]
        halo = jnp.where(i == 0, jnp.zeros_like(halo), halo)
        x = t_ref[...]
        shifted = _dot(shifter, jnp.concatenate([halo, x], axis=0))
        cwi = cw[:, idx * width:(idx + 1) * width]
        yi = cwi[CONV_K - 1:CONV_K] * x.astype(F32)
        for s in range(1, CONV_K):
            yi = yi + cwi[CONV_K - 1 - s:CONV_K - s] * shifted[(s - 1) * t:s * t]
        ys.append(yi * jax.nn.sigmoid(yi))
    y = jnp.concatenate(ys, axis=1)

    r = lax.broadcasted_iota(jnp.int32, (t, t), 0)
    c = lax.broadcasted_iota(jnp.int32, (t, t), 1)
    same = (r >= CHUNK) == (c >= CHUNK)
    incl = same & (c <= r)
    strict = same & (c < r)
    tri = jnp.where(incl, 1.0, 0.0).astype(BF16)
    tri_t = jnp.where(same & (r <= c), 1.0, 0.0).astype(BF16)
    blk = jnp.where(same, 1.0, 0.0).astype(BF16)
    eye = jnp.where(r == c, 1.0, 0.0).astype(F32)

    lane = lax.broadcasted_iota(jnp.int32, (t, LANES), 1)
    rowi = lax.broadcasted_iota(jnp.int32, (t, LANES), 0) + i * t
    gcol = gcol_ref[...]
    gpr = gpr_ref[...]
    is_g = (lane >= HEADS) & (lane < 2 * HEADS)
    live_c = rowi >= FRONT
    beta_c = jnp.where(live_c, jax.nn.sigmoid(gcol), 0.0)
    xg = gcol + gpr[1:2]
    sp = jnp.maximum(xg, 0.0) + jnp.log1p(jnp.exp(-jnp.abs(xg)))
    g_c = jnp.where(is_g & live_c, -jnp.exp(gpr[0:1]) * sp, 0.0)
    g3 = _split3(g_c)
    gc_c = _dot(tri, g3[0]) + _dot(tri, g3[1]) + _dot(tri, g3[2])
    gl_c = _dot(blk, g3[0]) + _dot(blk, g3[1]) + _dot(blk, g3[2])
    sub = lax.broadcasted_iota(jnp.int32, (2 * HEADS, t), 0)
    coli = lax.broadcasted_iota(jnp.int32, (2 * HEADS, t), 1) + i * t
    xr = grow_ref[...] + gpc_ref[1]
    spr = jnp.maximum(xr, 0.0) + jnp.log1p(jnp.exp(-jnp.abs(xr)))
    g_r = jnp.where((sub >= HEADS) & (coli >= FRONT), -jnp.exp(gpc_ref[0]) * spr, 0.0)
    gr3 = _split3(g_r)
    gc_r = _dot(gr3[0], tri_t) + _dot(gr3[1], tri_t) + _dot(gr3[2], tri_t)

    cd_ref[...] = jnp.exp(gl_c)

    def group(hs):
        sls = {h: slice(h * HEAD_DIM, (h + 1) * HEAD_DIM) for h in hs}
        q = {h: y[:, h * HEAD_DIM:(h + 1) * HEAD_DIM] for h in hs}
        k = {h: y[:, width + h * HEAD_DIM:width + (h + 1) * HEAD_DIM] for h in hs}
        v = {h: y[:, 2 * width + h * HEAD_DIM:2 * width + (h + 1) * HEAD_DIM] for h in hs}
        q = {h: x * lax.rsqrt(jnp.sum(x * x, axis=-1, keepdims=True) + 1e-6) * (HEAD_DIM ** -0.5)
             for h, x in q.items()}
        k = {h: x * lax.rsqrt(jnp.sum(x * x, axis=-1, keepdims=True) + 1e-6) for h, x in k.items()}
        beta = {h: beta_c[:, h:h + 1] for h in hs}
        gcc = {h: gc_c[:, HEADS + h:HEADS + h + 1] for h in hs}
        glc = {h: gl_c[:, HEADS + h:HEADS + h + 1] for h in hs}
        gcr = {h: gc_r[HEADS + h:HEADS + h + 1, :] for h in hs}
        dec = {h: jnp.where(incl, jnp.exp(jnp.where(incl, gcc[h] - gcr[h], 0.0)), 0.0) for h in hs}
        kb = {h: k[h] * beta[h] for h in hs}
        kt = {h: k[h].T.astype(BF16) for h in hs}
        m = {h: jnp.where(strict, _dot(kb[h].astype(BF16), kt[h]) * dec[h], 0.0) for h in hs}
        tinv = {h: eye - m[h] for h in hs}
        pwb = {h: m[h].astype(BF16) for h in hs}
        for _ in range(5):
            pw = {h: _dot(pwb[h], pwb[h]) for h in hs}
            pwb = {h: pw[h].astype(BF16) for h in hs}
            tinv = {h: tinv[h] + _dot(tinv[h].astype(BF16), pwb[h]) for h in hs}
        egc = {h: jnp.exp(gcc[h]) for h in hs}
        rhs = {h: jnp.concatenate([v[h] * beta[h], kb[h] * egc[h]], axis=1).astype(BF16) for h in hs}
        uw = {h: _dot(tinv[h].astype(BF16), rhs[h]) for h in hs}
        a_full = {h: _dot(q[h].astype(BF16), kt[h]) * dec[h] for h in hs}
        for h in hs:
            u_ref[:, sls[h]] = uw[h][:, :HEAD_DIM].astype(BF16)
            w_ref[:, sls[h]] = uw[h][:, HEAD_DIM:].astype(BF16)
            a_cmp = a_full[h] + pltpu.roll(a_full[h], CHUNK, axis=1)
            a_ref[:, h * CHUNK:(h + 1) * CHUNK] = a_cmp[:, :CHUNK].astype(BF16)
            qd_ref[:, sls[h]] = (q[h] * egc[h]).astype(BF16)
            kd_ref[:, sls[h]] = (k[h] * jnp.exp(glc[h] - gcc[h])).astype(BF16)

    for g in range(HEADS // DN_GROUP):
        group(range(g * DN_GROUP, (g + 1) * DN_GROUP))


def _dn_prep(proj3, conv_w2, gcol3, grow, gp_row, gp_col):
    b = proj3.shape[0]
    t = DN_T
    nt = LP // t
    hpb = t // HALO

    def tile(cb):
        return pl.BlockSpec((None, t, D_MODEL), lambda b_, i: (b_, i, cb))

    def halo(cb):
        return pl.BlockSpec((None, HALO, D_MODEL), lambda b_, i: (b_, jnp.maximum(i * hpb - 1, 0), cb))

    full = pl.BlockSpec((None, t, D_MODEL), lambda b_, i: (b_, i, 0))
    return pl.pallas_call(
        _dn_prep_kernel,
        grid=(b, nt),
        in_specs=[
            tile(C_DQ), tile(C_DK), tile(C_DV), halo(C_DQ), halo(C_DK), halo(C_DV),
            pl.BlockSpec((CONV_K, 3 * D_MODEL), lambda b_, i: (0, 0)),
            pl.BlockSpec(((CONV_K - 1) * t, HALO + t), lambda b_, i: (0, 0)),
            pl.BlockSpec((None, t, LANES), lambda b_, i: (b_, i, 0)),
            pl.BlockSpec((2 * HEADS, t), lambda b_, i: (0, b_ * nt + i)),
            pl.BlockSpec((2, LANES), lambda b_, i: (0, 0)),
            pl.BlockSpec((2, 2 * HEADS, t), lambda b_, i: (0, 0, 0)),
        ],
        out_specs=[
            full, full, full, full,
            pl.BlockSpec((None, t, HEADS * CHUNK), lambda b_, i: (b_, i, 0)),
            pl.BlockSpec((None, t, LANES), lambda b_, i: (b_, i, 0)),
        ],
        out_shape=[
            jax.ShapeDtypeStruct((b, LP, D_MODEL), BF16),
            jax.ShapeDtypeStruct((b, LP, D_MODEL), BF16),
            jax.ShapeDtypeStruct((b, LP, D_MODEL), BF16),
            jax.ShapeDtypeStruct((b, LP, D_MODEL), BF16),
            jax.ShapeDtypeStruct((b, LP, HEADS * CHUNK), BF16),
            jax.ShapeDtypeStruct((b, LP, LANES), F32),
        ],
        compiler_params=pltpu.CompilerParams(
            dimension_semantics=("parallel", "parallel"),
            vmem_limit_bytes=40 * 1024 * 1024),
        name="dn_prep",
    )(proj3, proj3, proj3, proj3, proj3, proj3, conv_w2, _shift_matrix(), gcol3, grow, gp_row, gp_col)


SC_T = 6 * CHUNK


def _dn_scan_kernel(u_ref, w_ref, qd_ref, kd_ref, a_ref, cd_ref, z_ref, nw_ref, o_ref, s_sc):
    @pl.when(pl.program_id(1) == 0)
    def _():
        s_sc[...] = jnp.zeros(s_sc.shape, F32)

    nw = nw_ref[...]
    hs = range(HEADS)
    sls = [slice(h * HEAD_DIM, (h + 1) * HEAD_DIM) for h in hs]
    s = [s_sc[h] for h in hs]
    for cix in range(SC_T // CHUNK):
        rs = slice(cix * CHUNK, (cix + 1) * CHUNK)
        sb = [x.astype(BF16) for x in s]
        ws_qs = [_dot(jnp.concatenate([w_ref[rs, sls[h]], qd_ref[rs, sls[h]]], axis=0), sb[h]) for h in hs]
        vb = [(u_ref[rs, sls[h]].astype(F32) - ws_qs[h][:CHUNK]).astype(BF16) for h in hs]
        o = [ws_qs[h][CHUNK:] + _dot(a_ref[rs, h * CHUNK:(h + 1) * CHUNK], vb[h]) for h in hs]
        cd = [cd_ref[cix * CHUNK:cix * CHUNK + 1, HEADS + h:HEADS + h + 1] for h in hs]
        s = [s[h] * cd[h] + _dot_tn(kd_ref[rs, sls[h]], vb[h]) for h in hs]
        for h in hs:
            z = z_ref[rs, sls[h]].astype(F32)
            o_ref[rs, sls[h]] = (_rms(o[h], nw) * (z * jax.nn.sigmoid(z))).astype(BF16)
    for h in hs:
        s_sc[h] = s[h]


def _dn_scan(u, w, qd, kd, a, cd, proj3, dn_norm_w):
    b = u.shape[0]
    t = SC_T
    full = pl.BlockSpec((None, t, D_MODEL), lambda b_, i: (b_, i, 0))
    return pl.pallas_call(
        _dn_scan_kernel,
        grid=(b, LP // t),
        in_specs=[
            full, full, full, full,
            pl.BlockSpec((None, t, HEADS * CHUNK), lambda b_, i: (b_, i, 0)),
            pl.BlockSpec((None, t, LANES), lambda b_, i: (b_, i, 0)),
            pl.BlockSpec((None, t, D_MODEL), lambda b_, i: (b_, i, C_DZ)),
            pl.BlockSpec((1, HEAD_DIM), lambda b_, i: (0, 0)),
        ],
        out_specs=full,
        out_shape=jax.ShapeDtypeStruct((b, LP, D_MODEL), BF16),
        scratch_shapes=[pltpu.VMEM((HEADS, HEAD_DIM, HEAD_DIM), F32)],
        compiler_params=pltpu.CompilerParams(
            dimension_semantics=("parallel", "arbitrary"),
            vmem_limit_bytes=40 * 1024 * 1024),
        name="dn_scan",
    )(u, w, qd, kd, a, cd, proj3, dn_norm_w)


OUT_SUB = 2


def _merge_kernel(*refs):
    n = OUT_SUB
    xa_refs, xd_refs, ga_refs, gd_refs = refs[0:n], refs[n:2 * n], refs[2 * n:3 * n], refs[3 * n:4 * n]
    x_ref, wa_ref, wd_ref, wo_ref, nw_ref, o_ref = refs[4 * n:]

    def rows(rs):
        return jnp.concatenate([r[...] for r in rs], axis=0)

    ya = _dot(rows(xa_refs), wa_ref[...])
    yd = _dot(rows(xd_refs), wd_ref[...])
    merged = (jax.nn.sigmoid(rows(ga_refs).astype(F32)) * ya
              + jax.nn.sigmoid(rows(gd_refs).astype(F32)) * yd)
    out = x_ref[...] + _dot(merged.astype(BF16), wo_ref[...])
    o_ref[...] = _rms(out, nw_ref[...])


def _merge(xa, xd, proj3, x, wa, wd, wo, final_norm_w):
    b = x.shape[0]
    tm = OUT_SUB * X0

    def padded(cb):
        return [pl.BlockSpec((None, X0, D_MODEL), lambda b_, i, s=s: (b_, OUT_SUB * i + 1 + s, cb))
                for s in range(OUT_SUB)]

    tok = pl.BlockSpec((None, tm, D_MODEL), lambda b_, i: (b_, i, 0))
    wspec = pl.BlockSpec((D_MODEL, D_MODEL), lambda b_, i: (0, 0))
    return pl.pallas_call(
        _merge_kernel,
        grid=(b, SEQ // tm),
        in_specs=padded(0) + padded(0) + padded(C_GA) + padded(C_GD) + [
            tok, wspec, wspec, wspec,
            pl.BlockSpec((1, D_MODEL), lambda b_, i: (0, 0)),
        ],
        out_specs=tok,
        out_shape=jax.ShapeDtypeStruct((b, SEQ, D_MODEL), F32),
        compiler_params=pltpu.CompilerParams(
            dimension_semantics=("parallel", "parallel"),
            vmem_limit_bytes=48 * 1024 * 1024),
        name="merge_out",
    )(*([xa] * OUT_SUB + [xd] * OUT_SUB + [proj3] * (2 * OUT_SUB)), x, wa, wd, wo, final_norm_w)


def kernel(x, meta_tokens, norm_w, w_in, lambda_q1, lambda_k1, lambda_q2, lambda_k2, attn_norm_w,
           conv_w, a_log, dt_bias, dn_norm_w, w_branch_attn, w_branch_delta, w_out, final_norm_w):
    b = x.shape[0]
    assert x.shape == (b, SEQ, D_MODEL) and norm_w.shape[0] == 1
    layer = 0
    lam_init = 0.8 - 0.6 * math.exp(-0.3 * layer)

    head = jnp.concatenate([jnp.zeros((FRONT, D_MODEL), x.dtype), meta_tokens.astype(x.dtype)], axis=0)

    wi = w_in[layer]
    gate0 = 8 * D_MODEL
    q_scale = QK_DIM ** -0.5 * math.log2(math.e)
    col_scale = jnp.concatenate([jnp.full((1, D_MODEL), q_scale, F32), jnp.ones((1, gate0 - D_MODEL), F32)], axis=1)
    w_all = wi.astype(BF16)
    w_tail = wi[:, gate0 + 2 * HEADS:].astype(BF16)
    w_gate = wi[:, gate0:gate0 + 2 * HEADS]
    w_gate_p = jnp.pad(w_gate, ((0, 0), (0, LANES - 2 * HEADS))).astype(BF16)
    zeros8 = jnp.zeros((HEADS,), F32)
    gp_row = jnp.stack([jnp.pad(jnp.concatenate([zeros8, a_log[layer]]), (0, LANES - 2 * HEADS)),
                        jnp.pad(jnp.concatenate([zeros8, dt_bias[layer]]), (0, LANES - 2 * HEADS))])
    gp_col = jnp.stack([jnp.concatenate([zeros8, a_log[layer]]),
                        jnp.concatenate([zeros8, dt_bias[layer]])])
    gp_col = jnp.broadcast_to(gp_col[:, :, None], (2, 2 * HEADS, DN_T)).astype(F32)
    lam_params = jnp.stack([lambda_q1[layer], lambda_k1[layer], lambda_q2[layer], lambda_k2[layer]])

    proj2, gcol, grow = _inproj(head, x, norm_w[layer][None], w_all, w_tail, col_scale, w_gate_p)
    proj3 = proj2.reshape(b, LP, N_MAIN)

    xa = _attention(proj3, lam_params, attn_norm_w[layer][None], lam_init)

    u, w, qd, kd, a, cd = _dn_prep(proj3, conv_w[layer], gcol.reshape(b, LP, LANES), grow, gp_row, gp_col)
    xd = _dn_scan(u, w, qd, kd, a, cd, proj3, dn_norm_w[layer][None])

    return _merge(xa, xd, proj3, x, w_branch_attn[layer].astype(BF16), w_branch_delta[layer].astype(BF16),
                  w_out[layer].astype(BF16), final_norm_w[None])
```
